```python
import math
import jax, jax.numpy as jnp
from jax import lax
import numpy as np

D_MODEL = 1024
BATCH = 8
SEQ = 4096
DEPTH = 2
DEC_BATCH = 1
DEC_SEQ = 16384
PAST_LEN = 128

GRID_W = 64
N_HEADS = 16
HEAD_DIM = D_MODEL // N_HEADS
WIN_H_MAX = 8
WIN_W = 16
Q_BLOCK_W = 16
K_BLOCK_W = 32
S5_GROUP = 16
S5_GROUPS = D_MODEL // S5_GROUP
S5_STATE = 64
DT_MIN = 1e-3
DT_MAX = 1e-1
D_FF = 4 * D_MODEL
N_MIXERS = 2
N_ATTN = (DEPTH + 1) // 2
N_SSM = DEPTH // 2
EPS = 1e-6
NEG_INF = -1e30

kernel_name = 'hybrid_natten_s5_encoder'


def rms_norm(x, g):
    xf = x.astype(jnp.float32)
    y = xf * lax.rsqrt(jnp.mean(xf * xf, axis=-1, keepdims=True) + EPS)
    return (y * g.astype(jnp.float32)).astype(x.dtype)


def neighbourhood_attention(h, w_qkv, w_o, rpb):
    bsz, L, _ = h.shape
    rows = L // GRID_W
    kh = min(WIN_H_MAX, rows)
    qkv = (h @ w_qkv).reshape(bsz, rows, GRID_W, 3, N_HEADS, HEAD_DIM)
    q = qkv[:, :, :, 0] * (HEAD_DIM ** -0.5)
    k = qkv[:, :, :, 1]
    v = qkv[:, :, :, 2]
    r = np.arange(rows)
    row_start = np.clip(r - kh // 2, 0, rows - kh)
    row_idx = row_start[:, None] + np.arange(kh)[None, :]
    dr = row_idx - r[:, None] + (WIN_H_MAX - 1)
    outs = []
    for j in range(GRID_W // Q_BLOCK_W):
        qc = j * Q_BLOCK_W + np.arange(Q_BLOCK_W)
        col_start = np.clip(qc - WIN_W // 2, 0, GRID_W - WIN_W)
        kb = int(np.clip(j * Q_BLOCK_W - WIN_W // 2, 0, GRID_W - K_BLOCK_W))
        kc = kb + np.arange(K_BLOCK_W)
        valid = (kc[None, :] >= col_start[:, None]) & (kc[None, :] < col_start[:, None] + WIN_W)
        dc = np.clip(kc[None, :] - qc[:, None], -(WIN_W - 1), WIN_W - 1) + (WIN_W - 1)
        q_blk = q[:, :, j * Q_BLOCK_W:(j + 1) * Q_BLOCK_W]
        k_blk = jnp.take(k[:, :, kb:kb + K_BLOCK_W], row_idx, axis=1)
        v_blk = jnp.take(v[:, :, kb:kb + K_BLOCK_W], row_idx, axis=1)
        s = jnp.einsum('brqhd,brkwhd->bhrqkw', q_blk, k_blk).astype(jnp.float32)
        bias = rpb[:, dr[:, None, :, None], dc[None, :, None, :]]
        s = s + bias[None].astype(jnp.float32)
        s = jnp.where(valid[:, None, :], s, NEG_INF)
        p = jax.nn.softmax(s.reshape(s.shape[:4] + (kh * K_BLOCK_W,)), axis=-1)
        p = p.reshape(s.shape).astype(v.dtype)
        outs.append(jnp.einsum('bhrqkw,brkwhd->brqhd', p, v_blk))
    o = jnp.concatenate(outs, axis=2).reshape(bsz, L, D_MODEL)
    return o @ w_o


def _complex_affine_combine(left, right):
    ar1, ai1, br1, bi1 = left
    ar2, ai2, br2, bi2 = right
    ar = ar2 * ar1 - ai2 * ai1
    ai = ar2 * ai1 + ai2 * ar1
    br = ar2 * br1 - ai2 * bi1 + br2
    bi = ar2 * bi1 + ai2 * br1 + bi2
    return ar, ai, br, bi


def s5_direction(u_g, a_re, a_im, log_dt, b_re, b_im, c_re, c_im, reverse):
    f32 = jnp.float32
    a_re = a_re.astype(f32)
    a_im = a_im.astype(f32)
    b_re = b_re.astype(f32)
    b_im = b_im.astype(f32)
    c_re = c_re.astype(f32)
    c_im = c_im.astype(f32)
    dt = jnp.exp(log_dt.astype(f32))[:, None]
    mag = jnp.exp(a_re * dt)
    lam_re = mag * jnp.cos(a_im * dt)
    lam_im = mag * jnp.sin(a_im * dt)
    den = a_re * a_re + a_im * a_im
    nr = lam_re - 1.0
    ni = lam_im
    zr = (nr * a_re + ni * a_im) / den
    zi = (ni * a_re - nr * a_im) / den
    bbar_re = zr[..., None] * b_re - zi[..., None] * b_im
    bbar_im = zr[..., None] * b_im + zi[..., None] * b_re
    bu_re = jnp.einsum('blgi,gpi->blgp', u_g, bbar_re)
    bu_im = jnp.einsum('blgi,gpi->blgp', u_g, bbar_im)
    lr = jnp.broadcast_to(lam_re, bu_re.shape)
    li = jnp.broadcast_to(lam_im, bu_im.shape)
    _, _, h_re, h_im = lax.associative_scan(_complex_affine_combine, (lr, li, bu_re, bu_im), reverse=reverse, axis=1)
    return jnp.einsum('blgp,gip->blgi', h_re, c_re) - jnp.einsum('blgp,gip->blgi', h_im, c_im)


def s5_mixer(h, a_re, a_im, log_dt, b_re, b_im, c_re, c_im, d_skip, w_glu):
    bsz, L, _ = h.shape
    u = h.astype(jnp.float32)
    u_g = u.reshape(bsz, L, S5_GROUPS, S5_GROUP)
    y = d_skip.astype(jnp.float32) * u
    for direction in range(2):
        y_dir = s5_direction(u_g, a_re[direction], a_im[direction], log_dt[direction],
                             b_re[direction], b_im[direction], c_re[direction], c_im[direction],
                             reverse=(direction == 1))
        y = y + y_dir.reshape(bsz, L, D_MODEL)
    g = jax.nn.gelu(y).astype(h.dtype)
    val, gate = jnp.split(g @ w_glu, 2, axis=-1)
    return val * jax.nn.sigmoid(gate)


def run_trunk(x, c, ada_w, ada_b, norm_gain, attn_w_qkv, attn_w_o, attn_rpb,
              s5_a_re, s5_a_im, s5_log_dt, s5_b_re, s5_b_im, s5_c_re, s5_c_im,
              s5_d, s5_w_glu, ffn_w1, ffn_w2):
    c_act = jax.nn.silu(c)
    for i in range(DEPTH):
        mod = c_act @ ada_w[i] + ada_b[i]
        sh_m, sc_m, g_m, sh_f, sc_f, g_f = [m[:, None, :] for m in jnp.split(mod, 6, axis=-1)]
        h = rms_norm(x, norm_gain[i, 0]) * (1.0 + sc_m) + sh_m
        j = i // N_MIXERS
        if i % N_MIXERS == 0:
            y = neighbourhood_attention(h, attn_w_qkv[j], attn_w_o[j], attn_rpb[j])
        else:
            y = s5_mixer(h, s5_a_re[j], s5_a_im[j], s5_log_dt[j], s5_b_re[j], s5_b_im[j],
                         s5_c_re[j], s5_c_im[j], s5_d[j], s5_w_glu[j])
        x = x + g_m * rms_norm(y, norm_gain[i, 1])
        h = rms_norm(x, norm_gain[i, 2]) * (1.0 + sc_f) + sh_f
        y = jnp.square(jax.nn.relu(h @ ffn_w1[i])) @ ffn_w2[i]
        x = x + g_f * rms_norm(y, norm_gain[i, 3])
    return x


def setup_inputs(seed: int = 0) -> dict:
    key = jax.random.key(seed)
    ks = jax.random.split(key, 24)
    f32 = jnp.float32
    D = D_MODEL
    n_idx = jnp.arange(S5_STATE, dtype=f32)
    return {
        'x_prompt': jax.random.normal(ks[0], (BATCH, SEQ, D), f32),
        'x_sample': jax.random.normal(ks[1], (DEC_BATCH, DEC_SEQ, D), f32),
        'c_prompt': jax.random.normal(ks[2], (BATCH, D), f32),
        'c_sample': jax.random.normal(ks[3], (DEC_BATCH, D), f32),
        'ada_w': jax.random.normal(ks[4], (DEPTH, D, 6 * D), f32) * (0.5 * D ** -0.5),
        'ada_b': jax.random.normal(ks[5], (DEPTH, 6 * D), f32) * 0.02,
        'norm_gain': 1.0 + 0.05 * jax.random.normal(ks[6], (DEPTH, 4, D), f32),
        'attn_w_qkv': jax.random.normal(ks[7], (N_ATTN, D, 3 * D), f32) * D ** -0.5,
        'attn_w_o': jax.random.normal(ks[8], (N_ATTN, D, D), f32) * D ** -0.5,
        'attn_rpb': jax.random.normal(ks[9], (N_ATTN, N_HEADS, 2 * WIN_H_MAX - 1, 2 * WIN_W - 1), f32) * 0.02,
        's5_a_re': -0.5 + 0.01 * jax.random.normal(ks[10], (N_SSM, 2, S5_GROUPS, S5_STATE), f32),
        's5_a_im': math.pi * n_idx + 0.01 * jax.random.normal(ks[11], (N_SSM, 2, S5_GROUPS, S5_STATE), f32),
        's5_log_dt': jax.random.uniform(ks[12], (N_SSM, 2, S5_GROUPS), f32, math.log(DT_MIN), math.log(DT_MAX)),
        's5_b_re': jax.random.normal(ks[13], (N_SSM, 2, S5_GROUPS, S5_STATE, S5_GROUP), f32) * (2 * S5_GROUP) ** -0.5,
        's5_b_im': jax.random.normal(ks[14], (N_SSM, 2, S5_GROUPS, S5_STATE, S5_GROUP), f32) * (2 * S5_GROUP) ** -0.5,
        's5_c_re': jax.random.normal(ks[15], (N_SSM, 2, S5_GROUPS, S5_GROUP, S5_STATE), f32) * (2 * S5_STATE) ** -0.5,
        's5_c_im': jax.random.normal(ks[16], (N_SSM, 2, S5_GROUPS, S5_GROUP, S5_STATE), f32) * (2 * S5_STATE) ** -0.5,
        's5_d': jax.random.normal(ks[17], (N_SSM, D), f32),
        's5_w_glu': jax.random.normal(ks[18], (N_SSM, D, 2 * D), f32) * D ** -0.5,
        'ffn_w1': jax.random.normal(ks[19], (DEPTH, D, D_FF), f32) * D ** -0.5,
        'ffn_w2': jax.random.normal(ks[20], (DEPTH, D_FF, D), f32) * D_FF ** -0.5,
    }


def reference(x_prompt, x_sample, c_prompt, c_sample, ada_w, ada_b, norm_gain,
              attn_w_qkv, attn_w_o, attn_rpb, s5_a_re, s5_a_im, s5_log_dt,
              s5_b_re, s5_b_im, s5_c_re, s5_c_im, s5_d, s5_w_glu, ffn_w1, ffn_w2):
    y_prompt = run_trunk(x_prompt, c_prompt, ada_w, ada_b, norm_gain, attn_w_qkv, attn_w_o, attn_rpb,
                         s5_a_re, s5_a_im, s5_log_dt, s5_b_re, s5_b_im, s5_c_re, s5_c_im,
                         s5_d, s5_w_glu, ffn_w1, ffn_w2)
    y_sample = run_trunk(x_sample, c_sample, ada_w, ada_b, norm_gain, attn_w_qkv, attn_w_o, attn_rpb,
                         s5_a_re, s5_a_im, s5_log_dt, s5_b_re, s5_b_im, s5_c_re, s5_c_im,
                         s5_d, s5_w_glu, ffn_w1, ffn_w2)
    return (y_prompt, y_sample)
```

```python
import functools
import math

import numpy as np
import jax
import jax.numpy as jnp
from jax import lax
from jax.experimental import pallas as pl
from jax.experimental.pallas import tpu as pltpu

D_MODEL = 1024
D_FF = 4 * D_MODEL
DEPTH = 2
EPS = 1e-6
NEG_INF = -1e30
LOG2E = math.log2(math.e)

GRID_W = 64
N_HEADS = 16
HEAD_DIM = D_MODEL // N_HEADS
WIN_H = 8
WIN_W = 16
ROWS_PER_TILE = 4
TILE_TOK = ROWS_PER_TILE * GRID_W
KEY_TILES = 3

S5_GROUP = 16
S5_GROUPS = D_MODEL // S5_GROUP
S5_STATE = 64
S5_CHUNK = 16
S5_SEGS = 8
S5_PAIRS = S5_GROUPS // 2
PAIR_W = 2 * S5_CHUNK * S5_GROUP
STATE_W = 2 * S5_STATE

VMEM_LIMIT = 56 * 1024 * 1024

_f32 = jnp.float32
_bf16 = jnp.bfloat16


def _cparams(sem):
    return pltpu.CompilerParams(dimension_semantics=sem, vmem_limit_bytes=VMEM_LIMIT)


def _resident(shape, index_map):
    return pl.BlockSpec(shape, index_map, pipeline_mode=pl.Buffered(1))


def _rms(x, gain):
    ms = jnp.mean(x * x, axis=-1, keepdims=True)
    return x * lax.rsqrt(ms + EPS) * gain


def _norm_mod(x, gain, scale, shift):
    return _rms(x, gain) * (1.0 + scale) + shift


def _mod_kernel(c_ref, w_ref, b_ref, o_ref):
    c = c_ref[...]
    act = c * jax.nn.sigmoid(c)
    o_ref[0] = jnp.dot(act, w_ref[0], preferred_element_type=_f32,
                       precision=lax.Precision.HIGHEST) + b_ref[0]


def _modulation(c_all, ada_w, ada_b):
    nb = c_all.shape[0]
    tn = 1536
    return pl.pallas_call(
        _mod_kernel,
        grid=(DEPTH, 6 * D_MODEL // tn),
        in_specs=[
            pl.BlockSpec((nb, D_MODEL), lambda i, n: (0, 0)),
            pl.BlockSpec((1, D_MODEL, tn), lambda i, n: (i, 0, n)),
            pl.BlockSpec((1, 1, tn), lambda i, n: (i, 0, n)),
        ],
        out_specs=pl.BlockSpec((1, nb, tn), lambda i, n: (i, 0, n)),
        out_shape=jax.ShapeDtypeStruct((DEPTH, nb, 6 * D_MODEL), _f32),
        compiler_params=_cparams(("arbitrary", "arbitrary")),
        name="adaln_mod",
    )(c_all, ada_w, ada_b.reshape(DEPTH, 1, 6 * D_MODEL))


def _qkv_kernel(x_ref, mod_ref, gain_ref, w_ref, o_ref):
    mod = mod_ref[0]
    h = _norm_mod(x_ref[0], gain_ref[0:1, :], mod[1:2, :], mod[0:1, :]).astype(_bf16)
    q_scale = HEAD_DIM ** -0.5 * LOG2E
    for part in range(3):
        cols = slice(part * D_MODEL, (part + 1) * D_MODEL)
        y = jnp.dot(h, w_ref[:, cols], preferred_element_type=_f32)
        if part == 0:
            y = y * q_scale
        o_ref[0, :, cols] = y.astype(_bf16)


def _qkv_proj(x, mod, gains, w_qkv, tm=512):
    b, l, _ = x.shape
    return pl.pallas_call(
        _qkv_kernel,
        grid=(b, l // tm),
        in_specs=[
            pl.BlockSpec((1, tm, D_MODEL), lambda i, j: (i, j, 0)),
            pl.BlockSpec((1, 6, D_MODEL), lambda i, j: (i, 0, 0)),
            _resident((4, D_MODEL), lambda i, j: (0, 0)),
            _resident((D_MODEL, 3 * D_MODEL), lambda i, j: (0, 0)),
        ],
        out_specs=pl.BlockSpec((1, tm, 3 * D_MODEL), lambda i, j: (i, j, 0)),
        out_shape=jax.ShapeDtypeStruct((b, l, 3 * D_MODEL), _bf16),
        compiler_params=_cparams(("parallel", "parallel")),
        name="norm_qkv",
    )(x, mod, gains, w_qkv)


def _normmod_kernel(x_ref, mod_ref, gain_ref, o_ref):
    mod = mod_ref[0]
    o_ref[0] = _norm_mod(x_ref[0], gain_ref[0:1, :], mod[1:2, :], mod[0:1, :]).astype(o_ref.dtype)


def _norm_mod_call(x, mod, gains, tm=1024):
    b, l, _ = x.shape
    return pl.pallas_call(
        _normmod_kernel,
        grid=(b, l // tm),
        in_specs=[
            pl.BlockSpec((1, tm, D_MODEL), lambda i, j: (i, j, 0)),
            pl.BlockSpec((1, 6, D_MODEL), lambda i, j: (i, 0, 0)),
            _resident((4, D_MODEL), lambda i, j: (0, 0)),
        ],
        out_specs=pl.BlockSpec((1, tm, D_MODEL), lambda i, j: (i, j, 0)),
        out_shape=jax.ShapeDtypeStruct((b, l, D_MODEL), _bf16),
        compiler_params=_cparams(("parallel", "parallel")),
        name="norm_mod",
    )(x, mod, gains)


def _attn_bias_table(rpb):
    i = np.arange(ROWS_PER_TILE)[:, None]
    kk = np.arange(KEY_TILES * ROWS_PER_TILE)[None, :]
    dr = np.stack([kk - i + 7, kk - i + 3, kk - i - 1])
    row_ok = np.stack([(kk < WIN_H) & (i >= 0), (kk >= i) & (kk < i + WIN_H), (kk >= 4) & (i >= 0)])
    dr = np.clip(dr, 0, 2 * WIN_H - 2)
    qc = np.arange(GRID_W)[:, None]
    kc = np.arange(GRID_W)[None, :]
    col_start = np.clip(qc - WIN_W // 2, 0, GRID_W - WIN_W)
    col_ok = (kc >= col_start) & (kc < col_start + WIN_W)
    dc = np.clip(kc - qc, -(WIN_W - 1), WIN_W - 1) + (WIN_W - 1)
    b = rpb[:, dr[:, :, None, :, None], dc[None, None, :, None, :]]
    ok = row_ok[:, :, None, :, None] & col_ok[None, None, :, None, :]
    b = jnp.where(ok[None], b.astype(_f32) * LOG2E, NEG_INF)
    b = jnp.transpose(b, (1, 0, 2, 3, 4, 5))
    return b.reshape(3, N_HEADS, TILE_TOK, KEY_TILES * TILE_TOK).astype(_bf16)


def _attn_kernel(q_ref, k0_ref, k1_ref, k2_ref, v0_ref, v1_ref, v2_ref, bias_ref, o_ref):
    lane = lax.broadcasted_iota(jnp.int32, (1, 128), 1)
    low = lane < HEAD_DIM
    head_mask = [low.astype(_bf16), (~low).astype(_bf16)]
    k_refs = (k0_ref, k1_ref, k2_ref)
    v_refs = (v0_ref, v1_ref, v2_ref)
    for hp in range(N_HEADS // 2):
        cols = slice(hp * 128, (hp + 1) * 128)
        qp = q_ref[0, :, cols]
        outs = []
        for hh in range(2):
            qm = qp * head_mask[hh]
            s = []
            for t in range(KEY_TILES):
                st = lax.dot_general(qm, k_refs[t][0, :, cols], (((1,), (1,)), ((), ())),
                                     preferred_element_type=_f32)
                s.append(st + bias_ref[0, 2 * hp + hh, :, t * TILE_TOK:(t + 1) * TILE_TOK].astype(_f32))
            m = jnp.max(jnp.maximum(jnp.maximum(s[0], s[1]), s[2]), axis=-1, keepdims=True)
            p = [jnp.exp2(st - m) for st in s]
            l = jnp.sum(p[0] + p[1] + p[2], axis=-1, keepdims=True)
            o = jnp.dot(p[0].astype(_bf16), v_refs[0][0, :, cols], preferred_element_type=_f32)
            for t in range(1, KEY_TILES):
                o = o + jnp.dot(p[t].astype(_bf16), v_refs[t][0, :, cols], preferred_element_type=_f32)
            outs.append(o / l)
        o_ref[0, :, cols] = jnp.where(low, outs[0], outs[1]).astype(o_ref.dtype)


def _attention(qkv, bias):
    b, l, _ = qkv.shape
    nt = l // TILE_TOK
    assert nt >= KEY_TILES

    def base(j):
        return jnp.clip(j - 1, 0, nt - KEY_TILES)

    def pattern(j):
        return jnp.where(j == 0, 0, jnp.where(j == nt - 1, 2, 1))

    def kv_spec(part, t):
        return pl.BlockSpec((1, TILE_TOK, D_MODEL), lambda i, j: (i, base(j) + t, part))

    return pl.pallas_call(
        _attn_kernel,
        grid=(b, nt),
        in_specs=[pl.BlockSpec((1, TILE_TOK, D_MODEL), lambda i, j: (i, j, 0))]
        + [kv_spec(1, t) for t in range(KEY_TILES)]
        + [kv_spec(2, t) for t in range(KEY_TILES)]
        + [pl.BlockSpec((1, N_HEADS, TILE_TOK, KEY_TILES * TILE_TOK), lambda i, j: (pattern(j), 0, 0, 0))],
        out_specs=pl.BlockSpec((1, TILE_TOK, D_MODEL), lambda i, j: (i, j, 0)),
        out_shape=jax.ShapeDtypeStruct((b, l, D_MODEL), _bf16),
        compiler_params=_cparams(("parallel", "arbitrary")),
        name="nbr_attention",
    )(qkv, qkv, qkv, qkv, qkv, qkv, qkv, bias)


def _s5_tables(a_re, a_im, log_dt, b_re, b_im, c_re, c_im):
    hi = lax.Precision.HIGHEST
    t = S5_CHUNK
    a_re, a_im, b_re, b_im, c_re, c_im = [v.astype(_f32) for v in (a_re, a_im, b_re, b_im, c_re, c_im)]
    dt = jnp.exp(log_dt.astype(_f32))[..., None]
    k = jnp.arange(t + 1, dtype=_f32)[:, None, None, None]
    mag = jnp.exp(k * (a_re * dt))
    ang = k * (a_im * dt)
    pw_re = mag * jnp.cos(ang)
    pw_im = mag * jnp.sin(ang)
    lam_re, lam_im = pw_re[1], pw_im[1]
    den = a_re * a_re + a_im * a_im
    nr = lam_re - 1.0
    ni = lam_im
    zr = (nr * a_re + ni * a_im) / den
    zi = (ni * a_re - nr * a_im) / den
    bb_re = zr[..., None] * b_re - zi[..., None] * b_im
    bb_im = zr[..., None] * b_im + zi[..., None] * b_re

    lb_re = pw_re[:t, ..., None] * bb_re - pw_im[:t, ..., None] * bb_im
    lb_im = pw_re[:t, ..., None] * bb_im + pw_im[:t, ..., None] * bb_re
    kern = (jnp.einsum('dgip,kdgpj->kdgij', c_re, lb_re, precision=hi)
            - jnp.einsum('dgip,kdgpj->kdgij', c_im, lb_im, precision=hi))
    s_idx = np.arange(t)[:, None]
    t_idx = np.arange(t)[None, :]
    m_dirs = []
    for d, lag in ((0, t_idx - s_idx), (1, s_idx - t_idx)):
        kd = kern[:, d]
        blk = kd[np.clip(lag, 0, t - 1)]
        blk = jnp.where((lag >= 0)[:, :, None, None, None], blk, 0.0)
        m_dirs.append(jnp.transpose(blk, (2, 0, 4, 1, 3)))
    m_sum = (m_dirs[0] + m_dirs[1]).reshape(S5_GROUPS, t * S5_GROUP, t * S5_GROUP).astype(_bf16)

    eye2 = jnp.eye(2, dtype=_f32)

    def pair_in(w):
        w = w.reshape(2, S5_PAIRS, 2, t * S5_GROUP, S5_STATE)
        w = w[:, :, :, :, None, :] * eye2[None, None, :, None, :, None]
        return w.reshape(2, S5_PAIRS, PAIR_W, STATE_W)

    def pair_out(v):
        v = v.reshape(2, S5_PAIRS, 2, S5_STATE, t * S5_GROUP)
        v = v[:, :, :, :, None, :] * eye2[None, None, :, None, :, None]
        return v.reshape(2, S5_PAIRS, STATE_W, PAIR_W)

    lb_re_t = jnp.transpose(lb_re, (1, 2, 0, 4, 3))
    lb_im_t = jnp.transpose(lb_im, (1, 2, 0, 4, 3))
    w_re = jnp.stack([lb_re_t[0, :, ::-1], lb_re_t[1]])
    w_im = jnp.stack([lb_im_t[0, :, ::-1], lb_im_t[1]])
    w_in = jnp.concatenate([pair_in(w_re), pair_in(w_im)], axis=-1).astype(_bf16)

    e_re = jnp.stack([pw_re[1:, 0], pw_re[1:, 1][::-1]], axis=0)
    e_im = jnp.stack([pw_im[1:, 0], pw_im[1:, 1][::-1]], axis=0)
    ce_re = c_re[:, None] * e_re[:, :, :, None, :] - c_im[:, None] * e_im[:, :, :, None, :]
    ce_im = c_re[:, None] * e_im[:, :, :, None, :] + c_im[:, None] * e_re[:, :, :, None, :]
    v_re = jnp.transpose(ce_re, (0, 2, 4, 1, 3))
    v_im = jnp.transpose(-ce_im, (0, 2, 4, 1, 3))
    v_out = jnp.concatenate([pair_out(v_re), pair_out(v_im)], axis=2).astype(_bf16)

    lam_c = jnp.concatenate([pw_re[t].reshape(2, S5_PAIRS, STATE_W),
                             pw_im[t].reshape(2, S5_PAIRS, STATE_W)], axis=-1)
    lam_t = jnp.broadcast_to(lam_c[:, :, None, :], (2, S5_PAIRS, S5_SEGS, 2 * STATE_W))
    return m_sum, w_in, v_out, lam_t


def _cmul(ar, ai, br, bi):
    return ar * br - ai * bi, ar * bi + ai * br


def _s5_kernel(chained, u_ref, m_ref, w_ref, v_ref, lam_ref, y_ref, x_scr, h_scr):
    rows = u_ref.shape[1]
    nc = rows // S5_SEGS
    rb = min(512, rows)
    sw = STATE_W

    def inc_body(r, carry):
        rs = pl.ds(pl.multiple_of(r * rb, rb), rb)
        u = u_ref[0, rs, :]
        for d in range(2):
            x_scr[d, rs, :] = jnp.dot(u, w_ref[d, 0], preferred_element_type=_f32)
        return carry

    lax.fori_loop(0, rows // rb, inc_body, 0)

    lam = [(lam_ref[d, 0, :, :sw], lam_ref[d, 0, :, sw:]) for d in range(2)]

    def row_of(d, c):
        cc = c if d == 0 else nc - 1 - c
        return pl.ds(pl.multiple_of(cc * S5_SEGS, S5_SEGS), S5_SEGS)

    def scan_body(c, carry):
        new = []
        for d in range(2):
            hr, hi = carry[d]
            rs = row_of(d, c)
            h_scr[d, rs, :sw] = hr
            h_scr[d, rs, sw:] = hi
            pr, pi = _cmul(lam[d][0], lam[d][1], hr, hi)
            new.append((pr + x_scr[d, rs, :sw], pi + x_scr[d, rs, sw:]))
        return tuple(new)

    zero = jnp.zeros((S5_SEGS, sw), _f32)
    ends = lax.fori_loop(0, nc, scan_body, ((zero, zero), (zero, zero)))

    if chained:
        def pow_body(c, carry):
            return tuple(_cmul(lam[d][0], lam[d][1], carry[d][0], carry[d][1]) for d in range(2))

        one = jnp.ones((S5_SEGS, sw), _f32)
        seg_decay = lax.fori_loop(0, nc, pow_body, ((one, zero), (one, zero)))
        h_in = []
        for d in range(2):
            er, ei = ends[d]
            dr, di = seg_decay[d][0][0:1], seg_decay[d][1][0:1]
            order = range(S5_SEGS) if d == 0 else range(S5_SEGS - 1, -1, -1)
            cr = jnp.zeros((1, sw), _f32)
            ci = jnp.zeros((1, sw), _f32)
            rows_r = [None] * S5_SEGS
            rows_i = [None] * S5_SEGS
            for sgm in order:
                rows_r[sgm], rows_i[sgm] = cr, ci
                pr, pi = _cmul(dr, di, cr, ci)
                cr, ci = pr + er[sgm:sgm + 1], pi + ei[sgm:sgm + 1]
            h_in.append((jnp.concatenate(rows_r, axis=0), jnp.concatenate(rows_i, axis=0)))

        def fix_body(c, carry):
            new = []
            for d in range(2):
                pr, pi = carry[d]
                rs = row_of(d, c)
                ar, ai = _cmul(pr, pi, h_in[d][0], h_in[d][1])
                h_scr[d, rs, :sw] = h_scr[d, rs, :sw] + ar
                h_scr[d, rs, sw:] = h_scr[d, rs, sw:] + ai
                new.append(_cmul(lam[d][0], lam[d][1], pr, pi))
            return tuple(new)

        lax.fori_loop(0, nc, fix_body, ((one, zero), (one, zero)))

    half = PAIR_W // 2

    def out_body(r, carry):
        rs = pl.ds(pl.multiple_of(r * rb, rb), rb)
        u = u_ref[0, rs, :]
        y = (jnp.dot(h_scr[0, rs, :].astype(_bf16), v_ref[0, 0], preferred_element_type=_f32)
             + jnp.dot(h_scr[1, rs, :].astype(_bf16), v_ref[1, 0], preferred_element_type=_f32))
        y_ref[0, rs, :half] = y[:, :half] + jnp.dot(u[:, :half], m_ref[0], preferred_element_type=_f32)
        y_ref[0, rs, half:] = y[:, half:] + jnp.dot(u[:, half:], m_ref[1], preferred_element_type=_f32)
        return carry

    lax.fori_loop(0, rows // rb, out_body, 0)


def _s5_core(u_t, tables, chained):
    m_sum, w_in, v_out, lam_t = tables
    _, rows, _ = u_t.shape
    return pl.pallas_call(
        functools.partial(_s5_kernel, chained),
        grid=(S5_PAIRS,),
        in_specs=[
            pl.BlockSpec((1, rows, PAIR_W), lambda p: (p, 0, 0)),
            pl.BlockSpec((2, PAIR_W // 2, PAIR_W // 2), lambda p: (p, 0, 0)),
            pl.BlockSpec((2, 1, PAIR_W, 2 * STATE_W), lambda p: (0, p, 0, 0)),
            pl.BlockSpec((2, 1, 2 * STATE_W, PAIR_W), lambda p: (0, p, 0, 0)),
            pl.BlockSpec((2, 1, S5_SEGS, 2 * STATE_W), lambda p: (0, p, 0, 0)),
        ],
        out_specs=pl.BlockSpec((1, rows, PAIR_W), lambda p: (p, 0, 0)),
        out_shape=jax.ShapeDtypeStruct((S5_PAIRS, rows, PAIR_W), _f32),
        scratch_shapes=[pltpu.VMEM((2, rows, 2 * STATE_W), _f32),
                        pltpu.VMEM((2, rows, 2 * STATE_W), _f32)],
        compiler_params=_cparams(("parallel",)),
        name="s5_core",
    )(u_t, m_sum, w_in, v_out, lam_t)


def _to_chunk_layout(u):
    b, l, _ = u.shape
    nc = b * l // (S5_SEGS * S5_CHUNK)
    u = u.reshape(S5_SEGS, nc, S5_CHUNK, S5_PAIRS, 2, S5_GROUP)
    u = jnp.transpose(u, (3, 1, 0, 4, 2, 5))
    return u.reshape(S5_PAIRS, nc * S5_SEGS, PAIR_W)


def _from_chunk_layout(y, b, l):
    nc = b * l // (S5_SEGS * S5_CHUNK)
    y = y.reshape(S5_PAIRS, nc, S5_SEGS, 2, S5_CHUNK, S5_GROUP)
    y = jnp.transpose(y, (2, 1, 4, 0, 3, 5))
    return y.reshape(b, l, D_MODEL)


def _gelu_tanh(x):
    return 0.5 * x * (1.0 + jnp.tanh(math.sqrt(2.0 / math.pi) * (x + 0.044715 * (x * x * x))))


def _post_kernel(mixer, *refs):
    if mixer == "attn":
        x_ref, mix_ref, mod_ref, gain_ref, wp_ref, w1_ref, w2_ref, o_ref, acc_ref = refs
    else:
        x_ref, mix_ref, mod_ref, gain_ref, skip_ref, wp_ref, w1_ref, w2_ref, o_ref, acc_ref = refs
    x = x_ref[0]
    mod = mod_ref[0]
    sh_m, sc_m, g_m = mod[0:1, :], mod[1:2, :], mod[2:3, :]
    sh_f, sc_f, g_f = mod[3:4, :], mod[4:5, :], mod[5:6, :]
    if mixer == "attn":
        y = jnp.dot(mix_ref[0], wp_ref[...], preferred_element_type=_f32)
    else:
        u = _norm_mod(x, gain_ref[0:1, :], sc_m, sh_m)
        g = _gelu_tanh(skip_ref[...] * u + mix_ref[0]).astype(_bf16)
        vg = jnp.dot(g, wp_ref[...], preferred_element_type=_f32)
        y = vg[:, :D_MODEL] * jax.nn.sigmoid(vg[:, D_MODEL:])
    x1 = x + g_m * _rms(y, gain_ref[1:2, :])
    h2 = _norm_mod(x1, gain_ref[2:3, :], sc_f, sh_f).astype(_bf16)
    tf = 512
    for kf in range(D_FF // tf):
        cols = slice(kf * tf, (kf + 1) * tf)
        a = jnp.dot(h2, w1_ref[:, cols], preferred_element_type=_f32)
        a = jnp.square(jnp.maximum(a, 0.0)).astype(_bf16)
        part = jnp.dot(a, w2_ref[cols, :], preferred_element_type=_f32)
        if kf == 0:
            acc_ref[...] = part
        else:
            acc_ref[...] += part
    o_ref[0] = x1 + g_f * _rms(acc_ref[...], gain_ref[3:4, :])


def _post_mixer(mixer, x, mix, mod, gains, w_proj, w1, w2, skip=None, tm=512):
    b, l, _ = x.shape
    tok = lambda i, j: (i, j, 0)
    const = lambda i, j: (0, 0)
    in_specs = [
        pl.BlockSpec((1, tm, D_MODEL), tok),
        pl.BlockSpec((1, tm, D_MODEL), tok),
        pl.BlockSpec((1, 6, D_MODEL), lambda i, j: (i, 0, 0)),
        _resident((4, D_MODEL), const),
    ]
    args = [x, mix, mod, gains]
    if mixer == "s5":
        in_specs.append(_resident((1, D_MODEL), const))
        args.append(skip)
    in_specs += [_resident(w_proj.shape, const), _resident(w1.shape, const), _resident(w2.shape, const)]
    args += [w_proj, w1, w2]
    return pl.pallas_call(
        functools.partial(_post_kernel, mixer),
        grid=(b, l // tm),
        in_specs=in_specs,
        out_specs=pl.BlockSpec((1, tm, D_MODEL), tok),
        out_shape=jax.ShapeDtypeStruct((b, l, D_MODEL), _f32),
        scratch_shapes=[pltpu.VMEM((tm, D_MODEL), _f32)],
        compiler_params=_cparams(("parallel", "parallel")),
        name="post_" + mixer,
    )(*args)


def _trunk(x, mod, params, chained):
    b, l, _ = x.shape
    gains = params["norm_gain"]
    qkv = _qkv_proj(x, mod[0], gains[0], params["w_qkv"])
    att = _attention(qkv, params["attn_bias"])
    x = _post_mixer("attn", x, att, mod[0], gains[0], params["w_o"], params["ffn_w1"][0], params["ffn_w2"][0])
    u = _norm_mod_call(x, mod[1], gains[1])
    y = _s5_core(_to_chunk_layout(u), params["s5_tables"], chained)
    y = _from_chunk_layout(y, b, l)
    x = _post_mixer("s5", x, y, mod[1], gains[1], params["w_glu"], params["ffn_w1"][1], params["ffn_w2"][1],
                    skip=params["s5_d"])
    return x


def kernel(x_prompt, x_sample, c_prompt, c_sample, ada_w, ada_b, norm_gain, attn_w_qkv, attn_w_o, attn_rpb,
           s5_a_re, s5_a_im, s5_log_dt, s5_b_re, s5_b_im, s5_c_re, s5_c_im, s5_d, s5_w_glu, ffn_w1, ffn_w2):
    nbp, nbs = c_prompt.shape[0], c_sample.shape[0]
    nb = -(-(nbp + nbs) // 8) * 8
    c_all = jnp.concatenate([c_prompt, c_sample, jnp.zeros((nb - nbp - nbs, D_MODEL), _f32)], axis=0)
    mod = _modulation(c_all, ada_w, ada_b)
    mod_p = mod[:, :nbp].reshape(DEPTH, nbp, 6, D_MODEL)
    mod_s = mod[:, nbp:nbp + nbs].reshape(DEPTH, nbs, 6, D_MODEL)

    params = {
        "norm_gain": norm_gain,
        "w_qkv": attn_w_qkv[0].astype(_bf16),
        "w_o": attn_w_o[0].astype(_bf16),
        "attn_bias": _attn_bias_table(attn_rpb[0]),
        "s5_tables": _s5_tables(s5_a_re[0], s5_a_im[0], s5_log_dt[0], s5_b_re[0], s5_b_im[0],
                                s5_c_re[0], s5_c_im[0]),
        "s5_d": s5_d[0].reshape(1, D_MODEL),
        "w_glu": s5_w_glu[0].astype(_bf16),
        "ffn_w1": ffn_w1.astype(_bf16),
        "ffn_w2": ffn_w2.astype(_bf16),
    }
    y_prompt = _trunk(x_prompt, mod_p, params, chained=False)
    y_sample = _trunk(x_sample, mod_s, params, chained=True)
    return (y_prompt, y_sample)
```

```python
import functools
import math

import numpy as np
import jax
import jax.numpy as jnp
from jax import lax
from jax.experimental import pallas as pl
from jax.experimental.pallas import tpu as pltpu

D_MODEL = 1024
D_FF = 4 * D_MODEL
DEPTH = 2
EPS = 1e-6
NEG_INF = -1e30
LOG2E = math.log2(math.e)

GRID_W = 64
N_HEADS = 16
HEAD_DIM = D_MODEL // N_HEADS
WIN_H = 8
WIN_W = 16
ROWS_PER_TILE = 4
TILE_TOK = ROWS_PER_TILE * GRID_W
KEY_TILES = 3

S5_GROUP = 16
S5_GROUPS = D_MODEL // S5_GROUP
S5_STATE = 64
S5_CHUNK = 16
S5_SEGS = 8
S5_PAIRS = S5_GROUPS // 2
PAIR_W = 2 * S5_CHUNK * S5_GROUP
STATE_W = 2 * S5_STATE

VMEM_LIMIT = 56 * 1024 * 1024

_f32 = jnp.float32
_bf16 = jnp.bfloat16


def _cparams(sem):
    return pltpu.CompilerParams(dimension_semantics=sem, vmem_limit_bytes=VMEM_LIMIT)


def _resident(shape, index_map):
    return pl.BlockSpec(shape, index_map, pipeline_mode=pl.Buffered(1))


def _rms(x, gain):
    ms = jnp.mean(x * x, axis=-1, keepdims=True)
    return x * lax.rsqrt(ms + EPS) * gain


def _norm_mod(x, gain, scale, shift):
    return _rms(x, gain) * (1.0 + scale) + shift


def _mod_kernel(c_ref, w_ref, b_ref, o_ref):
    c = c_ref[...]
    act = c * jax.nn.sigmoid(c)
    o_ref[0] = jnp.dot(act, w_ref[0], preferred_element_type=_f32,
                       precision=lax.Precision.HIGHEST) + b_ref[0]


def _modulation(c_all, ada_w, ada_b):
    nb = c_all.shape[0]
    tn = 1536
    return pl.pallas_call(
        _mod_kernel,
        grid=(DEPTH, 6 * D_MODEL // tn),
        in_specs=[
            pl.BlockSpec((nb, D_MODEL), lambda i, n: (0, 0)),
            pl.BlockSpec((1, D_MODEL, tn), lambda i, n: (i, 0, n)),
            pl.BlockSpec((1, 1, tn), lambda i, n: (i, 0, n)),
        ],
        out_specs=pl.BlockSpec((1, nb, tn), lambda i, n: (i, 0, n)),
        out_shape=jax.ShapeDtypeStruct((DEPTH, nb, 6 * D_MODEL), _f32),
        compiler_params=_cparams(("arbitrary", "arbitrary")),
        name="adaln_mod",
    )(c_all, ada_w, ada_b.reshape(DEPTH, 1, 6 * D_MODEL))


def _qkv_kernel(x_ref, mod_ref, gain_ref, w_ref, o_ref):
    mod = mod_ref[0]
    h = _norm_mod(x_ref[0], gain_ref[0:1, :], mod[1:2, :], mod[0:1, :]).astype(_bf16)
    q_scale = HEAD_DIM ** -0.5 * LOG2E
    for part in range(3):
        cols = slice(part * D_MODEL, (part + 1) * D_MODEL)
        y = jnp.dot(h, w_ref[:, cols], preferred_element_type=_f32)
        if part == 0:
            y = y * q_scale
        o_ref[0, :, cols] = y.astype(_bf16)


def _qkv_proj(x, mod, gains, w_qkv, tm=512):
    b, l, _ = x.shape
    return pl.pallas_call(
        _qkv_kernel,
        grid=(b, l // tm),
        in_specs=[
            pl.BlockSpec((1, tm, D_MODEL), lambda i, j: (i, j, 0)),
            pl.BlockSpec((1, 6, D_MODEL), lambda i, j: (i, 0, 0)),
            _resident((4, D_MODEL), lambda i, j: (0, 0)),
            _resident((D_MODEL, 3 * D_MODEL), lambda i, j: (0, 0)),
        ],
        out_specs=pl.BlockSpec((1, tm, 3 * D_MODEL), lambda i, j: (i, j, 0)),
        out_shape=jax.ShapeDtypeStruct((b, l, 3 * D_MODEL), _bf16),
        compiler_params=_cparams(("parallel", "parallel")),
        name="norm_qkv",
    )(x, mod, gains, w_qkv)


def _lane_group_masks():
    group = lax.broadcasted_iota(jnp.int32, (1, 128), 1) // S5_GROUP
    return {d: (group & d) != 0 for d in (4, 2, 1)}


def _block_transpose8(v, masks):
    for d in (4, 2, 1):
        new = list(v)
        for a in range(8):
            if a & d:
                continue
            b = a + d
            new[a] = jnp.where(masks[d], pltpu.roll(v[b], S5_GROUP * d, 1), v[a])
            new[b] = jnp.where(masks[d], v[b], pltpu.roll(v[a], 128 - S5_GROUP * d, 1))
        v = new
    return v


def _s5_in_kernel(x_ref, mod_ref, gain_ref, o_ref, scr):
    tmc = x_ref.shape[1]
    for seg in range(S5_SEGS):
        mod = mod_ref[seg]
        u = _norm_mod(x_ref[seg], gain_ref[0:1, :], mod[1:2, :], mod[0:1, :])
        for lt in range(D_MODEL // 128):
            scr[lt, pl.ds(seg, tmc, stride=S5_SEGS), :] = u[:, lt * 128:(lt + 1) * 128]
    masks = _lane_group_masks()

    def body(cp, carry):
        out_rows = pl.ds(pl.multiple_of(cp * 2 * S5_SEGS, 2 * S5_SEGS), 2 * S5_SEGS)
        for lt in range(D_MODEL // 128):
            for h in range(2):
                w = []
                for cc in range(2):
                    tok0 = (cp * 2 + cc) * S5_CHUNK + 8 * h
                    v = [scr[lt, pl.ds(pl.multiple_of((tok0 + s) * S5_SEGS, S5_SEGS), S5_SEGS), :] for s in range(8)]
                    w.append(_block_transpose8(v, masks))
                for gl in range(8):
                    g = 8 * lt + gl
                    col = (g % 2) * (PAIR_W // 2) + h * 128
                    blk = jnp.concatenate([w[0][gl], w[1][gl]], axis=0)
                    o_ref[g // 2, out_rows, col:col + 128] = blk.astype(o_ref.dtype)
        return carry

    lax.fori_loop(0, tmc // (2 * S5_CHUNK), body, 0)


def _s5_input(x8, mod8, gains, tmc=128):
    _, ls, _ = x8.shape
    rows_blk = tmc // S5_CHUNK * S5_SEGS
    return pl.pallas_call(
        _s5_in_kernel,
        grid=(ls // tmc,),
        in_specs=[
            pl.BlockSpec((S5_SEGS, tmc, D_MODEL), lambda j: (0, j, 0)),
            _resident((S5_SEGS, 6, D_MODEL), lambda j: (0, 0, 0)),
            _resident((4, D_MODEL), lambda j: (0, 0)),
        ],
        out_specs=pl.BlockSpec((S5_PAIRS, rows_blk, PAIR_W), lambda j: (0, j, 0)),
        out_shape=jax.ShapeDtypeStruct((S5_PAIRS, ls // S5_CHUNK * S5_SEGS, PAIR_W), _bf16),
        scratch_shapes=[pltpu.VMEM((D_MODEL // 128, tmc * S5_SEGS, 128), _f32)],
        compiler_params=_cparams(("parallel",)),
        name="s5_input",
    )(x8, mod8, gains)


def _attn_bias_table(rpb):
    n_dr = 2 * WIN_H - 1
    i = np.arange(ROWS_PER_TILE)[:, None]
    kk = np.arange(KEY_TILES * ROWS_PER_TILE)[None, :]
    dr = np.stack([kk - i + 7, kk - i + 3, kk - i - 1])
    row_ok = np.stack([(kk < WIN_H) & (i >= 0), (kk >= i) & (kk < i + WIN_H), (kk >= 4) & (i >= 0)])
    dr = np.where(row_ok, dr, n_dr)
    qc = np.arange(GRID_W)[:, None]
    kc = np.arange(GRID_W)[None, :]
    col_start = np.clip(qc - WIN_W // 2, 0, GRID_W - WIN_W)
    col_ok = (kc >= col_start) & (kc < col_start + WIN_W)
    pad = GRID_W - WIN_W
    rp = jnp.pad(rpb.astype(_f32) * LOG2E, ((0, 0), (0, 0), (pad, pad)))
    t = jnp.stack([rp[:, :, GRID_W - 1 - q:2 * GRID_W - 1 - q] for q in range(GRID_W)], axis=2)
    t = jnp.where(col_ok[None, None], t, NEG_INF)
    t = jnp.concatenate([t, jnp.full((N_HEADS, 1, GRID_W, GRID_W), NEG_INF, _f32)], axis=1).astype(_bf16)
    even = jnp.take(t, dr[:, :, 0::2].reshape(-1), axis=1)
    odd = jnp.take(t, dr[:, :, 1::2].reshape(-1), axis=1)
    b = jnp.concatenate([even, odd], axis=-1)
    b = b.reshape(N_HEADS, 3, ROWS_PER_TILE, KEY_TILES * 2, GRID_W, 2 * GRID_W)
    return jnp.transpose(b, (1, 0, 2, 3, 4, 5))


def _attn_kernel(q_ref, k0_ref, k1_ref, k2_ref, v0_ref, v1_ref, v2_ref, bias_ref, o_ref):
    lane = lax.broadcasted_iota(jnp.int32, (1, 128), 1)
    low = lane < HEAD_DIM
    head_mask = [low.astype(_bf16), (~low).astype(_bf16)]
    k_refs = (k0_ref, k1_ref, k2_ref)
    v_refs = (v0_ref, v1_ref, v2_ref)
    for hp in range(N_HEADS // 2):
        cols = slice(hp * 128, (hp + 1) * 128)
        qp = q_ref[0, :, cols]
        outs = []
        for hh in range(2):
            qm = qp * head_mask[hh]
            s = []
            for t in range(KEY_TILES):
                st = lax.dot_general(qm, k_refs[t][0, :, cols], (((1,), (1,)), ((), ())),
                                     preferred_element_type=_f32)
                bias = jnp.concatenate(
                    [jnp.concatenate([bias_ref[0, 2 * hp + hh, i, 2 * t], bias_ref[0, 2 * hp + hh, i, 2 * t + 1]],
                                     axis=1) for i in range(ROWS_PER_TILE)], axis=0)
                s.append(st + bias.astype(_f32))
            m = jnp.max(jnp.maximum(jnp.maximum(s[0], s[1]), s[2]), axis=-1, keepdims=True)
            p = [jnp.exp2(st - m) for st in s]
            l = jnp.sum(p[0] + p[1] + p[2], axis=-1, keepdims=True)
            o = jnp.dot(p[0].astype(_bf16), v_refs[0][0, :, cols], preferred_element_type=_f32)
            for t in range(1, KEY_TILES):
                o = o + jnp.dot(p[t].astype(_bf16), v_refs[t][0, :, cols], preferred_element_type=_f32)
            outs.append(o / l)
        o_ref[0, :, cols] = jnp.where(low, outs[0], outs[1]).astype(o_ref.dtype)


def _attention(qkv, bias):
    b, l, _ = qkv.shape
    nt = l // TILE_TOK
    assert nt >= KEY_TILES

    def base(j):
        return jnp.clip(j - 1, 0, nt - KEY_TILES)

    def pattern(j):
        return jnp.where(j == 0, 0, jnp.where(j == nt - 1, 2, 1))

    def kv_spec(part, t):
        return pl.BlockSpec((1, TILE_TOK, D_MODEL), lambda i, j: (i, base(j) + t, part))

    return pl.pallas_call(
        _attn_kernel,
        grid=(b, nt),
        in_specs=[pl.BlockSpec((1, TILE_TOK, D_MODEL), lambda i, j: (i, j, 0))]
        + [kv_spec(1, t) for t in range(KEY_TILES)]
        + [kv_spec(2, t) for t in range(KEY_TILES)]
        + [pl.BlockSpec((1, N_HEADS, ROWS_PER_TILE, KEY_TILES * 2, GRID_W, 2 * GRID_W),
                        lambda i, j: (pattern(j), 0, 0, 0, 0, 0))],
        out_specs=pl.BlockSpec((1, TILE_TOK, D_MODEL), lambda i, j: (i, j, 0)),
        out_shape=jax.ShapeDtypeStruct((b, l, D_MODEL), _bf16),
        compiler_params=_cparams(("parallel", "arbitrary")),
        name="nbr_attention",
    )(qkv, qkv, qkv, qkv, qkv, qkv, qkv, bias)


def _s5_tables(a_re, a_im, log_dt, b_re, b_im, c_re, c_im):
    hi = lax.Precision.HIGHEST
    t = S5_CHUNK
    a_re, a_im, b_re, b_im, c_re, c_im = [v.astype(_f32) for v in (a_re, a_im, b_re, b_im, c_re, c_im)]
    dt = jnp.exp(log_dt.astype(_f32))[..., None]
    k = jnp.arange(t + 1, dtype=_f32)[:, None, None, None]
    mag = jnp.exp(k * (a_re * dt))
    ang = k * (a_im * dt)
    pw_re = mag * jnp.cos(ang)
    pw_im = mag * jnp.sin(ang)
    lam_re, lam_im = pw_re[1], pw_im[1]
    den = a_re * a_re + a_im * a_im
    nr = lam_re - 1.0
    ni = lam_im
    zr = (nr * a_re + ni * a_im) / den
    zi = (ni * a_re - nr * a_im) / den
    bb_re = zr[..., None] * b_re - zi[..., None] * b_im
    bb_im = zr[..., None] * b_im + zi[..., None] * b_re

    lb_re = pw_re[:t, ..., None] * bb_re - pw_im[:t, ..., None] * bb_im
    lb_im = pw_re[:t, ..., None] * bb_im + pw_im[:t, ..., None] * bb_re
    kern = (jnp.einsum('dgip,kdgpj->kdgij', c_re, lb_re, precision=hi)
            - jnp.einsum('dgip,kdgpj->kdgij', c_im, lb_im, precision=hi))
    s_idx = np.arange(t)[:, None]
    t_idx = np.arange(t)[None, :]
    m_dirs = []
    for d, lag in ((0, t_idx - s_idx), (1, s_idx - t_idx)):
        kd = kern[:, d]
        blk = kd[np.clip(lag, 0, t - 1)]
        blk = jnp.where((lag >= 0)[:, :, None, None, None], blk, 0.0)
        m_dirs.append(jnp.transpose(blk, (2, 0, 4, 1, 3)))
    m_sum = (m_dirs[0] + m_dirs[1]).reshape(S5_GROUPS, t * S5_GROUP, t * S5_GROUP).astype(_bf16)

    eye2 = jnp.eye(2, dtype=_f32)

    def pair_in(w):
        w = w.reshape(2, S5_PAIRS, 2, t * S5_GROUP, S5_STATE)
        w = w[:, :, :, :, None, :] * eye2[None, None, :, None, :, None]
        return w.reshape(2, S5_PAIRS, PAIR_W, STATE_W)

    def pair_out(v):
        v = v.reshape(2, S5_PAIRS, 2, S5_STATE, t * S5_GROUP)
        v = v[:, :, :, :, None, :] * eye2[None, None, :, None, :, None]
        return v.reshape(2, S5_PAIRS, STATE_W, PAIR_W)

    lb_re_t = jnp.transpose(lb_re, (1, 2, 0, 4, 3))
    lb_im_t = jnp.transpose(lb_im, (1, 2, 0, 4, 3))
    w_re = jnp.stack([lb_re_t[0, :, ::-1], lb_re_t[1]])
    w_im = jnp.stack([lb_im_t[0, :, ::-1], lb_im_t[1]])
    w_in = jnp.concatenate([pair_in(w_re), pair_in(w_im)], axis=-1).astype(_bf16)

    e_re = jnp.stack([pw_re[1:, 0], pw_re[1:, 1][::-1]], axis=0)
    e_im = jnp.stack([pw_im[1:, 0], pw_im[1:, 1][::-1]], axis=0)
    ce_re = c_re[:, None] * e_re[:, :, :, None, :] - c_im[:, None] * e_im[:, :, :, None, :]
    ce_im = c_re[:, None] * e_im[:, :, :, None, :] + c_im[:, None] * e_re[:, :, :, None, :]
    v_re = jnp.transpose(ce_re, (0, 2, 4, 1, 3))
    v_im = jnp.transpose(-ce_im, (0, 2, 4, 1, 3))
    v_out = jnp.concatenate([pair_out(v_re), pair_out(v_im)], axis=2).astype(_bf16)

    lam_c = jnp.concatenate([pw_re[t].reshape(2, S5_PAIRS, STATE_W),
                             pw_im[t].reshape(2, S5_PAIRS, STATE_W)], axis=-1)
    lam_t = jnp.broadcast_to(lam_c[:, :, None, :], (2, S5_PAIRS, S5_SEGS, 2 * STATE_W))
    return m_sum, w_in, v_out, lam_t


def _cmul(ar, ai, br, bi):
    return ar * br - ai * bi, ar * bi + ai * br


def _s5_kernel(chained, u_ref, m_ref, w_ref, v_ref, lam_ref, y_ref, x_scr, h_scr):
    rows = u_ref.shape[1]
    nc = rows // S5_SEGS
    rb = min(512, rows)
    sw = STATE_W

    def inc_body(r, carry):
        rs = pl.ds(pl.multiple_of(r * rb, rb), rb)
        u = u_ref[0, rs, :]
        for d in range(2):
            x_scr[d, rs, :] = jnp.dot(u, w_ref[d, 0], preferred_element_type=_f32)
        return carry

    lax.fori_loop(0, rows // rb, inc_body, 0)

    lam = [(lam_ref[d, 0, :, :sw], lam_ref[d, 0, :, sw:]) for d in range(2)]

    def row_of(d, c):
        cc = c if d == 0 else nc - 1 - c
        return pl.ds(pl.multiple_of(cc * S5_SEGS, S5_SEGS), S5_SEGS)

    def scan_body(c, carry):
        new = []
        for d in range(2):
            hr, hi = carry[d]
            rs = row_of(d, c)
            h_scr[d, rs, :sw] = hr
            h_scr[d, rs, sw:] = hi
            pr, pi = _cmul(lam[d][0], lam[d][1], hr, hi)
            new.append((pr + x_scr[d, rs, :sw], pi + x_scr[d, rs, sw:]))
        return tuple(new)

    zero = jnp.zeros((S5_SEGS, sw), _f32)
    ends = lax.fori_loop(0, nc, scan_body, ((zero, zero), (zero, zero)))

    if chained:
        def pow_body(c, carry):
            return tuple(_cmul(lam[d][0], lam[d][1], carry[d][0], carry[d][1]) for d in range(2))

        one = jnp.ones((S5_SEGS, sw), _f32)
        seg_decay = lax.fori_loop(0, nc, pow_body, ((one, zero), (one, zero)))
        h_in = []
        for d in range(2):
            er, ei = ends[d]
            dr, di = seg_decay[d][0][0:1], seg_decay[d][1][0:1]
            order = range(S5_SEGS) if d == 0 else range(S5_SEGS - 1, -1, -1)
            cr = jnp.zeros((1, sw), _f32)
            ci = jnp.zeros((1, sw), _f32)
            rows_r = [None] * S5_SEGS
            rows_i = [None] * S5_SEGS
            for sgm in order:
                rows_r[sgm], rows_i[sgm] = cr, ci
                pr, pi = _cmul(dr, di, cr, ci)
                cr, ci = pr + er[sgm:sgm + 1], pi + ei[sgm:sgm + 1]
            h_in.append((jnp.concatenate(rows_r, axis=0), jnp.concatenate(rows_i, axis=0)))

        def fix_body(c, carry):
            new = []
            for d in range(2):
                pr, pi = carry[d]
                rs = row_of(d, c)
                ar, ai = _cmul(pr, pi, h_in[d][0], h_in[d][1])
                h_scr[d, rs, :sw] = h_scr[d, rs, :sw] + ar
                h_scr[d, rs, sw:] = h_scr[d, rs, sw:] + ai
                new.append(_cmul(lam[d][0], lam[d][1], pr, pi))
            return tuple(new)

        lax.fori_loop(0, nc, fix_body, ((one, zero), (one, zero)))

    half = PAIR_W // 2

    def out_body(r, carry):
        rs = pl.ds(pl.multiple_of(r * rb, rb), rb)
        u = u_ref[0, rs, :]
        y = (jnp.dot(h_scr[0, rs, :].astype(_bf16), v_ref[0, 0], preferred_element_type=_f32)
             + jnp.dot(h_scr[1, rs, :].astype(_bf16), v_ref[1, 0], preferred_element_type=_f32))
        y_ref[0, rs, :half] = y[:, :half] + jnp.dot(u[:, :half], m_ref[0], preferred_element_type=_f32)
        y_ref[0, rs, half:] = y[:, half:] + jnp.dot(u[:, half:], m_ref[1], preferred_element_type=_f32)
        return carry

    lax.fori_loop(0, rows // rb, out_body, 0)


def _s5_core(u_t, tables, chained):
    m_sum, w_in, v_out, lam_t = tables
    _, rows, _ = u_t.shape
    return pl.pallas_call(
        functools.partial(_s5_kernel, chained),
        grid=(S5_PAIRS,),
        in_specs=[
            pl.BlockSpec((1, rows, PAIR_W), lambda p: (p, 0, 0)),
            pl.BlockSpec((2, PAIR_W // 2, PAIR_W // 2), lambda p: (p, 0, 0)),
            pl.BlockSpec((2, 1, PAIR_W, 2 * STATE_W), lambda p: (0, p, 0, 0)),
            pl.BlockSpec((2, 1, 2 * STATE_W, PAIR_W), lambda p: (0, p, 0, 0)),
            pl.BlockSpec((2, 1, S5_SEGS, 2 * STATE_W), lambda p: (0, p, 0, 0)),
        ],
        out_specs=pl.BlockSpec((1, rows, PAIR_W), lambda p: (p, 0, 0)),
        out_shape=jax.ShapeDtypeStruct((S5_PAIRS, rows, PAIR_W), _f32),
        scratch_shapes=[pltpu.VMEM((2, rows, 2 * STATE_W), _f32),
                        pltpu.VMEM((2, rows, 2 * STATE_W), _f32)],
        compiler_params=_cparams(("parallel",)),
        name="s5_core",
    )(u_t, m_sum, w_in, v_out, lam_t)


def _gelu_tanh(x):
    return 0.5 * x * (1.0 + jnp.tanh(math.sqrt(2.0 / math.pi) * (x + 0.044715 * (x * x * x))))


def _post_kernel(mixer, *refs):
    if mixer == "attn":
        x_ref, mix_ref, mod_ref, gain_ref, wp_ref, w1_ref, w2_ref, o_ref, x1_scr, h_scr, acc_ref = refs
    else:
        (x_ref, mix_ref, mod_ref, gain_ref, skip_ref, wp_ref, w1_ref, w2_ref, o_ref,
         x1_scr, h_scr, acc_ref, nat_scr) = refs
    tmc = x_ref.shape[1]

    def rows(seg):
        return slice(seg * tmc, (seg + 1) * tmc)

    if mixer == "attn":
        for seg in range(S5_SEGS):
            h_scr[rows(seg), :] = mix_ref[seg]
        y = jnp.dot(h_scr[...], wp_ref[...], preferred_element_type=_f32)
    else:
        masks = _lane_group_masks()

        def relayout(c, carry):
            in_rows = pl.ds(pl.multiple_of(c * S5_SEGS, S5_SEGS), S5_SEGS)
            for lt in range(D_MODEL // 128):
                for h in range(2):
                    w = []
                    for gl in range(8):
                        g = 8 * lt + gl
                        col = (g % 2) * (PAIR_W // 2) + h * 128
                        w.append(mix_ref[g // 2, in_rows, col:col + 128])
                    v = _block_transpose8(w, masks)
                    for s in range(8):
                        tok = c * S5_CHUNK + 8 * h + s
                        nat_scr[lt, pl.ds(pl.multiple_of(tok * S5_SEGS, S5_SEGS), S5_SEGS), :] = v[s]
            return carry

        lax.fori_loop(0, tmc // S5_CHUNK, relayout, 0)
        for seg in range(S5_SEGS):
            mod = mod_ref[seg]
            u = _norm_mod(x_ref[seg], gain_ref[0:1, :], mod[1:2, :], mod[0:1, :])
            y_ssm = jnp.concatenate([nat_scr[lt, pl.ds(seg, tmc, stride=S5_SEGS), :]
                                     for lt in range(D_MODEL // 128)], axis=1)
            h_scr[rows(seg), :] = _gelu_tanh(skip_ref[...] * u + y_ssm).astype(_bf16)
        g = h_scr[...]
        val = jnp.dot(g, wp_ref[:, :D_MODEL], preferred_element_type=_f32)
        gate = jnp.dot(g, wp_ref[:, D_MODEL:], preferred_element_type=_f32)
        y = val * jax.nn.sigmoid(gate)
    for seg in range(S5_SEGS):
        mod = mod_ref[seg]
        x1 = x_ref[seg] + mod[2:3, :] * _rms(y[rows(seg), :], gain_ref[1:2, :])
        x1_scr[rows(seg), :] = x1
        h_scr[rows(seg), :] = _norm_mod(x1, gain_ref[2:3, :], mod[4:5, :], mod[3:4, :]).astype(_bf16)
    h2 = h_scr[...]
    tf = 512
    for kf in range(D_FF // tf):
        cols = slice(kf * tf, (kf + 1) * tf)
        a = jnp.dot(h2, w1_ref[:, cols], preferred_element_type=_f32)
        a = jnp.square(jnp.maximum(a, 0.0)).astype(_bf16)
        part = jnp.dot(a, w2_ref[cols, :], preferred_element_type=_f32)
        if kf == 0:
            acc_ref[...] = part
        else:
            acc_ref[...] += part
    for seg in range(S5_SEGS):
        mod = mod_ref[seg]
        o_ref[seg] = x1_scr[rows(seg), :] + mod[5:6, :] * _rms(acc_ref[rows(seg), :], gain_ref[3:4, :])


def _post_mixer(mixer, x8, mix, mod8, gains, w_proj, w1, w2, skip=None, tmc=64):
    _, ls, _ = x8.shape
    tok = lambda j: (0, j, 0)
    const2 = lambda j: (0, 0)
    m = S5_SEGS * tmc
    in_specs = [pl.BlockSpec((S5_SEGS, tmc, D_MODEL), tok)]
    scratch = [pltpu.VMEM((m, D_MODEL), _f32), pltpu.VMEM((m, D_MODEL), _bf16), pltpu.VMEM((m, D_MODEL), _f32)]
    if mixer == "attn":
        in_specs.append(pl.BlockSpec((S5_SEGS, tmc, D_MODEL), tok))
    else:
        in_specs.append(pl.BlockSpec((S5_PAIRS, tmc // S5_CHUNK * S5_SEGS, PAIR_W), tok))
        scratch.append(pltpu.VMEM((D_MODEL // 128, m, 128), _f32))
    in_specs += [_resident((S5_SEGS, 6, D_MODEL), lambda j: (0, 0, 0)), _resident((4, D_MODEL), const2)]
    args = [x8, mix, mod8, gains]
    if mixer == "s5":
        in_specs.append(_resident((1, D_MODEL), const2))
        args.append(skip)
    in_specs += [_resident(w_proj.shape, const2), _resident(w1.shape, const2), _resident(w2.shape, const2)]
    args += [w_proj, w1, w2]
    return pl.pallas_call(
        functools.partial(_post_kernel, mixer),
        grid=(ls // tmc,),
        in_specs=in_specs,
        out_specs=pl.BlockSpec((S5_SEGS, tmc, D_MODEL), tok),
        out_shape=jax.ShapeDtypeStruct(x8.shape, _f32),
        scratch_shapes=scratch,
        compiler_params=_cparams(("parallel",)),
        name="post_" + mixer,
    )(*args)


def _trunk(x, mod, params, chained):
    b, l, _ = x.shape
    assert (b == 1) if chained else (b == S5_SEGS)
    ls = b * l // S5_SEGS
    gains = params["norm_gain"]
    mod8 = jnp.broadcast_to(mod, (DEPTH, S5_SEGS, 6, D_MODEL))
    qkv = _qkv_proj(x, mod[0], gains[0], params["w_qkv"])
    att = _attention(qkv, params["attn_bias"])
    x8 = _post_mixer("attn", x.reshape(S5_SEGS, ls, D_MODEL), att.reshape(S5_SEGS, ls, D_MODEL), mod8[0], gains[0],
                     params["w_o"], params["ffn_w1"][0], params["ffn_w2"][0])
    u = _s5_input(x8, mod8[1], gains[1])
    y = _s5_core(u, params["s5_tables"], chained)
    x8 = _post_mixer("s5", x8, y, mod8[1], gains[1], params["w_glu"], params["ffn_w1"][1], params["ffn_w2"][1],
                     skip=params["s5_d"])
    return x8.reshape(b, l, D_MODEL)


def kernel(x_prompt, x_sample, c_prompt, c_sample, ada_w, ada_b, norm_gain, attn_w_qkv, attn_w_o, attn_rpb,
           s5_a_re, s5_a_im, s5_log_dt, s5_b_re, s5_b_im, s5_c_re, s5_c_im, s5_d, s5_w_glu, ffn_w1, ffn_w2):
    nbp, nbs = c_prompt.shape[0], c_sample.shape[0]
    nb = -(-(nbp + nbs) // 8) * 8
    c_all = jnp.concatenate([c_prompt, c_sample, jnp.zeros((nb - nbp - nbs, D_MODEL), _f32)], axis=0)
    mod = _modulation(c_all, ada_w, ada_b)
    mod_p = mod[:, :nbp].reshape(DEPTH, nbp, 6, D_MODEL)
    mod_s = mod[:, nbp:nbp + nbs].reshape(DEPTH, nbs, 6, D_MODEL)

    params = {
        "norm_gain": norm_gain,
        "w_qkv": attn_w_qkv[0].astype(_bf16),
        "w_o": attn_w_o[0].astype(_bf16),
        "attn_bias": _attn_bias_table(attn_rpb[0]),
        "s5_tables": _s5_tables(s5_a_re[0], s5_a_im[0], s5_log_dt[0], s5_b_re[0], s5_b_im[0],
                                s5_c_re[0], s5_c_im[0]),
        "s5_d": s5_d[0].reshape(1, D_MODEL),
        "w_glu": s5_w_glu[0].astype(_bf16),
        "ffn_w1": ffn_w1.astype(_bf16),
        "ffn_w2": ffn_w2.astype(_bf16),
    }
    y_prompt = _trunk(x_prompt, mod_p, params, chained=False)
    y_sample = _trunk(x_sample, mod_s, params, chained=True)
    return (y_prompt, y_sample)
```

```python
import functools
import math

import numpy as np
import jax
import jax.numpy as jnp
from jax import lax
from jax.experimental import pallas as pl
from jax.experimental.pallas import tpu as pltpu

D_MODEL = 1024
D_FF = 4 * D_MODEL
DEPTH = 2
EPS = 1e-6
NEG_INF = -1e30
LOG2E = math.log2(math.e)

GRID_W = 64
N_HEADS = 16
HEAD_DIM = D_MODEL // N_HEADS
WIN_H = 8
WIN_W = 16
ROWS_PER_TILE = 4
TILE_TOK = ROWS_PER_TILE * GRID_W
KEY_TILES = 3

S5_GROUP = 16
S5_GROUPS = D_MODEL // S5_GROUP
S5_STATE = 64
S5_CHUNK = 16
S5_SEGS = 8
S5_PAIRS = S5_GROUPS // 2
PAIR_W = 2 * S5_CHUNK * S5_GROUP
STATE_W = 2 * S5_STATE

VMEM_LIMIT = 56 * 1024 * 1024

_f32 = jnp.float32
_bf16 = jnp.bfloat16


def _cparams(sem):
    return pltpu.CompilerParams(dimension_semantics=sem, vmem_limit_bytes=VMEM_LIMIT)


def _resident(shape, index_map):
    return pl.BlockSpec(shape, index_map, pipeline_mode=pl.Buffered(1))


def _rms(x, gain):
    ms = jnp.mean(x * x, axis=-1, keepdims=True)
    return x * lax.rsqrt(ms + EPS) * gain


def _norm_mod(x, gain, scale, shift):
    return _rms(x, gain) * (1.0 + scale) + shift


def _mod_kernel(c_ref, w_ref, b_ref, o_ref):
    c = c_ref[...]
    act = c * jax.nn.sigmoid(c)
    o_ref[0] = jnp.dot(act, w_ref[0], preferred_element_type=_f32,
                       precision=lax.Precision.HIGHEST) + b_ref[0]


def _modulation(c_all, ada_w, ada_b):
    nb = c_all.shape[0]
    tn = 1536
    return pl.pallas_call(
        _mod_kernel,
        grid=(DEPTH, 6 * D_MODEL // tn),
        in_specs=[
            pl.BlockSpec((nb, D_MODEL), lambda i, n: (0, 0)),
            pl.BlockSpec((1, D_MODEL, tn), lambda i, n: (i, 0, n)),
            pl.BlockSpec((1, 1, tn), lambda i, n: (i, 0, n)),
        ],
        out_specs=pl.BlockSpec((1, nb, tn), lambda i, n: (i, 0, n)),
        out_shape=jax.ShapeDtypeStruct((DEPTH, nb, 6 * D_MODEL), _f32),
        compiler_params=_cparams(("arbitrary", "arbitrary")),
        name="adaln_mod",
    )(c_all, ada_w, ada_b.reshape(DEPTH, 1, 6 * D_MODEL))


def _qkv_kernel(x_ref, mod_ref, gain_ref, w_ref, o_ref):
    mod = mod_ref[0]
    h = _norm_mod(x_ref[0], gain_ref[0:1, :], mod[1:2, :], mod[0:1, :]).astype(_bf16)
    q_scale = HEAD_DIM ** -0.5 * LOG2E
    for part in range(3):
        cols = slice(part * D_MODEL, (part + 1) * D_MODEL)
        y = jnp.dot(h, w_ref[:, cols], preferred_element_type=_f32)
        if part == 0:
            y = y * q_scale
        o_ref[0, :, cols] = y.astype(_bf16)


def _qkv_proj(x, mod, gains, w_qkv, tm=512):
    b, l, _ = x.shape
    return pl.pallas_call(
        _qkv_kernel,
        grid=(b, l // tm),
        in_specs=[
            pl.BlockSpec((1, tm, D_MODEL), lambda i, j: (i, j, 0)),
            pl.BlockSpec((1, 6, D_MODEL), lambda i, j: (i, 0, 0)),
            _resident((4, D_MODEL), lambda i, j: (0, 0)),
            _resident((D_MODEL, 3 * D_MODEL), lambda i, j: (0, 0)),
        ],
        out_specs=pl.BlockSpec((1, tm, 3 * D_MODEL), lambda i, j: (i, j, 0)),
        out_shape=jax.ShapeDtypeStruct((b, l, 3 * D_MODEL), _bf16),
        compiler_params=_cparams(("parallel", "parallel")),
        name="norm_qkv",
    )(x, mod, gains, w_qkv)


LANE_TILES = D_MODEL // 128


def _lane_perm(inverse=False):
    eye = np.eye
    p = np.einsum('sS,gG,iI->sgiGSI', eye(8), eye(8), eye(S5_GROUP)).reshape(D_MODEL, D_MODEL)
    return jnp.asarray(p.T if inverse else p, _bf16)


def _chunk_col(g, h):
    return (g % 2) * (PAIR_W // 2) + h * 128


def _s5_in_kernel(x_ref, mod_ref, gain_ref, p_ref, o_ref, ubuf, lhs):
    tmc = x_ref.shape[1]
    nck = tmc // S5_CHUNK
    blk = nck * S5_SEGS
    for seg in range(S5_SEGS):
        mod = mod_ref[seg]
        u = _norm_mod(x_ref[seg], gain_ref[0:1, :], mod[1:2, :], mod[0:1, :])
        for lt in range(LANE_TILES):
            ubuf[lt] = u[:, lt * 128:(lt + 1) * 128]
        for lt in range(LANE_TILES):
            for h in range(2):
                row0 = (lt * 2 + h) * blk + seg
                for s in range(8):
                    lhs[s, pl.ds(row0, nck, stride=S5_SEGS), :] = ubuf[lt, pl.ds(8 * h + s, nck, stride=S5_CHUNK), :]
    a = jnp.concatenate([lhs[s] for s in range(8)], axis=1).astype(_bf16)
    for gp in range(4):
        out = jnp.dot(a, p_ref[:, gp * 256:(gp + 1) * 256], preferred_element_type=_f32)
        for lt in range(LANE_TILES):
            for h in range(2):
                for gl in (2 * gp, 2 * gp + 1):
                    g = 8 * lt + gl
                    col = _chunk_col(g, h)
                    piece = out[(lt * 2 + h) * blk:(lt * 2 + h + 1) * blk, (gl % 2) * 128:(gl % 2 + 1) * 128]
                    o_ref[g // 2, :, col:col + 128] = piece.astype(o_ref.dtype)


def _s5_input(x8, mod8, gains, perm, tmc=128):
    _, ls, _ = x8.shape
    rows_blk = tmc // S5_CHUNK * S5_SEGS
    return pl.pallas_call(
        _s5_in_kernel,
        grid=(ls // tmc,),
        in_specs=[
            pl.BlockSpec((S5_SEGS, tmc, D_MODEL), lambda j: (0, j, 0)),
            _resident((S5_SEGS, 6, D_MODEL), lambda j: (0, 0, 0)),
            _resident((4, D_MODEL), lambda j: (0, 0)),
            _resident((D_MODEL, D_MODEL), lambda j: (0, 0)),
        ],
        out_specs=pl.BlockSpec((S5_PAIRS, rows_blk, PAIR_W), lambda j: (0, j, 0)),
        out_shape=jax.ShapeDtypeStruct((S5_PAIRS, ls // S5_CHUNK * S5_SEGS, PAIR_W), _bf16),
        scratch_shapes=[pltpu.VMEM((LANE_TILES, tmc, 128), _f32),
                        pltpu.VMEM((8, 2 * LANE_TILES * rows_blk, 128), _f32)],
        compiler_params=_cparams(("parallel",)),
        name="s5_input",
    )(x8, mod8, gains, perm)


def _attn_bias_table(rpb):
    n_dr = 2 * WIN_H - 1
    i = np.arange(ROWS_PER_TILE)[:, None]
    kk = np.arange(KEY_TILES * ROWS_PER_TILE)[None, :]
    dr = np.stack([kk - i + 7, kk - i + 3, kk - i - 1])
    row_ok = np.stack([(kk < WIN_H) & (i >= 0), (kk >= i) & (kk < i + WIN_H), (kk >= 4) & (i >= 0)])
    dr = np.where(row_ok, dr, n_dr)
    qc = np.arange(GRID_W)[:, None]
    kc = np.arange(GRID_W)[None, :]
    col_start = np.clip(qc - WIN_W // 2, 0, GRID_W - WIN_W)
    col_ok = (kc >= col_start) & (kc < col_start + WIN_W)
    pad = GRID_W - WIN_W
    rp = jnp.pad(rpb.astype(_f32) * LOG2E, ((0, 0), (0, 0), (pad, pad)))
    t = jnp.stack([rp[:, :, GRID_W - 1 - q:2 * GRID_W - 1 - q] for q in range(GRID_W)], axis=2)
    t = jnp.where(col_ok[None, None], t, NEG_INF)
    t = jnp.concatenate([t, jnp.full((N_HEADS, 1, GRID_W, GRID_W), NEG_INF, _f32)], axis=1).astype(_bf16)
    even = jnp.take(t, dr[:, :, 0::2].reshape(-1), axis=1)
    odd = jnp.take(t, dr[:, :, 1::2].reshape(-1), axis=1)
    b = jnp.concatenate([even, odd], axis=-1)
    b = b.reshape(N_HEADS, 3, ROWS_PER_TILE, KEY_TILES * 2, GRID_W, 2 * GRID_W)
    return jnp.transpose(b, (1, 0, 2, 3, 4, 5))


def _attn_kernel(q_ref, k0_ref, k1_ref, k2_ref, v0_ref, v1_ref, v2_ref, bias_ref, o_ref):
    lane = lax.broadcasted_iota(jnp.int32, (1, 128), 1)
    low = lane < HEAD_DIM
    head_mask = [low.astype(_bf16), (~low).astype(_bf16)]
    k_refs = (k0_ref, k1_ref, k2_ref)
    v_refs = (v0_ref, v1_ref, v2_ref)
    for hp in range(N_HEADS // 2):
        cols = slice(hp * 128, (hp + 1) * 128)
        qp = q_ref[0, :, cols]
        outs = []
        for hh in range(2):
            qm = qp * head_mask[hh]
            s = []
            for t in range(KEY_TILES):
                st = lax.dot_general(qm, k_refs[t][0, :, cols], (((1,), (1,)), ((), ())),
                                     preferred_element_type=_f32)
                bias = jnp.concatenate(
                    [jnp.concatenate([bias_ref[0, 2 * hp + hh, i, 2 * t], bias_ref[0, 2 * hp + hh, i, 2 * t + 1]],
                                     axis=1) for i in range(ROWS_PER_TILE)], axis=0)
                s.append(st + bias.astype(_f32))
            m = jnp.max(jnp.maximum(jnp.maximum(s[0], s[1]), s[2]), axis=-1, keepdims=True)
            p = [jnp.exp2(st - m) for st in s]
            l = jnp.sum(p[0] + p[1] + p[2], axis=-1, keepdims=True)
            o = jnp.dot(p[0].astype(_bf16), v_refs[0][0, :, cols], preferred_element_type=_f32)
            for t in range(1, KEY_TILES):
                o = o + jnp.dot(p[t].astype(_bf16), v_refs[t][0, :, cols], preferred_element_type=_f32)
            outs.append(o / l)
        o_ref[0, :, cols] = jnp.where(low, outs[0], outs[1]).astype(o_ref.dtype)


def _attention(qkv, bias):
    b, l, _ = qkv.shape
    nt = l // TILE_TOK
    assert nt >= KEY_TILES

    def base(j):
        return jnp.clip(j - 1, 0, nt - KEY_TILES)

    def pattern(j):
        return jnp.where(j == 0, 0, jnp.where(j == nt - 1, 2, 1))

    def kv_spec(part, t):
        return pl.BlockSpec((1, TILE_TOK, D_MODEL), lambda i, j: (i, base(j) + t, part))

    return pl.pallas_call(
        _attn_kernel,
        grid=(b, nt),
        in_specs=[pl.BlockSpec((1, TILE_TOK, D_MODEL), lambda i, j: (i, j, 0))]
        + [kv_spec(1, t) for t in range(KEY_TILES)]
        + [kv_spec(2, t) for t in range(KEY_TILES)]
        + [pl.BlockSpec((1, N_HEADS, ROWS_PER_TILE, KEY_TILES * 2, GRID_W, 2 * GRID_W),
                        lambda i, j: (pattern(j), 0, 0, 0, 0, 0))],
        out_specs=pl.BlockSpec((1, TILE_TOK, D_MODEL), lambda i, j: (i, j, 0)),
        out_shape=jax.ShapeDtypeStruct((b, l, D_MODEL), _bf16),
        compiler_params=_cparams(("parallel", "arbitrary")),
        name="nbr_attention",
    )(qkv, qkv, qkv, qkv, qkv, qkv, qkv, bias)


def _s5_tables(a_re, a_im, log_dt, b_re, b_im, c_re, c_im):
    hi = lax.Precision.HIGHEST
    t = S5_CHUNK
    a_re, a_im, b_re, b_im, c_re, c_im = [v.astype(_f32) for v in (a_re, a_im, b_re, b_im, c_re, c_im)]
    dt = jnp.exp(log_dt.astype(_f32))[..., None]
    k = jnp.arange(t + 1, dtype=_f32)[:, None, None, None]
    mag = jnp.exp(k * (a_re * dt))
    ang = k * (a_im * dt)
    pw_re = mag * jnp.cos(ang)
    pw_im = mag * jnp.sin(ang)
    lam_re, lam_im = pw_re[1], pw_im[1]
    den = a_re * a_re + a_im * a_im
    nr = lam_re - 1.0
    ni = lam_im
    zr = (nr * a_re + ni * a_im) / den
    zi = (ni * a_re - nr * a_im) / den
    bb_re = zr[..., None] * b_re - zi[..., None] * b_im
    bb_im = zr[..., None] * b_im + zi[..., None] * b_re

    lb_re = pw_re[:t, ..., None] * bb_re - pw_im[:t, ..., None] * bb_im
    lb_im = pw_re[:t, ..., None] * bb_im + pw_im[:t, ..., None] * bb_re
    kern = (jnp.einsum('dgip,kdgpj->kdgij', c_re, lb_re, precision=hi)
            - jnp.einsum('dgip,kdgpj->kdgij', c_im, lb_im, precision=hi))
    s_idx = np.arange(t)[:, None]
    t_idx = np.arange(t)[None, :]
    m_dirs = []
    for d, lag in ((0, t_idx - s_idx), (1, s_idx - t_idx)):
        kd = kern[:, d]
        blk = kd[np.clip(lag, 0, t - 1)]
        blk = jnp.where((lag >= 0)[:, :, None, None, None], blk, 0.0)
        m_dirs.append(jnp.transpose(blk, (2, 0, 4, 1, 3)))
    m_sum = (m_dirs[0] + m_dirs[1]).reshape(S5_GROUPS, t * S5_GROUP, t * S5_GROUP).astype(_bf16)

    eye2 = jnp.eye(2, dtype=_f32)

    def pair_in(w):
        w = w.reshape(2, S5_PAIRS, 2, t * S5_GROUP, S5_STATE)
        w = w[:, :, :, :, None, :] * eye2[None, None, :, None, :, None]
        return w.reshape(2, S5_PAIRS, PAIR_W, STATE_W)

    def pair_out(v):
        v = v.reshape(2, S5_PAIRS, 2, S5_STATE, t * S5_GROUP)
        v = v[:, :, :, :, None, :] * eye2[None, None, :, None, :, None]
        return v.reshape(2, S5_PAIRS, STATE_W, PAIR_W)

    lb_re_t = jnp.transpose(lb_re, (1, 2, 0, 4, 3))
    lb_im_t = jnp.transpose(lb_im, (1, 2, 0, 4, 3))
    w_re = jnp.stack([lb_re_t[0, :, ::-1], lb_re_t[1]])
    w_im = jnp.stack([lb_im_t[0, :, ::-1], lb_im_t[1]])
    w_in = jnp.concatenate([pair_in(w_re), pair_in(w_im)], axis=-1).astype(_bf16)

    e_re = jnp.stack([pw_re[1:, 0], pw_re[1:, 1][::-1]], axis=0)
    e_im = jnp.stack([pw_im[1:, 0], pw_im[1:, 1][::-1]], axis=0)
    ce_re = c_re[:, None] * e_re[:, :, :, None, :] - c_im[:, None] * e_im[:, :, :, None, :]
    ce_im = c_re[:, None] * e_im[:, :, :, None, :] + c_im[:, None] * e_re[:, :, :, None, :]
    v_re = jnp.transpose(ce_re, (0, 2, 4, 1, 3))
    v_im = jnp.transpose(-ce_im, (0, 2, 4, 1, 3))
    v_out = jnp.concatenate([pair_out(v_re), pair_out(v_im)], axis=2).astype(_bf16)

    lam_c = jnp.concatenate([pw_re[t].reshape(2, S5_PAIRS, STATE_W),
                             pw_im[t].reshape(2, S5_PAIRS, STATE_W)], axis=-1)
    lam_t = jnp.broadcast_to(lam_c[:, :, None, :], (2, S5_PAIRS, S5_SEGS, 2 * STATE_W))
    return m_sum, w_in, v_out, lam_t


def _cmul(ar, ai, br, bi):
    return ar * br - ai * bi, ar * bi + ai * br


def _s5_kernel(chained, u_ref, m_ref, w_ref, v_ref, lam_ref, y_ref, x_scr, h_scr):
    rows = u_ref.shape[1]
    nc = rows // S5_SEGS
    rb = min(512, rows)
    sw = STATE_W

    def inc_body(r, carry):
        rs = pl.ds(pl.multiple_of(r * rb, rb), rb)
        u = u_ref[0, rs, :]
        for d in range(2):
            x_scr[d, rs, :] = jnp.dot(u, w_ref[d, 0], preferred_element_type=_f32)
        return carry

    lax.fori_loop(0, rows // rb, inc_body, 0)

    lam = [(lam_ref[d, 0, :, :sw], lam_ref[d, 0, :, sw:]) for d in range(2)]

    def row_of(d, c):
        cc = c if d == 0 else nc - 1 - c
        return pl.ds(pl.multiple_of(cc * S5_SEGS, S5_SEGS), S5_SEGS)

    def scan_body(c, carry):
        new = []
        for d in range(2):
            hr, hi = carry[d]
            rs = row_of(d, c)
            h_scr[d, rs, :sw] = hr
            h_scr[d, rs, sw:] = hi
            pr, pi = _cmul(lam[d][0], lam[d][1], hr, hi)
            new.append((pr + x_scr[d, rs, :sw], pi + x_scr[d, rs, sw:]))
        return tuple(new)

    zero = jnp.zeros((S5_SEGS, sw), _f32)
    ends = lax.fori_loop(0, nc, scan_body, ((zero, zero), (zero, zero)))

    if chained:
        def pow_body(c, carry):
            return tuple(_cmul(lam[d][0], lam[d][1], carry[d][0], carry[d][1]) for d in range(2))

        one = jnp.ones((S5_SEGS, sw), _f32)
        seg_decay = lax.fori_loop(0, nc, pow_body, ((one, zero), (one, zero)))
        h_in = []
        for d in range(2):
            er, ei = ends[d]
            dr, di = seg_decay[d][0][0:1], seg_decay[d][1][0:1]
            order = range(S5_SEGS) if d == 0 else range(S5_SEGS - 1, -1, -1)
            cr = jnp.zeros((1, sw), _f32)
            ci = jnp.zeros((1, sw), _f32)
            rows_r = [None] * S5_SEGS
            rows_i = [None] * S5_SEGS
            for sgm in order:
                rows_r[sgm], rows_i[sgm] = cr, ci
                pr, pi = _cmul(dr, di, cr, ci)
                cr, ci = pr + er[sgm:sgm + 1], pi + ei[sgm:sgm + 1]
            h_in.append((jnp.concatenate(rows_r, axis=0), jnp.concatenate(rows_i, axis=0)))

        def fix_body(c, carry):
            new = []
            for d in range(2):
                pr, pi = carry[d]
                rs = row_of(d, c)
                ar, ai = _cmul(pr, pi, h_in[d][0], h_in[d][1])
                h_scr[d, rs, :sw] = h_scr[d, rs, :sw] + ar
                h_scr[d, rs, sw:] = h_scr[d, rs, sw:] + ai
                new.append(_cmul(lam[d][0], lam[d][1], pr, pi))
            return tuple(new)

        lax.fori_loop(0, nc, fix_body, ((one, zero), (one, zero)))

    half = PAIR_W // 2

    def out_body(r, carry):
        rs = pl.ds(pl.multiple_of(r * rb, rb), rb)
        u = u_ref[0, rs, :]
        y = (jnp.dot(h_scr[0, rs, :].astype(_bf16), v_ref[0, 0], preferred_element_type=_f32)
             + jnp.dot(h_scr[1, rs, :].astype(_bf16), v_ref[1, 0], preferred_element_type=_f32))
        y0 = y[:, :half] + jnp.dot(u[:, :half], m_ref[0], preferred_element_type=_f32)
        y1 = y[:, half:] + jnp.dot(u[:, half:], m_ref[1], preferred_element_type=_f32)
        y_ref[0, rs, :half] = y0.astype(y_ref.dtype)
        y_ref[0, rs, half:] = y1.astype(y_ref.dtype)
        return carry

    lax.fori_loop(0, rows // rb, out_body, 0)


def _s5_core(u_t, tables, chained):
    m_sum, w_in, v_out, lam_t = tables
    _, rows, _ = u_t.shape
    return pl.pallas_call(
        functools.partial(_s5_kernel, chained),
        grid=(S5_PAIRS,),
        in_specs=[
            pl.BlockSpec((1, rows, PAIR_W), lambda p: (p, 0, 0)),
            pl.BlockSpec((2, PAIR_W // 2, PAIR_W // 2), lambda p: (p, 0, 0)),
            pl.BlockSpec((2, 1, PAIR_W, 2 * STATE_W), lambda p: (0, p, 0, 0)),
            pl.BlockSpec((2, 1, 2 * STATE_W, PAIR_W), lambda p: (0, p, 0, 0)),
            pl.BlockSpec((2, 1, S5_SEGS, 2 * STATE_W), lambda p: (0, p, 0, 0)),
        ],
        out_specs=pl.BlockSpec((1, rows, PAIR_W), lambda p: (p, 0, 0)),
        out_shape=jax.ShapeDtypeStruct((S5_PAIRS, rows, PAIR_W), _bf16),
        scratch_shapes=[pltpu.VMEM((2, rows, 2 * STATE_W), _f32),
                        pltpu.VMEM((2, rows, 2 * STATE_W), _f32)],
        compiler_params=_cparams(("parallel",)),
        name="s5_core",
    )(u_t, m_sum, w_in, v_out, lam_t)


def _gelu_tanh(x):
    return 0.5 * x * (1.0 + jnp.tanh(math.sqrt(2.0 / math.pi) * (x + 0.044715 * (x * x * x))))


def _post_kernel(mixer, *refs):
    if mixer == "attn":
        x_ref, mix_ref, mod_ref, gain_ref, wp_ref, w1_ref, w2_ref, o_ref, x1_scr, h_scr, acc_ref = refs
    else:
        (x_ref, mix_ref, mod_ref, gain_ref, skip_ref, pt_ref, wp_ref, w1_ref, w2_ref, o_ref,
         x1_scr, h_scr, acc_ref, nat_scr, lhs_scr) = refs
    tmc = x_ref.shape[1]

    def rows(seg):
        return slice(seg * tmc, (seg + 1) * tmc)

    if mixer == "attn":
        for seg in range(S5_SEGS):
            h_scr[rows(seg), :] = mix_ref[seg]
        y = jnp.dot(h_scr[...], wp_ref[...], preferred_element_type=_f32)
    else:
        nck = tmc // S5_CHUNK
        blk = nck * S5_SEGS
        for lt in range(LANE_TILES):
            for h in range(2):
                for gl in range(8):
                    g = 8 * lt + gl
                    col = _chunk_col(g, h)
                    lhs_scr[(lt * 2 + h) * blk:(lt * 2 + h + 1) * blk, gl * 128:(gl + 1) * 128] = (
                        mix_ref[g // 2, :, col:col + 128])
        a = lhs_scr[...]
        for sp in range(4):
            out = jnp.dot(a, pt_ref[:, sp * 256:(sp + 1) * 256], preferred_element_type=_f32)
            for lt in range(LANE_TILES):
                for h in range(2):
                    for s in (2 * sp, 2 * sp + 1):
                        for c in range(nck):
                            tok = c * S5_CHUNK + 8 * h + s
                            row = (lt * 2 + h) * blk + c * S5_SEGS
                            nat_scr[lt, tok * S5_SEGS:(tok + 1) * S5_SEGS, :] = (
                                out[row:row + S5_SEGS, (s % 2) * 128:(s % 2 + 1) * 128])
        for seg in range(S5_SEGS):
            mod = mod_ref[seg]
            u = _norm_mod(x_ref[seg], gain_ref[0:1, :], mod[1:2, :], mod[0:1, :])
            y_ssm = jnp.concatenate([nat_scr[lt, pl.ds(seg, tmc, stride=S5_SEGS), :]
                                     for lt in range(D_MODEL // 128)], axis=1)
            h_scr[rows(seg), :] = _gelu_tanh(skip_ref[...] * u + y_ssm).astype(_bf16)
        g = h_scr[...]
        val = jnp.dot(g, wp_ref[:, :D_MODEL], preferred_element_type=_f32)
        gate = jnp.dot(g, wp_ref[:, D_MODEL:], preferred_element_type=_f32)
        y = val * jax.nn.sigmoid(gate)
    for seg in range(S5_SEGS):
        mod = mod_ref[seg]
        x1 = x_ref[seg] + mod[2:3, :] * _rms(y[rows(seg), :], gain_ref[1:2, :])
        x1_scr[rows(seg), :] = x1
        h_scr[rows(seg), :] = _norm_mod(x1, gain_ref[2:3, :], mod[4:5, :], mod[3:4, :]).astype(_bf16)
    h2 = h_scr[...]
    tf = 512
    for kf in range(D_FF // tf):
        cols = slice(kf * tf, (kf + 1) * tf)
        a = jnp.dot(h2, w1_ref[:, cols], preferred_element_type=_f32)
        a = jnp.square(jnp.maximum(a, 0.0)).astype(_bf16)
        part = jnp.dot(a, w2_ref[cols, :], preferred_element_type=_f32)
        if kf == 0:
            acc_ref[...] = part
        else:
            acc_ref[...] += part
    for seg in range(S5_SEGS):
        mod = mod_ref[seg]
        o_ref[seg] = x1_scr[rows(seg), :] + mod[5:6, :] * _rms(acc_ref[rows(seg), :], gain_ref[3:4, :])


def _post_mixer(mixer, x8, mix, mod8, gains, w_proj, w1, w2, skip=None, perm_t=None, tmc=64):
    _, ls, _ = x8.shape
    tok = lambda j: (0, j, 0)
    const2 = lambda j: (0, 0)
    m = S5_SEGS * tmc
    in_specs = [pl.BlockSpec((S5_SEGS, tmc, D_MODEL), tok)]
    scratch = [pltpu.VMEM((m, D_MODEL), _f32), pltpu.VMEM((m, D_MODEL), _bf16), pltpu.VMEM((m, D_MODEL), _f32)]
    if mixer == "attn":
        in_specs.append(pl.BlockSpec((S5_SEGS, tmc, D_MODEL), tok))
    else:
        in_specs.append(pl.BlockSpec((S5_PAIRS, tmc // S5_CHUNK * S5_SEGS, PAIR_W), tok))
        scratch += [pltpu.VMEM((LANE_TILES, m, 128), _f32), pltpu.VMEM((m, D_MODEL), _bf16)]
    in_specs += [_resident((S5_SEGS, 6, D_MODEL), lambda j: (0, 0, 0)), _resident((4, D_MODEL), const2)]
    args = [x8, mix, mod8, gains]
    if mixer == "s5":
        in_specs += [_resident((1, D_MODEL), const2), _resident((D_MODEL, D_MODEL), const2)]
        args += [skip, perm_t]
    in_specs += [_resident(w_proj.shape, const2), _resident(w1.shape, const2), _resident(w2.shape, const2)]
    args += [w_proj, w1, w2]
    return pl.pallas_call(
        functools.partial(_post_kernel, mixer),
        grid=(ls // tmc,),
        in_specs=in_specs,
        out_specs=pl.BlockSpec((S5_SEGS, tmc, D_MODEL), tok),
        out_shape=jax.ShapeDtypeStruct(x8.shape, _f32),
        scratch_shapes=scratch,
        compiler_params=_cparams(("parallel",)),
        name="post_" + mixer,
    )(*args)


def _trunk(x, mod, params, chained):
    b, l, _ = x.shape
    assert (b == 1) if chained else (b == S5_SEGS)
    ls = b * l // S5_SEGS
    gains = params["norm_gain"]
    mod8 = jnp.broadcast_to(mod, (DEPTH, S5_SEGS, 6, D_MODEL))
    qkv = _qkv_proj(x, mod[0], gains[0], params["w_qkv"])
    att = _attention(qkv, params["attn_bias"])
    x8 = _post_mixer("attn", x.reshape(S5_SEGS, ls, D_MODEL), att.reshape(S5_SEGS, ls, D_MODEL), mod8[0], gains[0],
                     params["w_o"], params["ffn_w1"][0], params["ffn_w2"][0])
    u = _s5_input(x8, mod8[1], gains[1], _lane_perm())
    y = _s5_core(u, params["s5_tables"], chained)
    x8 = _post_mixer("s5", x8, y, mod8[1], gains[1], params["w_glu"], params["ffn_w1"][1], params["ffn_w2"][1],
                     skip=params["s5_d"], perm_t=_lane_perm(inverse=True))
    return x8.reshape(b, l, D_MODEL)


def kernel(x_prompt, x_sample, c_prompt, c_sample, ada_w, ada_b, norm_gain, attn_w_qkv, attn_w_o, attn_rpb,
           s5_a_re, s5_a_im, s5_log_dt, s5_b_re, s5_b_im, s5_c_re, s5_c_im, s5_d, s5_w_glu, ffn_w1, ffn_w2):
    nbp, nbs = c_prompt.shape[0], c_sample.shape[0]
    nb = -(-(nbp + nbs) // 8) * 8
    c_all = jnp.concatenate([c_prompt, c_sample, jnp.zeros((nb - nbp - nbs, D_MODEL), _f32)], axis=0)
    mod = _modulation(c_all, ada_w, ada_b)
    mod_p = mod[:, :nbp].reshape(DEPTH, nbp, 6, D_MODEL)
    mod_s = mod[:, nbp:nbp + nbs].reshape(DEPTH, nbs, 6, D_MODEL)

    params = {
        "norm_gain": norm_gain,
        "w_qkv": attn_w_qkv[0].astype(_bf16),
        "w_o": attn_w_o[0].astype(_bf16),
        "attn_bias": _attn_bias_table(attn_rpb[0]),
        "s5_tables": _s5_tables(s5_a_re[0], s5_a_im[0], s5_log_dt[0], s5_b_re[0], s5_b_im[0],
                                s5_c_re[0], s5_c_im[0]),
        "s5_d": s5_d[0].reshape(1, D_MODEL),
        "w_glu": s5_w_glu[0].astype(_bf16),
        "ffn_w1": ffn_w1.astype(_bf16),
        "ffn_w2": ffn_w2.astype(_bf16),
    }
    y_prompt = _trunk(x_prompt, mod_p, params, chained=False)
    y_sample = _trunk(x_sample, mod_s, params, chained=True)
    return (y_prompt, y_sample)
```

```python
import functools
import math

import numpy as np
import jax
import jax.numpy as jnp
from jax import lax
from jax.experimental import pallas as pl
from jax.experimental.pallas import tpu as pltpu

D_MODEL = 1024
D_FF = 4 * D_MODEL
DEPTH = 2
EPS = 1e-6
NEG_INF = -1e30
LOG2E = math.log2(math.e)

GRID_W = 64
N_HEADS = 16
HEAD_DIM = D_MODEL // N_HEADS
WIN_H = 8
WIN_W = 16
ROWS_PER_TILE = 4
TILE_TOK = ROWS_PER_TILE * GRID_W
KEY_TILES = 3

S5_GROUP = 16
S5_GROUPS = D_MODEL // S5_GROUP
S5_STATE = 64
S5_CHUNK = 16
S5_SEGS = 8
S5_PAIRS = S5_GROUPS // 2
PAIR_W = 2 * S5_CHUNK * S5_GROUP
STATE_W = 2 * S5_STATE

VMEM_LIMIT = 56 * 1024 * 1024

_f32 = jnp.float32
_bf16 = jnp.bfloat16


def _cparams(sem):
    return pltpu.CompilerParams(dimension_semantics=sem, vmem_limit_bytes=VMEM_LIMIT)


def _resident(shape, index_map):
    return pl.BlockSpec(shape, index_map, pipeline_mode=pl.Buffered(1))


def _rms(x, gain):
    ms = jnp.mean(x * x, axis=-1, keepdims=True)
    return x * lax.rsqrt(ms + EPS) * gain


def _norm_mod(x, gain, scale, shift):
    return _rms(x, gain) * (1.0 + scale) + shift


def _mod_kernel(c_ref, w_ref, b_ref, o_ref):
    c = c_ref[...]
    act = c * jax.nn.sigmoid(c)
    o_ref[0] = jnp.dot(act, w_ref[0], preferred_element_type=_f32,
                       precision=lax.Precision.HIGHEST) + b_ref[0]


def _modulation(c_all, ada_w, ada_b):
    nb = c_all.shape[0]
    tn = 1536
    return pl.pallas_call(
        _mod_kernel,
        grid=(DEPTH, 6 * D_MODEL // tn),
        in_specs=[
            pl.BlockSpec((nb, D_MODEL), lambda i, n: (0, 0)),
            pl.BlockSpec((1, D_MODEL, tn), lambda i, n: (i, 0, n)),
            pl.BlockSpec((1, 1, tn), lambda i, n: (i, 0, n)),
        ],
        out_specs=pl.BlockSpec((1, nb, tn), lambda i, n: (i, 0, n)),
        out_shape=jax.ShapeDtypeStruct((DEPTH, nb, 6 * D_MODEL), _f32),
        compiler_params=_cparams(("arbitrary", "arbitrary")),
        name="adaln_mod",
    )(c_all, ada_w, ada_b.reshape(DEPTH, 1, 6 * D_MODEL))


def _qkv_kernel(x_ref, mod_ref, gain_ref, w_ref, o_ref):
    mod = mod_ref[0]
    h = _norm_mod(x_ref[0], gain_ref[0:1, :], mod[1:2, :], mod[0:1, :]).astype(_bf16)
    q_scale = HEAD_DIM ** -0.5 * LOG2E
    for part in range(3):
        cols = slice(part * D_MODEL, (part + 1) * D_MODEL)
        y = jnp.dot(h, w_ref[:, cols], preferred_element_type=_f32)
        if part == 0:
            y = y * q_scale
        o_ref[0, :, cols] = y.astype(_bf16)


def _qkv_proj(x, mod, gains, w_qkv, tm=512):
    b, l, _ = x.shape
    return pl.pallas_call(
        _qkv_kernel,
        grid=(b, l // tm),
        in_specs=[
            pl.BlockSpec((1, tm, D_MODEL), lambda i, j: (i, j, 0)),
            pl.BlockSpec((1, 6, D_MODEL), lambda i, j: (i, 0, 0)),
            _resident((4, D_MODEL), lambda i, j: (0, 0)),
            _resident((D_MODEL, 3 * D_MODEL), lambda i, j: (0, 0)),
        ],
        out_specs=pl.BlockSpec((1, tm, 3 * D_MODEL), lambda i, j: (i, j, 0)),
        out_shape=jax.ShapeDtypeStruct((b, l, 3 * D_MODEL), _bf16),
        compiler_params=_cparams(("parallel", "parallel")),
        name="norm_qkv",
    )(x, mod, gains, w_qkv)


LANE_TILES = D_MODEL // 128


def _lane_perm(inverse=False):
    eye = np.eye
    p = np.einsum('sS,gG,iI->sgiGSI', eye(8), eye(8), eye(S5_GROUP)).reshape(D_MODEL, D_MODEL)
    return jnp.asarray(p.T if inverse else p, _bf16)


def _chunk_col(g, h):
    return (g % 2) * (PAIR_W // 2) + h * 128


def _s5_in_kernel(x_ref, mod_ref, gain_ref, p_ref, o_ref, ubuf, lhs):
    tmc = x_ref.shape[1]
    nck = tmc // S5_CHUNK
    blk = nck * S5_SEGS
    for seg in range(S5_SEGS):
        mod = mod_ref[seg]
        u = _norm_mod(x_ref[seg], gain_ref[0:1, :], mod[1:2, :], mod[0:1, :])
        for lt in range(LANE_TILES):
            ubuf[lt] = u[:, lt * 128:(lt + 1) * 128]
        for lt in range(LANE_TILES):
            for h in range(2):
                row0 = (lt * 2 + h) * blk + seg
                for s in range(8):
                    lhs[s, pl.ds(row0, nck, stride=S5_SEGS), :] = ubuf[lt, pl.ds(8 * h + s, nck, stride=S5_CHUNK), :]
    a = jnp.concatenate([lhs[s] for s in range(8)], axis=1).astype(_bf16)
    for gp in range(4):
        out = jnp.dot(a, p_ref[:, gp * 256:(gp + 1) * 256], preferred_element_type=_f32)
        for lt in range(LANE_TILES):
            for h in range(2):
                for gl in (2 * gp, 2 * gp + 1):
                    g = 8 * lt + gl
                    col = _chunk_col(g, h)
                    piece = out[(lt * 2 + h) * blk:(lt * 2 + h + 1) * blk, (gl % 2) * 128:(gl % 2 + 1) * 128]
                    o_ref[g // 2, :, col:col + 128] = piece.astype(o_ref.dtype)


def _s5_input(x8, mod8, gains, perm, tmc=128):
    _, ls, _ = x8.shape
    rows_blk = tmc // S5_CHUNK * S5_SEGS
    return pl.pallas_call(
        _s5_in_kernel,
        grid=(ls // tmc,),
        in_specs=[
            pl.BlockSpec((S5_SEGS, tmc, D_MODEL), lambda j: (0, j, 0)),
            _resident((S5_SEGS, 6, D_MODEL), lambda j: (0, 0, 0)),
            _resident((4, D_MODEL), lambda j: (0, 0)),
            _resident((D_MODEL, D_MODEL), lambda j: (0, 0)),
        ],
        out_specs=pl.BlockSpec((S5_PAIRS, rows_blk, PAIR_W), lambda j: (0, j, 0)),
        out_shape=jax.ShapeDtypeStruct((S5_PAIRS, ls // S5_CHUNK * S5_SEGS, PAIR_W), _bf16),
        scratch_shapes=[pltpu.VMEM((LANE_TILES, tmc, 128), _f32),
                        pltpu.VMEM((8, 2 * LANE_TILES * rows_blk, 128), _f32)],
        compiler_params=_cparams(("parallel",)),
        name="s5_input",
    )(x8, mod8, gains, perm)


def _attn_bias_table(rpb):
    n_dr = 2 * WIN_H - 1
    i = np.arange(ROWS_PER_TILE)[:, None]
    kk = np.arange(KEY_TILES * ROWS_PER_TILE)[None, :]
    dr = np.stack([kk - i + 7, kk - i + 3, kk - i - 1])
    row_ok = np.stack([(kk < WIN_H) & (i >= 0), (kk >= i) & (kk < i + WIN_H), (kk >= 4) & (i >= 0)])
    dr = np.where(row_ok, dr, n_dr)
    qc = np.arange(GRID_W)[:, None]
    kc = np.arange(GRID_W)[None, :]
    col_start = np.clip(qc - WIN_W // 2, 0, GRID_W - WIN_W)
    col_ok = (kc >= col_start) & (kc < col_start + WIN_W)
    pad = GRID_W - WIN_W
    rp = jnp.pad(rpb.astype(_f32) * LOG2E, ((0, 0), (0, 0), (pad, pad)))
    t = jnp.stack([rp[:, :, GRID_W - 1 - q:2 * GRID_W - 1 - q] for q in range(GRID_W)], axis=2)
    t = jnp.where(col_ok[None, None], t, NEG_INF)
    t = jnp.concatenate([t, jnp.full((N_HEADS, 1, GRID_W, GRID_W), NEG_INF, _f32)], axis=1).astype(_bf16)
    even = jnp.take(t, dr[:, :, 0::2].reshape(-1), axis=1)
    odd = jnp.take(t, dr[:, :, 1::2].reshape(-1), axis=1)
    b = jnp.concatenate([even, odd], axis=-1)
    b = b.reshape(N_HEADS, 3, ROWS_PER_TILE, KEY_TILES * 2, GRID_W, 2 * GRID_W)
    return jnp.transpose(b, (1, 0, 2, 3, 4, 5))


def _attn_kernel(q_ref, k0_ref, k1_ref, k2_ref, v0_ref, v1_ref, v2_ref, bias_ref, o_ref):
    lane = lax.broadcasted_iota(jnp.int32, (1, 128), 1)
    low = lane < HEAD_DIM
    head_mask = [low.astype(_bf16), (~low).astype(_bf16)]
    k_refs = (k0_ref, k1_ref, k2_ref)
    v_refs = (v0_ref, v1_ref, v2_ref)
    head_of_lane = jnp.right_shift(lax.broadcasted_iota(jnp.int32, (1, 256), 1), HEAD_DIM.bit_length() - 1)
    for hq in range(N_HEADS // 4):
        vcols = slice(hq * 256, (hq + 1) * 256)
        acc = None
        for h4 in range(4):
            head = 4 * hq + h4
            cols = slice((head // 2) * 128, (head // 2 + 1) * 128)
            qm = q_ref[0, :, cols] * head_mask[head % 2]
            s = []
            for t in range(KEY_TILES):
                st = lax.dot_general(qm, k_refs[t][0, :, cols], (((1,), (1,)), ((), ())),
                                     preferred_element_type=_f32)
                bias = jnp.concatenate(
                    [jnp.concatenate([bias_ref[0, head, i, 2 * t], bias_ref[0, head, i, 2 * t + 1]], axis=1)
                     for i in range(ROWS_PER_TILE)], axis=0)
                s.append(st + bias.astype(_f32))
            m = jnp.max(jnp.maximum(jnp.maximum(s[0], s[1]), s[2]), axis=-1, keepdims=True)
            p = [jnp.exp2(st - m) for st in s]
            l = jnp.sum(p[0] + p[1] + p[2], axis=-1, keepdims=True)
            o = jnp.dot(p[0].astype(_bf16), v_refs[0][0, :, vcols], preferred_element_type=_f32)
            for t in range(1, KEY_TILES):
                o = o + jnp.dot(p[t].astype(_bf16), v_refs[t][0, :, vcols], preferred_element_type=_f32)
            o = o * (1.0 / l)
            acc = o if acc is None else jnp.where(head_of_lane == h4, o, acc)
        o_ref[0, :, vcols] = acc.astype(o_ref.dtype)


def _attention(qkv, bias):
    b, l, _ = qkv.shape
    nt = l // TILE_TOK
    assert nt >= KEY_TILES

    def base(j):
        return jnp.clip(j - 1, 0, nt - KEY_TILES)

    def pattern(j):
        return jnp.where(j == 0, 0, jnp.where(j == nt - 1, 2, 1))

    def kv_spec(part, t):
        return pl.BlockSpec((1, TILE_TOK, D_MODEL), lambda i, j: (i, base(j) + t, part))

    return pl.pallas_call(
        _attn_kernel,
        grid=(b, nt),
        in_specs=[pl.BlockSpec((1, TILE_TOK, D_MODEL), lambda i, j: (i, j, 0))]
        + [kv_spec(1, t) for t in range(KEY_TILES)]
        + [kv_spec(2, t) for t in range(KEY_TILES)]
        + [pl.BlockSpec((1, N_HEADS, ROWS_PER_TILE, KEY_TILES * 2, GRID_W, 2 * GRID_W),
                        lambda i, j: (pattern(j), 0, 0, 0, 0, 0))],
        out_specs=pl.BlockSpec((1, TILE_TOK, D_MODEL), lambda i, j: (i, j, 0)),
        out_shape=jax.ShapeDtypeStruct((b, l, D_MODEL), _bf16),
        compiler_params=_cparams(("parallel", "arbitrary")),
        name="nbr_attention",
    )(qkv, qkv, qkv, qkv, qkv, qkv, qkv, bias)


def _s5_tables(a_re, a_im, log_dt, b_re, b_im, c_re, c_im):
    f = lambda v: v.astype(_f32)
    pairs = (2, S5_PAIRS)
    a_row = [f(a).reshape(*pairs, 1, STATE_W) for a in (a_re, a_im)]
    a_col = [f(a).reshape(*pairs, 2, S5_STATE, 1) for a in (a_re, a_im)]
    ldt_row = jnp.repeat(f(log_dt), S5_STATE, axis=-1).reshape(*pairs, 1, STATE_W)
    ldt_col = f(log_dt).reshape(*pairs, 2, 1, 1)
    b_t = [jnp.transpose(f(b).reshape(*pairs, 2, S5_STATE, S5_GROUP), (0, 1, 4, 2, 3))
           .reshape(*pairs, S5_GROUP, STATE_W) for b in (b_re, b_im)]
    c_t = [jnp.transpose(f(c), (0, 1, 3, 2)).reshape(*pairs, 2, S5_STATE, S5_GROUP) for c in (c_re, c_im)]

    def spec(shape):
        nd = len(shape)
        return pl.BlockSpec((2, 1) + tuple(shape[2:]), lambda p: (0, p) + (0,) * (nd - 2))

    ins = a_row + a_col + [ldt_row, ldt_col] + b_t + c_t
    out_shapes = (
        jax.ShapeDtypeStruct((S5_GROUPS, PAIR_W // 2, PAIR_W // 2), _bf16),
        jax.ShapeDtypeStruct((2, S5_PAIRS, PAIR_W, 2 * STATE_W), _bf16),
        jax.ShapeDtypeStruct((2, S5_PAIRS, 2 * STATE_W, PAIR_W), _bf16),
        jax.ShapeDtypeStruct((2, S5_PAIRS, S5_SEGS, 2 * STATE_W), _f32),
    )
    return pl.pallas_call(
        _s5_tables_kernel,
        grid=(S5_PAIRS,),
        in_specs=[spec(v.shape) for v in ins],
        out_specs=(pl.BlockSpec((2, PAIR_W // 2, PAIR_W // 2), lambda p: (p, 0, 0)),
                   spec(out_shapes[1].shape), spec(out_shapes[2].shape), spec(out_shapes[3].shape)),
        out_shape=out_shapes,
        compiler_params=_cparams(("parallel",)),
        name="s5_tables",
    )(*ins)


def _cexp(k, a_re, a_im, dt):
    mag = jnp.exp(k * (a_re * dt))
    ang = k * (a_im * dt)
    return mag * jnp.cos(ang), mag * jnp.sin(ang)


def _s5_tables_kernel(ar_re_ref, ar_im_ref, ac_re_ref, ac_im_ref, ldr_ref, ldc_ref,
                      bt_re_ref, bt_im_ref, ct_re_ref, ct_im_ref, m_ref, w_ref, v_ref, lam_ref):
    hi = lax.Precision.HIGHEST
    t = S5_CHUNK
    cw = t * S5_GROUP
    lane_c = lax.broadcasted_iota(jnp.int32, (1, cw), 1)
    log_group = S5_GROUP.bit_length() - 1
    lag = jnp.right_shift(lane_c, log_group).astype(_f32)
    tok_row = jnp.right_shift(lax.broadcasted_iota(jnp.int32, (cw, 1), 0), log_group).astype(_f32)
    low = lax.broadcasted_iota(jnp.int32, (1, STATE_W), 1) < S5_STATE
    tile_i = (lax.broadcasted_iota(jnp.int32, (S5_GROUP, cw), 0)
              == jnp.bitwise_and(lax.broadcasted_iota(jnp.int32, (S5_GROUP, cw), 1), S5_GROUP - 1)).astype(_f32)
    g_lag = [[None, None], [None, None]]
    for d in range(2):
        a_re, a_im = ar_re_ref[d, 0], ar_im_ref[d, 0]
        dt = jnp.exp(ldr_ref[d, 0])
        lam_re, lam_im = _cexp(1.0, a_re, a_im, dt)
        den = a_re * a_re + a_im * a_im
        nr, ni = lam_re - 1.0, lam_im
        zr = (nr * a_re + ni * a_im) / den
        zi = (ni * a_re - nr * a_im) / den
        bb_re = zr * bt_re_ref[d, 0] - zi * bt_im_ref[d, 0]
        bb_im = zr * bt_im_ref[d, 0] + zi * bt_re_ref[d, 0]
        p_re, p_im = _cexp((t - 1.0) - tok_row if d == 0 else tok_row, a_re, a_im, dt)
        bt_re = jnp.concatenate([bb_re] * t, axis=0)
        bt_im = jnp.concatenate([bb_im] * t, axis=0)
        w_re, w_im = _cmul(p_re, p_im, bt_re, bt_im)
        top = jnp.concatenate([jnp.where(low, w_re, 0.0), jnp.where(low, w_im, 0.0)], axis=1)
        bot = jnp.concatenate([jnp.where(low, 0.0, w_re), jnp.where(low, 0.0, w_im)], axis=1)
        w_ref[d, 0] = jnp.concatenate([top, bot], axis=0).astype(w_ref.dtype)
        l16_re, l16_im = _cexp(float(t), a_re, a_im, dt)
        lam_ref[d, 0] = jnp.broadcast_to(jnp.concatenate([l16_re, l16_im], axis=1), (S5_SEGS, 2 * STATE_W))
        ce_lag, v_blocks = [], []
        for g2 in range(2):
            ac_re, ac_im = ac_re_ref[d, 0, g2], ac_im_ref[d, 0, g2]
            dtc = jnp.exp(ldc_ref[d, 0, g2])
            ct_re = jnp.dot(ct_re_ref[d, 0, g2], tile_i, preferred_element_type=_f32, precision=hi)
            ct_im = jnp.dot(ct_im_ref[d, 0, g2], tile_i, preferred_element_type=_f32, precision=hi)
            e_re, e_im = _cexp(lag + 1.0 if d == 0 else float(t) - lag, ac_re, ac_im, dtc)
            v_blocks.append(_cmul(ct_re, ct_im, e_re, e_im))
            k_re, k_im = _cexp(lag if d == 0 else (t - 1.0) - lag, ac_re, ac_im, dtc)
            ce_lag.append(_cmul(ct_re, ct_im, k_re, k_im))
        zero = jnp.zeros((S5_STATE, cw), _f32)
        v_ref[d, 0] = jnp.concatenate([
            jnp.concatenate([v_blocks[0][0], zero], axis=1), jnp.concatenate([zero, v_blocks[1][0]], axis=1),
            jnp.concatenate([-v_blocks[0][1], zero], axis=1), jnp.concatenate([zero, -v_blocks[1][1]], axis=1),
        ], axis=0).astype(v_ref.dtype)
        ce_re = jnp.concatenate([ce_lag[0][0], ce_lag[1][0]], axis=0)
        ce_im = jnp.concatenate([ce_lag[0][1], ce_lag[1][1]], axis=0)
        for g2 in range(2):
            sel = low if g2 == 0 else ~low
            g_lag[d][g2] = (
                jnp.dot(jnp.where(sel, bb_re, 0.0), ce_re, preferred_element_type=_f32, precision=hi)
                - jnp.dot(jnp.where(sel, bb_im, 0.0), ce_im, preferred_element_type=_f32, precision=hi))
    for g2 in range(2):
        rows = []
        for s in range(t):
            fwd = g_lag[0][g2] if s == 0 else pltpu.roll(g_lag[0][g2], S5_GROUP * s, 1)
            fwd = jnp.where(lane_c >= S5_GROUP * s, fwd, 0.0)
            shift = (cw - S5_GROUP * (t - 1 - s)) % cw
            bwd = g_lag[1][g2] if shift == 0 else pltpu.roll(g_lag[1][g2], shift, 1)
            bwd = jnp.where(lane_c < S5_GROUP * (s + 1), bwd, 0.0)
            rows.append(fwd + bwd)
        m_ref[g2] = jnp.concatenate(rows, axis=0).astype(m_ref.dtype)


def _cmul(ar, ai, br, bi):
    return ar * br - ai * bi, ar * bi + ai * br


def _s5_kernel(chained, u_ref, m_ref, w_ref, v_ref, lam_ref, y_ref, x_scr, h_scr):
    rows = u_ref.shape[1]
    nc = rows // S5_SEGS
    rb = min(512, rows)
    sw = STATE_W

    def inc_body(r, carry):
        rs = pl.ds(pl.multiple_of(r * rb, rb), rb)
        u = u_ref[0, rs, :]
        for d in range(2):
            x_scr[d, rs, :] = jnp.dot(u, w_ref[d, 0], preferred_element_type=_f32)
        return carry

    lax.fori_loop(0, rows // rb, inc_body, 0)

    lam = [(lam_ref[d, 0, :, :sw], lam_ref[d, 0, :, sw:]) for d in range(2)]

    def row_of(d, c):
        cc = c if d == 0 else nc - 1 - c
        return pl.ds(pl.multiple_of(cc * S5_SEGS, S5_SEGS), S5_SEGS)

    def scan_body(c, carry):
        new = []
        for d in range(2):
            hr, hi = carry[d]
            rs = row_of(d, c)
            h_scr[d, rs, :sw] = hr
            h_scr[d, rs, sw:] = hi
            pr, pi = _cmul(lam[d][0], lam[d][1], hr, hi)
            new.append((pr + x_scr[d, rs, :sw], pi + x_scr[d, rs, sw:]))
        return tuple(new)

    zero = jnp.zeros((S5_SEGS, sw), _f32)
    ends = lax.fori_loop(0, nc, scan_body, ((zero, zero), (zero, zero)), unroll=4)

    if chained:
        one = jnp.ones((S5_SEGS, sw), _f32)
        h_in = []
        for d in range(2):
            er, ei = ends[d]
            dr, di = one[0:1], zero[0:1]
            br, bi = lam[d][0][0:1], lam[d][1][0:1]
            n = nc
            while n:
                if n & 1:
                    dr, di = _cmul(dr, di, br, bi)
                n >>= 1
                if n:
                    br, bi = _cmul(br, bi, br, bi)
            order = range(S5_SEGS) if d == 0 else range(S5_SEGS - 1, -1, -1)
            cr = jnp.zeros((1, sw), _f32)
            ci = jnp.zeros((1, sw), _f32)
            rows_r = [None] * S5_SEGS
            rows_i = [None] * S5_SEGS
            for sgm in order:
                rows_r[sgm], rows_i[sgm] = cr, ci
                pr, pi = _cmul(dr, di, cr, ci)
                cr, ci = pr + er[sgm:sgm + 1], pi + ei[sgm:sgm + 1]
            h_in.append((jnp.concatenate(rows_r, axis=0), jnp.concatenate(rows_i, axis=0)))

        def fix_body(c, carry):
            new = []
            for d in range(2):
                pr, pi = carry[d]
                rs = row_of(d, c)
                ar, ai = _cmul(pr, pi, h_in[d][0], h_in[d][1])
                h_scr[d, rs, :sw] = h_scr[d, rs, :sw] + ar
                h_scr[d, rs, sw:] = h_scr[d, rs, sw:] + ai
                new.append(_cmul(lam[d][0], lam[d][1], pr, pi))
            return tuple(new)

        lax.fori_loop(0, nc, fix_body, ((one, zero), (one, zero)), unroll=4)

    half = PAIR_W // 2

    def out_body(r, carry):
        rs = pl.ds(pl.multiple_of(r * rb, rb), rb)
        u = u_ref[0, rs, :]
        y = (jnp.dot(h_scr[0, rs, :].astype(_bf16), v_ref[0, 0], preferred_element_type=_f32)
             + jnp.dot(h_scr[1, rs, :].astype(_bf16), v_ref[1, 0], preferred_element_type=_f32))
        y0 = y[:, :half] + jnp.dot(u[:, :half], m_ref[0], preferred_element_type=_f32)
        y1 = y[:, half:] + jnp.dot(u[:, half:], m_ref[1], preferred_element_type=_f32)
        y_ref[0, rs, :half] = y0.astype(y_ref.dtype)
        y_ref[0, rs, half:] = y1.astype(y_ref.dtype)
        return carry

    lax.fori_loop(0, rows // rb, out_body, 0)


def _s5_core(u_t, tables, chained):
    m_sum, w_in, v_out, lam_t = tables
    _, rows, _ = u_t.shape
    return pl.pallas_call(
        functools.partial(_s5_kernel, chained),
        grid=(S5_PAIRS,),
        in_specs=[
            pl.BlockSpec((1, rows, PAIR_W), lambda p: (p, 0, 0)),
            pl.BlockSpec((2, PAIR_W // 2, PAIR_W // 2), lambda p: (p, 0, 0)),
            pl.BlockSpec((2, 1, PAIR_W, 2 * STATE_W), lambda p: (0, p, 0, 0)),
            pl.BlockSpec((2, 1, 2 * STATE_W, PAIR_W), lambda p: (0, p, 0, 0)),
            pl.BlockSpec((2, 1, S5_SEGS, 2 * STATE_W), lambda p: (0, p, 0, 0)),
        ],
        out_specs=pl.BlockSpec((1, rows, PAIR_W), lambda p: (p, 0, 0)),
        out_shape=jax.ShapeDtypeStruct((S5_PAIRS, rows, PAIR_W), _bf16),
        scratch_shapes=[pltpu.VMEM((2, rows, 2 * STATE_W), _f32),
                        pltpu.VMEM((2, rows, 2 * STATE_W), _f32)],
        compiler_params=_cparams(("parallel",)),
        name="s5_core",
    )(u_t, m_sum, w_in, v_out, lam_t)


def _gelu_tanh(x):
    return 0.5 * x * (1.0 + jnp.tanh(math.sqrt(2.0 / math.pi) * (x + 0.044715 * (x * x * x))))


def _post_kernel(mixer, *refs):
    if mixer == "attn":
        x_ref, mix_ref, mod_ref, gain_ref, wp_ref, w1_ref, w2_ref, o_ref, x1_scr, h_scr, acc_ref = refs
    else:
        (x_ref, mix_ref, mod_ref, gain_ref, skip_ref, pt_ref, wp_ref, w1_ref, w2_ref, o_ref,
         x1_scr, h_scr, acc_ref, nat_scr, lhs_scr) = refs
    tmc = x_ref.shape[1]

    def rows(seg):
        return slice(seg * tmc, (seg + 1) * tmc)

    if mixer == "attn":
        for seg in range(S5_SEGS):
            h_scr[rows(seg), :] = mix_ref[seg]
        y = jnp.dot(h_scr[...], wp_ref[...], preferred_element_type=_f32)
    else:
        nck = tmc // S5_CHUNK
        blk = nck * S5_SEGS
        for lt in range(LANE_TILES):
            for h in range(2):
                for gl in range(8):
                    g = 8 * lt + gl
                    col = _chunk_col(g, h)
                    lhs_scr[(lt * 2 + h) * blk:(lt * 2 + h + 1) * blk, gl * 128:(gl + 1) * 128] = (
                        mix_ref[g // 2, :, col:col + 128])
        a = lhs_scr[...]
        for sp in range(4):
            out = jnp.dot(a, pt_ref[:, sp * 256:(sp + 1) * 256], preferred_element_type=_f32)
            for lt in range(LANE_TILES):
                for h in range(2):
                    for s in (2 * sp, 2 * sp + 1):
                        for c in range(nck):
                            tok = c * S5_CHUNK + 8 * h + s
                            row = (lt * 2 + h) * blk + c * S5_SEGS
                            nat_scr[lt, tok * S5_SEGS:(tok + 1) * S5_SEGS, :] = (
                                out[row:row + S5_SEGS, (s % 2) * 128:(s % 2 + 1) * 128])
        for seg in range(S5_SEGS):
            mod = mod_ref[seg]
            u = _norm_mod(x_ref[seg], gain_ref[0:1, :], mod[1:2, :], mod[0:1, :])
            y_ssm = jnp.concatenate([nat_scr[lt, pl.ds(seg, tmc, stride=S5_SEGS), :]
                                     for lt in range(D_MODEL // 128)], axis=1)
            h_scr[rows(seg), :] = _gelu_tanh(skip_ref[...] * u + y_ssm).astype(_bf16)
        g = h_scr[...]
        val = jnp.dot(g, wp_ref[:, :D_MODEL], preferred_element_type=_f32)
        gate = jnp.dot(g, wp_ref[:, D_MODEL:], preferred_element_type=_f32)
        y = val * jax.nn.sigmoid(gate)
    for seg in range(S5_SEGS):
        mod = mod_ref[seg]
        x1 = x_ref[seg] + mod[2:3, :] * _rms(y[rows(seg), :], gain_ref[1:2, :])
        x1_scr[rows(seg), :] = x1
        h_scr[rows(seg), :] = _norm_mod(x1, gain_ref[2:3, :], mod[4:5, :], mod[3:4, :]).astype(_bf16)
    h2 = h_scr[...]
    tf = 512
    for kf in range(D_FF // tf):
        cols = slice(kf * tf, (kf + 1) * tf)
        a = jnp.dot(h2, w1_ref[:, cols], preferred_element_type=_f32)
        a = jnp.square(jnp.maximum(a, 0.0)).astype(_bf16)
        part = jnp.dot(a, w2_ref[cols, :], preferred_element_type=_f32)
        if kf == 0:
            acc_ref[...] = part
        else:
            acc_ref[...] += part
    for seg in range(S5_SEGS):
        mod = mod_ref[seg]
        o_ref[seg] = x1_scr[rows(seg), :] + mod[5:6, :] * _rms(acc_ref[rows(seg), :], gain_ref[3:4, :])


def _post_mixer(mixer, x8, mix, mod8, gains, w_proj, w1, w2, skip=None, perm_t=None, tmc=64):
    _, ls, _ = x8.shape
    tok = lambda j: (0, j, 0)
    const2 = lambda j: (0, 0)
    m = S5_SEGS * tmc
    in_specs = [pl.BlockSpec((S5_SEGS, tmc, D_MODEL), tok)]
    scratch = [pltpu.VMEM((m, D_MODEL), _f32), pltpu.VMEM((m, D_MODEL), _bf16), pltpu.VMEM((m, D_MODEL), _f32)]
    if mixer == "attn":
        in_specs.append(pl.BlockSpec((S5_SEGS, tmc, D_MODEL), tok))
    else:
        in_specs.append(pl.BlockSpec((S5_PAIRS, tmc // S5_CHUNK * S5_SEGS, PAIR_W), tok))
        scratch += [pltpu.VMEM((LANE_TILES, m, 128), _f32), pltpu.VMEM((m, D_MODEL), _bf16)]
    in_specs += [_resident((S5_SEGS, 6, D_MODEL), lambda j: (0, 0, 0)), _resident((4, D_MODEL), const2)]
    args = [x8, mix, mod8, gains]
    if mixer == "s5":
        in_specs += [_resident((1, D_MODEL), const2), _resident((D_MODEL, D_MODEL), const2)]
        args += [skip, perm_t]
    in_specs += [_resident(w_proj.shape, const2), _resident(w1.shape, const2), _resident(w2.shape, const2)]
    args += [w_proj, w1, w2]
    return pl.pallas_call(
        functools.partial(_post_kernel, mixer),
        grid=(ls // tmc,),
        in_specs=in_specs,
        out_specs=pl.BlockSpec((S5_SEGS, tmc, D_MODEL), tok),
        out_shape=jax.ShapeDtypeStruct(x8.shape, _f32),
        scratch_shapes=scratch,
        compiler_params=_cparams(("parallel",)),
        name="post_" + mixer,
    )(*args)


def _trunk(x, mod, params, chained):
    b, l, _ = x.shape
    assert (b == 1) if chained else (b == S5_SEGS)
    ls = b * l // S5_SEGS
    gains = params["norm_gain"]
    mod8 = jnp.broadcast_to(mod, (DEPTH, S5_SEGS, 6, D_MODEL))
    qkv = _qkv_proj(x, mod[0], gains[0], params["w_qkv"])
    att = _attention(qkv, params["attn_bias"])
    x8 = _post_mixer("attn", x.reshape(S5_SEGS, ls, D_MODEL), att.reshape(S5_SEGS, ls, D_MODEL), mod8[0], gains[0],
                     params["w_o"], params["ffn_w1"][0], params["ffn_w2"][0])
    u = _s5_input(x8, mod8[1], gains[1], _lane_perm())
    y = _s5_core(u, params["s5_tables"], chained)
    x8 = _post_mixer("s5", x8, y, mod8[1], gains[1], params["w_glu"], params["ffn_w1"][1], params["ffn_w2"][1],
                     skip=params["s5_d"], perm_t=_lane_perm(inverse=True))
    return x8.reshape(b, l, D_MODEL)


def kernel(x_prompt, x_sample, c_prompt, c_sample, ada_w, ada_b, norm_gain, attn_w_qkv, attn_w_o, attn_rpb,
           s5_a_re, s5_a_im, s5_log_dt, s5_b_re, s5_b_im, s5_c_re, s5_c_im, s5_d, s5_w_glu, ffn_w1, ffn_w2):
    nbp, nbs = c_prompt.shape[0], c_sample.shape[0]
    nb = -(-(nbp + nbs) // 8) * 8
    c_all = jnp.concatenate([c_prompt, c_sample, jnp.zeros((nb - nbp - nbs, D_MODEL), _f32)], axis=0)
    mod = _modulation(c_all, ada_w, ada_b)
    mod_p = mod[:, :nbp].reshape(DEPTH, nbp, 6, D_MODEL)
    mod_s = mod[:, nbp:nbp + nbs].reshape(DEPTH, nbs, 6, D_MODEL)

    params = {
        "norm_gain": norm_gain,
        "w_qkv": attn_w_qkv[0].astype(_bf16),
        "w_o": attn_w_o[0].astype(_bf16),
        "attn_bias": _attn_bias_table(attn_rpb[0]),
        "s5_tables": _s5_tables(s5_a_re[0], s5_a_im[0], s5_log_dt[0], s5_b_re[0], s5_b_im[0],
                                s5_c_re[0], s5_c_im[0]),
        "s5_d": s5_d[0].reshape(1, D_MODEL),
        "w_glu": s5_w_glu[0].astype(_bf16),
        "ffn_w1": ffn_w1.astype(_bf16),
        "ffn_w2": ffn_w2.astype(_bf16),
    }
    y_prompt = _trunk(x_prompt, mod_p, params, chained=False)
    y_sample = _trunk(x_sample, mod_s, params, chained=True)
    return (y_prompt, y_sample)
```

```python
import functools
import math

import numpy as np
import jax
import jax.numpy as jnp
from jax import lax
from jax.experimental import pallas as pl
from jax.experimental.pallas import tpu as pltpu

D_MODEL = 1024
D_FF = 4 * D_MODEL
DEPTH = 2
EPS = 1e-6
NEG_INF = -1e30
LOG2E = math.log2(math.e)

GRID_W = 64
N_HEADS = 16
HEAD_DIM = D_MODEL // N_HEADS
WIN_H = 8
WIN_W = 16
ROWS_PER_TILE = 4
TILE_TOK = ROWS_PER_TILE * GRID_W
KEY_TILES = 3
S5_GROUP = 16
S5_GROUPS = D_MODEL // S5_GROUP
S5_STATE = 64
S5_CHUNK = 16
S5_SEGS = 8
S5_PAIRS = S5_GROUPS // 2
PAIR_W = 2 * S5_CHUNK * S5_GROUP
STATE_W = 2 * S5_STATE
NPOW = 24

VMEM_LIMIT = 56 * 1024 * 1024

_f32 = jnp.float32
_bf16 = jnp.bfloat16


def _cparams(sem):
    return pltpu.CompilerParams(dimension_semantics=sem, vmem_limit_bytes=VMEM_LIMIT)


def _resident(shape, index_map):
    return pl.BlockSpec(shape, index_map, pipeline_mode=pl.Buffered(1))


def _rms(x, gain):
    ms = jnp.mean(x * x, axis=-1, keepdims=True)
    return x * lax.rsqrt(ms + EPS) * gain


def _norm_mod(x, gain, scale, shift):
    return _rms(x, gain) * (1.0 + scale) + shift


def _mod_kernel(c_ref, w_ref, b_ref, o_ref):
    c = c_ref[...]
    act = c * jax.nn.sigmoid(c)
    o_ref[0] = jnp.dot(act, w_ref[0], preferred_element_type=_f32,
                       precision=lax.Precision.HIGHEST) + b_ref[0]


def _modulation(c_all, ada_w, ada_b):
    nb = c_all.shape[0]
    tn = 1536
    return pl.pallas_call(
        _mod_kernel,
        grid=(DEPTH, 6 * D_MODEL // tn),
        in_specs=[
            pl.BlockSpec((nb, D_MODEL), lambda i, n: (0, 0)),
            pl.BlockSpec((1, D_MODEL, tn), lambda i, n: (i, 0, n)),
            pl.BlockSpec((1, 1, tn), lambda i, n: (i, 0, n)),
        ],
        out_specs=pl.BlockSpec((1, nb, tn), lambda i, n: (i, 0, n)),
        out_shape=jax.ShapeDtypeStruct((DEPTH, nb, 6 * D_MODEL), _f32),
        compiler_params=_cparams(("arbitrary", "arbitrary")),
        name="adaln_mod",
    )(c_all, ada_w, ada_b.reshape(DEPTH, 1, 6 * D_MODEL))


def _qkv_kernel(x_ref, mod_ref, gain_ref, w_ref, o_ref):
    mod = mod_ref[0]
    h = _norm_mod(x_ref[0], gain_ref[0:1, :], mod[1:2, :], mod[0:1, :]).astype(_bf16)
    q_scale = HEAD_DIM ** -0.5 * LOG2E
    for part in range(3):
        cols = slice(part * D_MODEL, (part + 1) * D_MODEL)
        y = jnp.dot(h, w_ref[:, cols], preferred_element_type=_f32)
        if part == 0:
            y = y * q_scale
        o_ref[0, :, cols] = y.astype(_bf16)


def _qkv_proj(x, mod, gains, w_qkv, tm=1024):
    b, l, _ = x.shape
    return pl.pallas_call(
        _qkv_kernel,
        grid=(b, l // tm),
        in_specs=[
            pl.BlockSpec((1, tm, D_MODEL), lambda i, j: (i, j, 0)),
            pl.BlockSpec((1, 6, D_MODEL), lambda i, j: (i, 0, 0)),
            _resident((4, D_MODEL), lambda i, j: (0, 0)),
            _resident((D_MODEL, 3 * D_MODEL), lambda i, j: (0, 0)),
        ],
        out_specs=pl.BlockSpec((1, tm, 3 * D_MODEL), lambda i, j: (i, j, 0)),
        out_shape=jax.ShapeDtypeStruct((b, l, 3 * D_MODEL), _bf16),
        compiler_params=_cparams(("parallel", "parallel")),
        name="norm_qkv",
    )(x, mod, gains, w_qkv)


LANE_TILES = D_MODEL // 128


def _lane_perm(inverse=False):
    eye = np.eye
    p = np.einsum('sS,gG,iI->sgiGSI', eye(8), eye(8), eye(S5_GROUP)).reshape(D_MODEL, D_MODEL)
    return jnp.asarray(p.T if inverse else p, _bf16)


def _chunk_col(g, h):
    return (g % 2) * (PAIR_W // 2) + h * 128


def _s5_in_kernel(x_ref, mod_ref, gain_ref, p_ref, o_ref, ubuf, lhs):
    tmc = x_ref.shape[1]
    nck = tmc // S5_CHUNK
    blk = nck * S5_SEGS
    for seg in range(S5_SEGS):
        mod = mod_ref[seg]
        u = _norm_mod(x_ref[seg], gain_ref[0:1, :], mod[1:2, :], mod[0:1, :])
        for lt in range(LANE_TILES):
            ubuf[lt] = u[:, lt * 128:(lt + 1) * 128]
        for lt in range(LANE_TILES):
            for h in range(2):
                row0 = (lt * 2 + h) * blk + seg
                for s in range(8):
                    lhs[s, pl.ds(row0, nck, stride=S5_SEGS), :] = ubuf[lt, pl.ds(8 * h + s, nck, stride=S5_CHUNK), :]
    a = jnp.concatenate([lhs[s] for s in range(8)], axis=1).astype(_bf16)
    for gp in range(4):
        out = jnp.dot(a, p_ref[:, gp * 256:(gp + 1) * 256], preferred_element_type=_f32)
        for lt in range(LANE_TILES):
            for h in range(2):
                for gl in (2 * gp, 2 * gp + 1):
                    g = 8 * lt + gl
                    col = _chunk_col(g, h)
                    piece = out[(lt * 2 + h) * blk:(lt * 2 + h + 1) * blk, (gl % 2) * 128:(gl % 2 + 1) * 128]
                    o_ref[g // 2, :, col:col + 128] = piece.astype(o_ref.dtype)


def _s5_input(x8, mod8, gains, perm, tmc=128):
    _, ls, _ = x8.shape
    rows_blk = tmc // S5_CHUNK * S5_SEGS
    return pl.pallas_call(
        _s5_in_kernel,
        grid=(ls // tmc,),
        in_specs=[
            pl.BlockSpec((S5_SEGS, tmc, D_MODEL), lambda j: (0, j, 0)),
            _resident((S5_SEGS, 6, D_MODEL), lambda j: (0, 0, 0)),
            _resident((4, D_MODEL), lambda j: (0, 0)),
            _resident((D_MODEL, D_MODEL), lambda j: (0, 0)),
        ],
        out_specs=pl.BlockSpec((S5_PAIRS, rows_blk, PAIR_W), lambda j: (0, j, 0)),
        out_shape=jax.ShapeDtypeStruct((S5_PAIRS, ls // S5_CHUNK * S5_SEGS, PAIR_W), _bf16),
        scratch_shapes=[pltpu.VMEM((LANE_TILES, tmc, 128), _f32),
                        pltpu.VMEM((8, 2 * LANE_TILES * rows_blk, 128), _f32)],
        compiler_params=_cparams(("parallel",)),
        name="s5_input",
    )(x8, mod8, gains, perm)


def _attn_bias_table(rpb):
    n_dr = 2 * WIN_H - 1
    i = np.arange(ROWS_PER_TILE)[:, None]
    kk = np.arange(KEY_TILES * ROWS_PER_TILE)[None, :]
    dr = np.stack([kk - i + 7, kk - i + 3, kk - i - 1])
    row_ok = np.stack([(kk < WIN_H) & (i >= 0), (kk >= i) & (kk < i + WIN_H), (kk >= 4) & (i >= 0)])
    dr = np.where(row_ok, dr, n_dr)
    qc = np.arange(GRID_W)[:, None]
    kc = np.arange(GRID_W)[None, :]
    col_start = np.clip(qc - WIN_W // 2, 0, GRID_W - WIN_W)
    col_ok = (kc >= col_start) & (kc < col_start + WIN_W)
    pad = GRID_W - WIN_W
    rp = jnp.pad(rpb.astype(_f32) * LOG2E, ((0, 0), (0, 0), (pad, pad)))
    t = jnp.stack([rp[:, :, GRID_W - 1 - q:2 * GRID_W - 1 - q] for q in range(GRID_W)], axis=2)
    t = jnp.where(col_ok[None, None], t, NEG_INF)
    t = jnp.concatenate([t, jnp.full((N_HEADS, 1, GRID_W, GRID_W), NEG_INF, _f32)], axis=1).astype(_bf16)
    even = jnp.take(t, dr[:, :, 0::2].reshape(-1), axis=1)
    odd = jnp.take(t, dr[:, :, 1::2].reshape(-1), axis=1)
    b = jnp.concatenate([even, odd], axis=-1)
    b = b.reshape(N_HEADS, 3, ROWS_PER_TILE, KEY_TILES * 2, GRID_W, 2 * GRID_W)
    return jnp.transpose(b, (1, 0, 2, 3, 4, 5))


def _attn_heads(interior, q_ref, k_refs, v_refs, bias_ref, o_ref):
    lane = lax.broadcasted_iota(jnp.int32, (1, 128), 1)
    low = lane < HEAD_DIM
    head_mask = [low.astype(_bf16), (~low).astype(_bf16)]
    head_of_lane = jnp.right_shift(lax.broadcasted_iota(jnp.int32, (1, 256), 1), HEAD_DIM.bit_length() - 1)
    n_pairs = 2 * KEY_TILES
    zero_blk = jnp.zeros((GRID_W, 2 * GRID_W), _bf16)
    for hq in range(N_HEADS // 4):
        vcols = slice(hq * 256, (hq + 1) * 256)
        acc = None
        for h4 in range(4):
            head = 4 * hq + h4
            cols = slice((head // 2) * 128, (head // 2 + 1) * 128)
            qm = q_ref[0, :, cols] * head_mask[head % 2]
            s = [lax.dot_general(qm, k_refs[t][0, :, cols], (((1,), (1,)), ((), ())),
                                 preferred_element_type=_f32) for t in range(KEY_TILES)]
            p_rows, inv_l = [], []
            for i in range(ROWS_PER_TILE):
                pairs = range(i // 2, (i + WIN_H - 1) // 2 + 1) if interior else range(n_pairs)
                rows = slice(i * GRID_W, (i + 1) * GRID_W)
                blk = {pr: s[pr // 2][rows, (pr % 2) * 128:(pr % 2 + 1) * 128]
                       + bias_ref[0, head, i, pr].astype(_f32) for pr in pairs}
                m = None
                for pr in pairs:
                    m = blk[pr] if m is None else jnp.maximum(m, blk[pr])
                m = jnp.max(m, axis=-1, keepdims=True)
                e = {pr: jnp.exp2(blk[pr] - m) for pr in pairs}
                tot = None
                for pr in pairs:
                    tot = e[pr] if tot is None else tot + e[pr]
                inv_l.append(1.0 / jnp.sum(tot, axis=-1, keepdims=True))
                p_rows.append([e[pr].astype(_bf16) if pr in e else zero_blk for pr in range(n_pairs)])
            o = None
            for t in range(KEY_TILES):
                p_t = jnp.concatenate(
                    [jnp.concatenate([p_rows[i][2 * t], p_rows[i][2 * t + 1]], axis=1)
                     for i in range(ROWS_PER_TILE)], axis=0)
                part = jnp.dot(p_t, v_refs[t][0, :, vcols], preferred_element_type=_f32)
                o = part if o is None else o + part
            o = o * jnp.concatenate(inv_l, axis=0)
            acc = o if acc is None else jnp.where(head_of_lane == h4, o, acc)
        o_ref[0, :, vcols] = acc.astype(o_ref.dtype)


def _attn_kernel(q_ref, k0_ref, k1_ref, k2_ref, v0_ref, v1_ref, v2_ref, bias_ref, o_ref):
    j = pl.program_id(1)
    interior = jnp.logical_and(j > 0, j < pl.num_programs(1) - 1)
    args = (q_ref, (k0_ref, k1_ref, k2_ref), (v0_ref, v1_ref, v2_ref), bias_ref, o_ref)
    pl.when(interior)(functools.partial(_attn_heads, True, *args))
    pl.when(jnp.logical_not(interior))(functools.partial(_attn_heads, False, *args))


def _attention(qkv, bias):
    b, l, _ = qkv.shape
    nt = l // TILE_TOK
    assert nt >= KEY_TILES

    def base(j):
        return jnp.clip(j - 1, 0, nt - KEY_TILES)

    def pattern(j):
        return jnp.where(j == 0, 0, jnp.where(j == nt - 1, 2, 1))

    def kv_spec(part, t):
        return pl.BlockSpec((1, TILE_TOK, D_MODEL), lambda i, j: (i, base(j) + t, part))

    return pl.pallas_call(
        _attn_kernel,
        grid=(b, nt),
        in_specs=[pl.BlockSpec((1, TILE_TOK, D_MODEL), lambda i, j: (i, j, 0))]
        + [kv_spec(1, t) for t in range(KEY_TILES)]
        + [kv_spec(2, t) for t in range(KEY_TILES)]
        + [pl.BlockSpec((1, N_HEADS, ROWS_PER_TILE, KEY_TILES * 2, GRID_W, 2 * GRID_W),
                        lambda i, j: (pattern(j), 0, 0, 0, 0, 0))],
        out_specs=pl.BlockSpec((1, TILE_TOK, D_MODEL), lambda i, j: (i, j, 0)),
        out_shape=jax.ShapeDtypeStruct((b, l, D_MODEL), _bf16),
        compiler_params=_cparams(("parallel", "arbitrary")),
        name="nbr_attention",
    )(qkv, qkv, qkv, qkv, qkv, qkv, qkv, bias)


def _s5_tables(a_re, a_im, log_dt, b_re, b_im, c_re, c_im):
    f = lambda v: v.astype(_f32)
    pairs = (2, S5_PAIRS)
    a_row = [f(a).reshape(*pairs, 1, STATE_W) for a in (a_re, a_im)]
    a_col = [f(a).reshape(*pairs, STATE_W, 1) for a in (a_re, a_im)]
    ldt = jnp.repeat(f(log_dt), S5_STATE, axis=-1)
    ldt_row = ldt.reshape(*pairs, 1, STATE_W)
    ldt_col = ldt.reshape(*pairs, STATE_W, 1)
    b_t = [jnp.transpose(f(b).reshape(*pairs, 2, S5_STATE, S5_GROUP), (0, 1, 4, 2, 3))
           .reshape(*pairs, S5_GROUP, STATE_W) for b in (b_re, b_im)]
    c_t = [jnp.transpose(f(c), (0, 1, 3, 2)).reshape(*pairs, STATE_W, S5_GROUP) for c in (c_re, c_im)]

    def spec(shape):
        nd = len(shape)
        return pl.BlockSpec((2, 1) + tuple(shape[2:]), lambda p: (0, p) + (0,) * (nd - 2))

    ins = a_row + a_col + [ldt_row, ldt_col] + b_t + c_t
    out_shapes = (
        jax.ShapeDtypeStruct((S5_GROUPS, PAIR_W // 2, PAIR_W // 2), _bf16),
        jax.ShapeDtypeStruct((2, S5_PAIRS, PAIR_W, 2 * STATE_W), _bf16),
        jax.ShapeDtypeStruct((2, S5_PAIRS, 2 * STATE_W, PAIR_W), _bf16),
        jax.ShapeDtypeStruct((2, S5_PAIRS, S5_SEGS, 2 * STATE_W), _f32),
    )
    return pl.pallas_call(
        _s5_tables_kernel,
        grid=(S5_PAIRS,),
        in_specs=[spec(v.shape) for v in ins],
        out_specs=(pl.BlockSpec((2, PAIR_W // 2, PAIR_W // 2), lambda p: (p, 0, 0)),
                   spec(out_shapes[1].shape), spec(out_shapes[2].shape), spec(out_shapes[3].shape)),
        out_shape=out_shapes,
        compiler_params=_cparams(("parallel",)),
        name="s5_tables",
    )(*ins)


def _cexp(k, a_re, a_im, dt):
    mag = jnp.exp(k * (a_re * dt))
    ang = k * (a_im * dt)
    return mag * jnp.cos(ang), mag * jnp.sin(ang)


def _s5_tables_kernel(ar_re_ref, ar_im_ref, ac_re_ref, ac_im_ref, ldr_ref, ldc_ref,
                      bt_re_ref, bt_im_ref, ct_re_ref, ct_im_ref, m_ref, w_ref, v_ref, lam_ref):
    hi = lax.Precision.HIGHEST
    t = S5_CHUNK
    cw = t * S5_GROUP
    lane_c = lax.broadcasted_iota(jnp.int32, (1, cw), 1)
    log_group = S5_GROUP.bit_length() - 1
    lag = jnp.right_shift(lane_c, log_group).astype(_f32)
    tok_row = jnp.right_shift(lax.broadcasted_iota(jnp.int32, (cw, 1), 0), log_group).astype(_f32)
    low = lax.broadcasted_iota(jnp.int32, (1, STATE_W), 1) < S5_STATE
    tile_i = (lax.broadcasted_iota(jnp.int32, (S5_GROUP, cw), 0)
              == jnp.bitwise_and(lax.broadcasted_iota(jnp.int32, (S5_GROUP, cw), 1), S5_GROUP - 1)).astype(_f32)
    k_sub = jnp.minimum(lax.broadcasted_iota(jnp.int32, (NPOW, 1), 0), t).astype(_f32)
    k_lane = jnp.minimum(lax.broadcasted_iota(jnp.int32, (1, STATE_W), 1), t).astype(_f32)
    pow_of_row = lax.broadcasted_iota(jnp.int32, (cw, NPOW), 1).astype(_f32)
    pow_of_lane = lax.broadcasted_iota(jnp.int32, (STATE_W, cw), 0).astype(_f32)

    def spread(rows, sel):
        return jnp.dot(rows, sel, preferred_element_type=_f32, precision=hi)

    g_lag = [[None, None], [None, None]]
    for d in range(2):
        a_re, a_im = ar_re_ref[d, 0], ar_im_ref[d, 0]
        pw_re, pw_im = _cexp(k_sub, a_re, a_im, jnp.exp(ldr_ref[d, 0]))
        lam_re, lam_im = pw_re[1:2], pw_im[1:2]
        den = a_re * a_re + a_im * a_im
        nr, ni = lam_re - 1.0, lam_im
        zr = (nr * a_re + ni * a_im) / den
        zi = (ni * a_re - nr * a_im) / den
        bb_re = zr * bt_re_ref[d, 0] - zi * bt_im_ref[d, 0]
        bb_im = zr * bt_im_ref[d, 0] + zi * bt_re_ref[d, 0]
        sel_tok = (pow_of_row == ((t - 1.0) - tok_row if d == 0 else tok_row)).astype(_f32)
        p_re, p_im = spread(sel_tok, pw_re), spread(sel_tok, pw_im)
        bt_re = jnp.concatenate([bb_re] * t, axis=0)
        bt_im = jnp.concatenate([bb_im] * t, axis=0)
        w_re, w_im = _cmul(p_re, p_im, bt_re, bt_im)
        top = jnp.concatenate([jnp.where(low, w_re, 0.0), jnp.where(low, w_im, 0.0)], axis=1)
        bot = jnp.concatenate([jnp.where(low, 0.0, w_re), jnp.where(low, 0.0, w_im)], axis=1)
        w_ref[d, 0] = jnp.concatenate([top, bot], axis=0).astype(w_ref.dtype)
        lam_ref[d, 0] = jnp.broadcast_to(jnp.concatenate([pw_re[t:t + 1], pw_im[t:t + 1]], axis=1),
                                         (S5_SEGS, 2 * STATE_W))
        pc_re, pc_im = _cexp(k_lane, ac_re_ref[d, 0], ac_im_ref[d, 0], jnp.exp(ldc_ref[d, 0]))
        ct_re = spread(ct_re_ref[d, 0], tile_i)
        ct_im = spread(ct_im_ref[d, 0], tile_i)

        def c_pow(k_of_block):
            sel = (pow_of_lane == k_of_block).astype(_f32)
            return _cmul(ct_re, ct_im, spread(pc_re, sel), spread(pc_im, sel))

        vb_re, vb_im = c_pow(lag + 1.0 if d == 0 else float(t) - lag)
        zero = jnp.zeros((S5_STATE, cw), _f32)
        v_ref[d, 0] = jnp.concatenate([
            jnp.concatenate([vb_re[:S5_STATE], zero], axis=1), jnp.concatenate([zero, vb_re[S5_STATE:]], axis=1),
            jnp.concatenate([-vb_im[:S5_STATE], zero], axis=1), jnp.concatenate([zero, -vb_im[S5_STATE:]], axis=1),
        ], axis=0).astype(v_ref.dtype)
        ce_re, ce_im = c_pow(lag if d == 0 else (t - 1.0) - lag)
        for g2 in range(2):
            sel = low if g2 == 0 else ~low
            g_lag[d][g2] = (
                jnp.dot(jnp.where(sel, bb_re, 0.0), ce_re, preferred_element_type=_f32, precision=hi)
                - jnp.dot(jnp.where(sel, bb_im, 0.0), ce_im, preferred_element_type=_f32, precision=hi))
    for g2 in range(2):
        rows = []
        for s in range(t):
            fwd = g_lag[0][g2] if s == 0 else pltpu.roll(g_lag[0][g2], S5_GROUP * s, 1)
            fwd = jnp.where(lane_c >= S5_GROUP * s, fwd, 0.0)
            shift = (cw - S5_GROUP * (t - 1 - s)) % cw
            bwd = g_lag[1][g2] if shift == 0 else pltpu.roll(g_lag[1][g2], shift, 1)
            bwd = jnp.where(lane_c < S5_GROUP * (s + 1), bwd, 0.0)
            rows.append(fwd + bwd)
        m_ref[g2] = jnp.concatenate(rows, axis=0).astype(m_ref.dtype)


def _cmul(ar, ai, br, bi):
    return ar * br - ai * bi, ar * bi + ai * br


def _s5_kernel(chained, u_ref, m_ref, w_ref, v_ref, lam_ref, y_ref, x_scr, h_scr):
    rows = u_ref.shape[1]
    nc = rows // S5_SEGS
    rb = min(512, rows)
    sw = STATE_W

    def inc_body(r, carry):
        rs = pl.ds(pl.multiple_of(r * rb, rb), rb)
        u = u_ref[0, rs, :]
        for d in range(2):
            x_scr[d, rs, :] = jnp.dot(u, w_ref[d, 0], preferred_element_type=_f32)
        return carry

    lax.fori_loop(0, rows // rb, inc_body, 0)

    lam = [(lam_ref[d, 0, :, :sw], lam_ref[d, 0, :, sw:]) for d in range(2)]

    def row_of(d, c):
        cc = c if d == 0 else nc - 1 - c
        return pl.ds(pl.multiple_of(cc * S5_SEGS, S5_SEGS), S5_SEGS)

    def scan_body(c, carry):
        new = []
        for d in range(2):
            hr, hi = carry[d]
            rs = row_of(d, c)
            h_scr[d, rs, :sw] = hr
            h_scr[d, rs, sw:] = hi
            pr, pi = _cmul(lam[d][0], lam[d][1], hr, hi)
            new.append((pr + x_scr[d, rs, :sw], pi + x_scr[d, rs, sw:]))
        return tuple(new)

    zero = jnp.zeros((S5_SEGS, sw), _f32)
    ends = lax.fori_loop(0, nc, scan_body, ((zero, zero), (zero, zero)), unroll=4)

    if chained:
        one = jnp.ones((S5_SEGS, sw), _f32)
        h_in = []
        for d in range(2):
            er, ei = ends[d]
            dr, di = one[0:1], zero[0:1]
            br, bi = lam[d][0][0:1], lam[d][1][0:1]
            n = nc
            while n:
                if n & 1:
                    dr, di = _cmul(dr, di, br, bi)
                n >>= 1
                if n:
                    br, bi = _cmul(br, bi, br, bi)
            order = range(S5_SEGS) if d == 0 else range(S5_SEGS - 1, -1, -1)
            cr = jnp.zeros((1, sw), _f32)
            ci = jnp.zeros((1, sw), _f32)
            rows_r = [None] * S5_SEGS
            rows_i = [None] * S5_SEGS
            for sgm in order:
                rows_r[sgm], rows_i[sgm] = cr, ci
                pr, pi = _cmul(dr, di, cr, ci)
                cr, ci = pr + er[sgm:sgm + 1], pi + ei[sgm:sgm + 1]
            h_in.append((jnp.concatenate(rows_r, axis=0), jnp.concatenate(rows_i, axis=0)))

        def fix_body(c, carry):
            new = []
            for d in range(2):
                pr, pi = carry[d]
                rs = row_of(d, c)
                ar, ai = _cmul(pr, pi, h_in[d][0], h_in[d][1])
                h_scr[d, rs, :sw] = h_scr[d, rs, :sw] + ar
                h_scr[d, rs, sw:] = h_scr[d, rs, sw:] + ai
                new.append(_cmul(lam[d][0], lam[d][1], pr, pi))
            return tuple(new)

        lax.fori_loop(0, nc, fix_body, ((one, zero), (one, zero)), unroll=4)

    half = PAIR_W // 2

    def out_body(r, carry):
        rs = pl.ds(pl.multiple_of(r * rb, rb), rb)
        u = u_ref[0, rs, :]
        y = (jnp.dot(h_scr[0, rs, :].astype(_bf16), v_ref[0, 0], preferred_element_type=_f32)
             + jnp.dot(h_scr[1, rs, :].astype(_bf16), v_ref[1, 0], preferred_element_type=_f32))
        y0 = y[:, :half] + jnp.dot(u[:, :half], m_ref[0], preferred_element_type=_f32)
        y1 = y[:, half:] + jnp.dot(u[:, half:], m_ref[1], preferred_element_type=_f32)
        y_ref[0, rs, :half] = y0.astype(y_ref.dtype)
        y_ref[0, rs, half:] = y1.astype(y_ref.dtype)
        return carry

    lax.fori_loop(0, rows // rb, out_body, 0)


def _s5_core(u_t, tables, chained):
    m_sum, w_in, v_out, lam_t = tables
    _, rows, _ = u_t.shape
    return pl.pallas_call(
        functools.partial(_s5_kernel, chained),
        grid=(S5_PAIRS,),
        in_specs=[
            pl.BlockSpec((1, rows, PAIR_W), lambda p: (p, 0, 0)),
            pl.BlockSpec((2, PAIR_W // 2, PAIR_W // 2), lambda p: (p, 0, 0)),
            pl.BlockSpec((2, 1, PAIR_W, 2 * STATE_W), lambda p: (0, p, 0, 0)),
            pl.BlockSpec((2, 1, 2 * STATE_W, PAIR_W), lambda p: (0, p, 0, 0)),
            pl.BlockSpec((2, 1, S5_SEGS, 2 * STATE_W), lambda p: (0, p, 0, 0)),
        ],
        out_specs=pl.BlockSpec((1, rows, PAIR_W), lambda p: (p, 0, 0)),
        out_shape=jax.ShapeDtypeStruct((S5_PAIRS, rows, PAIR_W), _bf16),
        scratch_shapes=[pltpu.VMEM((2, rows, 2 * STATE_W), _f32),
                        pltpu.VMEM((2, rows, 2 * STATE_W), _f32)],
        compiler_params=_cparams(("parallel",)),
        name="s5_core",
    )(u_t, m_sum, w_in, v_out, lam_t)


def _gelu_tanh(x):
    return 0.5 * x * (1.0 + jnp.tanh(math.sqrt(2.0 / math.pi) * (x + 0.044715 * (x * x * x))))


def _post_kernel(mixer, *refs):
    if mixer == "attn":
        x_ref, mix_ref, mod_ref, gain_ref, wp_ref, w1_ref, w2_ref, o_ref, x1_scr, h_scr, acc_ref = refs
    else:
        (x_ref, mix_ref, mod_ref, gain_ref, skip_ref, pt_ref, wp_ref, w1_ref, w2_ref, o_ref,
         x1_scr, h_scr, acc_ref, nat_scr, lhs_scr) = refs
    tmc = x_ref.shape[1]

    def rows(seg):
        return slice(seg * tmc, (seg + 1) * tmc)

    if mixer == "attn":
        for seg in range(S5_SEGS):
            h_scr[rows(seg), :] = mix_ref[seg]
        y = jnp.dot(h_scr[...], wp_ref[...], preferred_element_type=_f32)
    else:
        nck = tmc // S5_CHUNK
        blk = nck * S5_SEGS
        for lt in range(LANE_TILES):
            for h in range(2):
                for gl in range(8):
                    g = 8 * lt + gl
                    col = _chunk_col(g, h)
                    lhs_scr[(lt * 2 + h) * blk:(lt * 2 + h + 1) * blk, gl * 128:(gl + 1) * 128] = (
                        mix_ref[g // 2, :, col:col + 128])
        a = lhs_scr[...]
        for sp in range(4):
            out = jnp.dot(a, pt_ref[:, sp * 256:(sp + 1) * 256], preferred_element_type=_f32)
            for lt in range(LANE_TILES):
                for h in range(2):
                    for s in (2 * sp, 2 * sp + 1):
                        for c in range(nck):
                            tok = c * S5_CHUNK + 8 * h + s
                            row = (lt * 2 + h) * blk + c * S5_SEGS
                            nat_scr[lt, tok * S5_SEGS:(tok + 1) * S5_SEGS, :] = (
                                out[row:row + S5_SEGS, (s % 2) * 128:(s % 2 + 1) * 128])
        for seg in range(S5_SEGS):
            mod = mod_ref[seg]
            u = _norm_mod(x_ref[seg], gain_ref[0:1, :], mod[1:2, :], mod[0:1, :])
            y_ssm = jnp.concatenate([nat_scr[lt, pl.ds(seg, tmc, stride=S5_SEGS), :]
                                     for lt in range(D_MODEL // 128)], axis=1)
            h_scr[rows(seg), :] = _gelu_tanh(skip_ref[...] * u + y_ssm).astype(_bf16)
        g = h_scr[...]
        val = jnp.dot(g, wp_ref[:, :D_MODEL], preferred_element_type=_f32)
        gate = jnp.dot(g, wp_ref[:, D_MODEL:], preferred_element_type=_f32)
        y = val * jax.nn.sigmoid(gate)
    for seg in range(S5_SEGS):
        mod = mod_ref[seg]
        x1 = x_ref[seg] + mod[2:3, :] * _rms(y[rows(seg), :], gain_ref[1:2, :])
        x1_scr[rows(seg), :] = x1
        h_scr[rows(seg), :] = _norm_mod(x1, gain_ref[2:3, :], mod[4:5, :], mod[3:4, :]).astype(_bf16)
    h2 = h_scr[...]
    tf = 512
    for kf in range(D_FF // tf):
        cols = slice(kf * tf, (kf + 1) * tf)
        a = jnp.dot(h2, w1_ref[:, cols], preferred_element_type=_f32)
        a = jnp.square(jnp.maximum(a, 0.0)).astype(_bf16)
        part = jnp.dot(a, w2_ref[cols, :], preferred_element_type=_f32)
        if kf == 0:
            acc_ref[...] = part
        else:
            acc_ref[...] += part
    for seg in range(S5_SEGS):
        mod = mod_ref[seg]
        o_ref[seg] = x1_scr[rows(seg), :] + mod[5:6, :] * _rms(acc_ref[rows(seg), :], gain_ref[3:4, :])


def _post_mixer(mixer, x8, mix, mod8, gains, w_proj, w1, w2, skip=None, perm_t=None, tmc=64):
    _, ls, _ = x8.shape
    tok = lambda j: (0, j, 0)
    const2 = lambda j: (0, 0)
    m = S5_SEGS * tmc
    in_specs = [pl.BlockSpec((S5_SEGS, tmc, D_MODEL), tok)]
    scratch = [pltpu.VMEM((m, D_MODEL), _f32), pltpu.VMEM((m, D_MODEL), _bf16), pltpu.VMEM((m, D_MODEL), _f32)]
    if mixer == "attn":
        in_specs.append(pl.BlockSpec((S5_SEGS, tmc, D_MODEL), tok))
    else:
        in_specs.append(pl.BlockSpec((S5_PAIRS, tmc // S5_CHUNK * S5_SEGS, PAIR_W), tok))
        scratch += [pltpu.VMEM((LANE_TILES, m, 128), _f32), pltpu.VMEM((m, D_MODEL), _bf16)]
    in_specs += [_resident((S5_SEGS, 6, D_MODEL), lambda j: (0, 0, 0)), _resident((4, D_MODEL), const2)]
    args = [x8, mix, mod8, gains]
    if mixer == "s5":
        in_specs += [_resident((1, D_MODEL), const2), _resident((D_MODEL, D_MODEL), const2)]
        args += [skip, perm_t]
    in_specs += [_resident(w_proj.shape, const2), _resident(w1.shape, const2), _resident(w2.shape, const2)]
    args += [w_proj, w1, w2]
    return pl.pallas_call(
        functools.partial(_post_kernel, mixer),
        grid=(ls // tmc,),
        in_specs=in_specs,
        out_specs=pl.BlockSpec((S5_SEGS, tmc, D_MODEL), tok),
        out_shape=jax.ShapeDtypeStruct(x8.shape, _f32),
        scratch_shapes=scratch,
        compiler_params=_cparams(("parallel",)),
        name="post_" + mixer,
    )(*args)


def _trunk(x, mod, params, chained):
    b, l, _ = x.shape
    assert (b == 1) if chained else (b == S5_SEGS)
    ls = b * l // S5_SEGS
    gains = params["norm_gain"]
    mod8 = jnp.broadcast_to(mod, (DEPTH, S5_SEGS, 6, D_MODEL))
    qkv = _qkv_proj(x, mod[0], gains[0], params["w_qkv"])
    att = _attention(qkv, params["attn_bias"])
    x8 = _post_mixer("attn", x.reshape(S5_SEGS, ls, D_MODEL), att.reshape(S5_SEGS, ls, D_MODEL), mod8[0], gains[0],
                     params["w_o"], params["ffn_w1"][0], params["ffn_w2"][0])
    u = _s5_input(x8, mod8[1], gains[1], _lane_perm())
    y = _s5_core(u, params["s5_tables"], chained)
    x8 = _post_mixer("s5", x8, y, mod8[1], gains[1], params["w_glu"], params["ffn_w1"][1], params["ffn_w2"][1],
                     skip=params["s5_d"], perm_t=_lane_perm(inverse=True))
    return x8.reshape(b, l, D_MODEL)


def kernel(x_prompt, x_sample, c_prompt, c_sample, ada_w, ada_b, norm_gain, attn_w_qkv, attn_w_o, attn_rpb,
           s5_a_re, s5_a_im, s5_log_dt, s5_b_re, s5_b_im, s5_c_re, s5_c_im, s5_d, s5_w_glu, ffn_w1, ffn_w2):
    nbp, nbs = c_prompt.shape[0], c_sample.shape[0]
    nb = -(-(nbp + nbs) // 8) * 8
    c_all = jnp.concatenate([c_prompt, c_sample, jnp.zeros((nb - nbp - nbs, D_MODEL), _f32)], axis=0)
    mod = _modulation(c_all, ada_w, ada_b)
    mod_p = mod[:, :nbp].reshape(DEPTH, nbp, 6, D_MODEL)
    mod_s = mod[:, nbp:nbp + nbs].reshape(DEPTH, nbs, 6, D_MODEL)

    params = {
        "norm_gain": norm_gain,
        "w_qkv": attn_w_qkv[0].astype(_bf16),
        "w_o": attn_w_o[0].astype(_bf16),
        "attn_bias": _attn_bias_table(attn_rpb[0]),
        "s5_tables": _s5_tables(s5_a_re[0], s5_a_im[0], s5_log_dt[0], s5_b_re[0], s5_b_im[0],
                                s5_c_re[0], s5_c_im[0]),
        "s5_d": s5_d[0].reshape(1, D_MODEL),
        "w_glu": s5_w_glu[0].astype(_bf16),
        "ffn_w1": ffn_w1.astype(_bf16),
        "ffn_w2": ffn_w2.astype(_bf16),
    }
    y_prompt = _trunk(x_prompt, mod_p, params, chained=False)
    y_sample = _trunk(x_sample, mod_s, params, chained=True)
    return (y_prompt, y_sample)
```

```python
import functools
import math

import numpy as np
import jax
import jax.numpy as jnp
from jax import lax
from jax.experimental import pallas as pl
from jax.experimental.pallas import tpu as pltpu

D_MODEL = 1024
D_FF = 4 * D_MODEL
DEPTH = 2
EPS = 1e-6
NEG_INF = -1e30
LOG2E = math.log2(math.e)

GRID_W = 64
N_HEADS = 16
HEAD_DIM = D_MODEL // N_HEADS
WIN_H = 8
WIN_W = 16
ROWS_PER_TILE = 4
TILE_TOK = ROWS_PER_TILE * GRID_W
KEY_TILES = 3
S5_GROUP = 16
S5_GROUPS = D_MODEL // S5_GROUP
S5_STATE = 64
S5_CHUNK = 16
S5_SEGS = 8
S5_PAIRS = S5_GROUPS // 2
PAIR_W = 2 * S5_CHUNK * S5_GROUP
STATE_W = 2 * S5_STATE
NPOW = 24

VMEM_LIMIT = 56 * 1024 * 1024

_f32 = jnp.float32
_bf16 = jnp.bfloat16


def _cparams(sem):
    return pltpu.CompilerParams(dimension_semantics=sem, vmem_limit_bytes=VMEM_LIMIT)


def _resident(shape, index_map):
    return pl.BlockSpec(shape, index_map, pipeline_mode=pl.Buffered(1))


def _rms(x, gain):
    ms = jnp.mean(x * x, axis=-1, keepdims=True)
    return x * lax.rsqrt(ms + EPS) * gain


def _norm_mod(x, gain, scale, shift):
    return _rms(x, gain) * (1.0 + scale) + shift


def _mod_kernel(c_ref, w_ref, b_ref, o_ref):
    c = c_ref[...]
    act = c * jax.nn.sigmoid(c)
    o_ref[0] = jnp.dot(act, w_ref[0], preferred_element_type=_f32,
                       precision=lax.Precision.HIGHEST) + b_ref[0]


def _modulation(c_all, ada_w, ada_b):
    nb = c_all.shape[0]
    tn = 1536
    return pl.pallas_call(
        _mod_kernel,
        grid=(DEPTH, 6 * D_MODEL // tn),
        in_specs=[
            pl.BlockSpec((nb, D_MODEL), lambda i, n: (0, 0)),
            pl.BlockSpec((1, D_MODEL, tn), lambda i, n: (i, 0, n)),
            pl.BlockSpec((1, 1, tn), lambda i, n: (i, 0, n)),
        ],
        out_specs=pl.BlockSpec((1, nb, tn), lambda i, n: (i, 0, n)),
        out_shape=jax.ShapeDtypeStruct((DEPTH, nb, 6 * D_MODEL), _f32),
        compiler_params=_cparams(("arbitrary", "arbitrary")),
        name="adaln_mod",
    )(c_all, ada_w, ada_b.reshape(DEPTH, 1, 6 * D_MODEL))


def _qkv_kernel(x_ref, mod_ref, gain_ref, w_ref, o_ref):
    mod = mod_ref[0]
    h = _norm_mod(x_ref[0], gain_ref[0:1, :], mod[1:2, :], mod[0:1, :]).astype(_bf16)
    q_scale = HEAD_DIM ** -0.5 * LOG2E
    for part in range(3):
        cols = slice(part * D_MODEL, (part + 1) * D_MODEL)
        y = jnp.dot(h, w_ref[:, cols], preferred_element_type=_f32)
        if part == 0:
            y = y * q_scale
        o_ref[0, :, cols] = y.astype(_bf16)


def _qkv_proj(x, mod, gains, w_qkv, tm=1024):
    b, l, _ = x.shape
    return pl.pallas_call(
        _qkv_kernel,
        grid=(b, l // tm),
        in_specs=[
            pl.BlockSpec((1, tm, D_MODEL), lambda i, j: (i, j, 0)),
            pl.BlockSpec((1, 6, D_MODEL), lambda i, j: (i, 0, 0)),
            _resident((4, D_MODEL), lambda i, j: (0, 0)),
            _resident((D_MODEL, 3 * D_MODEL), lambda i, j: (0, 0)),
        ],
        out_specs=pl.BlockSpec((1, tm, 3 * D_MODEL), lambda i, j: (i, j, 0)),
        out_shape=jax.ShapeDtypeStruct((b, l, 3 * D_MODEL), _bf16),
        compiler_params=_cparams(("parallel", "parallel")),
        name="norm_qkv",
    )(x, mod, gains, w_qkv)


LANE_TILES = D_MODEL // 128


def _lane_perm(inverse=False):
    eye = np.eye
    p = np.einsum('sS,gG,iI->sgiGSI', eye(8), eye(8), eye(S5_GROUP)).reshape(D_MODEL, D_MODEL)
    return jnp.asarray(p.T if inverse else p, _bf16)


def _chunk_col(g, h):
    return (g % 2) * (PAIR_W // 2) + h * 128


def _chunk_stage(u, seg, ubuf):
    tmc = u.shape[0]
    for lt in range(LANE_TILES):
        ubuf[lt, seg * tmc:(seg + 1) * tmc, :] = u[:, lt * 128:(lt + 1) * 128]


def _chunk_emit(ubuf, p_ref, o_ref):
    tmc = ubuf.shape[1] // S5_SEGS
    nck = tmc // S5_CHUNK
    blk = nck * S5_SEGS
    a = jnp.concatenate(
        [jnp.concatenate([ubuf[lt, pl.ds(c * S5_CHUNK + 8 * h + s, S5_SEGS, stride=tmc), :]
                          for lt in range(LANE_TILES) for h in range(2) for c in range(nck)], axis=0)
         for s in range(8)], axis=1).astype(_bf16)
    for gp in range(4):
        out = jnp.dot(a, p_ref[:, gp * 256:(gp + 1) * 256], preferred_element_type=_f32)
        for lt in range(LANE_TILES):
            for h in range(2):
                for gl in (2 * gp, 2 * gp + 1):
                    g = 8 * lt + gl
                    col = _chunk_col(g, h)
                    piece = out[(lt * 2 + h) * blk:(lt * 2 + h + 1) * blk, (gl % 2) * 128:(gl % 2 + 1) * 128]
                    o_ref[g // 2, :, col:col + 128] = piece.astype(o_ref.dtype)


def _attn_bias_table(rpb):
    n_dr = 2 * WIN_H - 1
    i = np.arange(ROWS_PER_TILE)[:, None]
    kk = np.arange(KEY_TILES * ROWS_PER_TILE)[None, :]
    dr = np.stack([kk - i + 7, kk - i + 3, kk - i - 1])
    row_ok = np.stack([(kk < WIN_H) & (i >= 0), (kk >= i) & (kk < i + WIN_H), (kk >= 4) & (i >= 0)])
    dr = np.where(row_ok, dr, n_dr)
    qc = np.arange(GRID_W)[:, None]
    kc = np.arange(GRID_W)[None, :]
    col_start = np.clip(qc - WIN_W // 2, 0, GRID_W - WIN_W)
    col_ok = (kc >= col_start) & (kc < col_start + WIN_W)
    pad = GRID_W - WIN_W
    rp = jnp.pad(rpb.astype(_f32) * LOG2E, ((0, 0), (0, 0), (pad, pad)))
    t = jnp.stack([rp[:, :, GRID_W - 1 - q:2 * GRID_W - 1 - q] for q in range(GRID_W)], axis=2)
    t = jnp.where(col_ok[None, None], t, NEG_INF)
    t = jnp.concatenate([t, jnp.full((N_HEADS, 1, GRID_W, GRID_W), NEG_INF, _f32)], axis=1).astype(_bf16)
    even = jnp.take(t, dr[:, :, 0::2].reshape(-1), axis=1)
    odd = jnp.take(t, dr[:, :, 1::2].reshape(-1), axis=1)
    b = jnp.concatenate([even, odd], axis=-1)
    b = b.reshape(N_HEADS, 3, ROWS_PER_TILE, KEY_TILES * 2, GRID_W, 2 * GRID_W)
    return jnp.transpose(b, (1, 0, 2, 3, 4, 5))


def _attn_heads(interior, q_ref, k_refs, v_refs, bias_ref, o_ref):
    lane = lax.broadcasted_iota(jnp.int32, (1, 128), 1)
    low = lane < HEAD_DIM
    head_mask = [low.astype(_bf16), (~low).astype(_bf16)]
    head_of_lane = jnp.right_shift(lax.broadcasted_iota(jnp.int32, (1, 256), 1), HEAD_DIM.bit_length() - 1)
    n_pairs = 2 * KEY_TILES
    zero_blk = jnp.zeros((GRID_W, 2 * GRID_W), _bf16)
    for hq in range(N_HEADS // 4):
        vcols = slice(hq * 256, (hq + 1) * 256)
        acc = None
        for h4 in range(4):
            head = 4 * hq + h4
            cols = slice((head // 2) * 128, (head // 2 + 1) * 128)
            qm = q_ref[0, :, cols] * head_mask[head % 2]
            s = [lax.dot_general(qm, k_refs[t][0, :, cols], (((1,), (1,)), ((), ())),
                                 preferred_element_type=_f32) for t in range(KEY_TILES)]
            p_rows, inv_l = [], []
            for i in range(ROWS_PER_TILE):
                pairs = range(i // 2, (i + WIN_H - 1) // 2 + 1) if interior else range(n_pairs)
                rows = slice(i * GRID_W, (i + 1) * GRID_W)
                blk = {pr: s[pr // 2][rows, (pr % 2) * 128:(pr % 2 + 1) * 128]
                       + bias_ref[0, head, i, pr].astype(_f32) for pr in pairs}
                m = None
                for pr in pairs:
                    m = blk[pr] if m is None else jnp.maximum(m, blk[pr])
                m = jnp.max(m, axis=-1, keepdims=True)
                e = {pr: jnp.exp2(blk[pr] - m) for pr in pairs}
                tot = None
                for pr in pairs:
                    tot = e[pr] if tot is None else tot + e[pr]
                inv_l.append(1.0 / jnp.sum(tot, axis=-1, keepdims=True))
                p_rows.append([e[pr].astype(_bf16) if pr in e else zero_blk for pr in range(n_pairs)])
            o = None
            for t in range(KEY_TILES):
                p_t = jnp.concatenate(
                    [jnp.concatenate([p_rows[i][2 * t], p_rows[i][2 * t + 1]], axis=1)
                     for i in range(ROWS_PER_TILE)], axis=0)
                part = jnp.dot(p_t, v_refs[t][0, :, vcols], preferred_element_type=_f32)
                o = part if o is None else o + part
            o = o * jnp.concatenate(inv_l, axis=0)
            acc = o if acc is None else jnp.where(head_of_lane == h4, o, acc)
        o_ref[0, :, vcols] = acc.astype(o_ref.dtype)


def _attn_kernel(q_ref, k0_ref, k1_ref, k2_ref, v0_ref, v1_ref, v2_ref, bias_ref, o_ref):
    j = pl.program_id(1)
    interior = jnp.logical_and(j > 0, j < pl.num_programs(1) - 1)
    args = (q_ref, (k0_ref, k1_ref, k2_ref), (v0_ref, v1_ref, v2_ref), bias_ref, o_ref)
    pl.when(interior)(functools.partial(_attn_heads, True, *args))
    pl.when(jnp.logical_not(interior))(functools.partial(_attn_heads, False, *args))


def _attention(qkv, bias):
    b, l, _ = qkv.shape
    nt = l // TILE_TOK
    assert nt >= KEY_TILES

    def base(j):
        return jnp.clip(j - 1, 0, nt - KEY_TILES)

    def pattern(j):
        return jnp.where(j == 0, 0, jnp.where(j == nt - 1, 2, 1))

    def kv_spec(part, t):
        return pl.BlockSpec((1, TILE_TOK, D_MODEL), lambda i, j: (i, base(j) + t, part))

    return pl.pallas_call(
        _attn_kernel,
        grid=(b, nt),
        in_specs=[pl.BlockSpec((1, TILE_TOK, D_MODEL), lambda i, j: (i, j, 0))]
        + [kv_spec(1, t) for t in range(KEY_TILES)]
        + [kv_spec(2, t) for t in range(KEY_TILES)]
        + [pl.BlockSpec((1, N_HEADS, ROWS_PER_TILE, KEY_TILES * 2, GRID_W, 2 * GRID_W),
                        lambda i, j: (pattern(j), 0, 0, 0, 0, 0))],
        out_specs=pl.BlockSpec((1, TILE_TOK, D_MODEL), lambda i, j: (i, j, 0)),
        out_shape=jax.ShapeDtypeStruct((b, l, D_MODEL), _bf16),
        compiler_params=_cparams(("parallel", "arbitrary")),
        name="nbr_attention",
    )(qkv, qkv, qkv, qkv, qkv, qkv, qkv, bias)


def _s5_tables(a_re, a_im, log_dt, b_re, b_im, c_re, c_im):
    f = lambda v: v.astype(_f32)
    pairs = (2, S5_PAIRS)
    a_row = [f(a).reshape(*pairs, 1, STATE_W) for a in (a_re, a_im)]
    a_col = [f(a).reshape(*pairs, STATE_W, 1) for a in (a_re, a_im)]
    ldt = jnp.repeat(f(log_dt), S5_STATE, axis=-1)
    ldt_row = ldt.reshape(*pairs, 1, STATE_W)
    ldt_col = ldt.reshape(*pairs, STATE_W, 1)
    b_t = [jnp.transpose(f(b).reshape(*pairs, 2, S5_STATE, S5_GROUP), (0, 1, 4, 2, 3))
           .reshape(*pairs, S5_GROUP, STATE_W) for b in (b_re, b_im)]
    c_t = [jnp.transpose(f(c), (0, 1, 3, 2)).reshape(*pairs, STATE_W, S5_GROUP) for c in (c_re, c_im)]

    def spec(shape):
        nd = len(shape)
        return pl.BlockSpec((2, 1) + tuple(shape[2:]), lambda p: (0, p) + (0,) * (nd - 2))

    ins = a_row + a_col + [ldt_row, ldt_col] + b_t + c_t
    out_shapes = (
        jax.ShapeDtypeStruct((S5_GROUPS, PAIR_W // 2, PAIR_W // 2), _bf16),
        jax.ShapeDtypeStruct((2, S5_PAIRS, PAIR_W, 2 * STATE_W), _bf16),
        jax.ShapeDtypeStruct((2, S5_PAIRS, 2 * STATE_W, PAIR_W), _bf16),
        jax.ShapeDtypeStruct((2, S5_PAIRS, S5_SEGS, 2 * STATE_W), _f32),
    )
    return pl.pallas_call(
        _s5_tables_kernel,
        grid=(S5_PAIRS,),
        in_specs=[spec(v.shape) for v in ins],
        out_specs=(pl.BlockSpec((2, PAIR_W // 2, PAIR_W // 2), lambda p: (p, 0, 0)),
                   spec(out_shapes[1].shape), spec(out_shapes[2].shape), spec(out_shapes[3].shape)),
        out_shape=out_shapes,
        compiler_params=_cparams(("parallel",)),
        name="s5_tables",
    )(*ins)


def _cexp(k, a_re, a_im, dt):
    mag = jnp.exp(k * (a_re * dt))
    ang = k * (a_im * dt)
    return mag * jnp.cos(ang), mag * jnp.sin(ang)


def _s5_tables_kernel(ar_re_ref, ar_im_ref, ac_re_ref, ac_im_ref, ldr_ref, ldc_ref,
                      bt_re_ref, bt_im_ref, ct_re_ref, ct_im_ref, m_ref, w_ref, v_ref, lam_ref):
    hi = lax.Precision.HIGHEST
    t = S5_CHUNK
    cw = t * S5_GROUP
    lane_c = lax.broadcasted_iota(jnp.int32, (1, cw), 1)
    log_group = S5_GROUP.bit_length() - 1
    lag = jnp.right_shift(lane_c, log_group).astype(_f32)
    tok_row = jnp.right_shift(lax.broadcasted_iota(jnp.int32, (cw, 1), 0), log_group).astype(_f32)
    low = lax.broadcasted_iota(jnp.int32, (1, STATE_W), 1) < S5_STATE
    tile_i = (lax.broadcasted_iota(jnp.int32, (S5_GROUP, cw), 0)
              == jnp.bitwise_and(lax.broadcasted_iota(jnp.int32, (S5_GROUP, cw), 1), S5_GROUP - 1)).astype(_f32)
    k_sub = jnp.minimum(lax.broadcasted_iota(jnp.int32, (NPOW, 1), 0), t).astype(_f32)
    k_lane = jnp.minimum(lax.broadcasted_iota(jnp.int32, (1, STATE_W), 1), t).astype(_f32)
    pow_of_row = lax.broadcasted_iota(jnp.int32, (cw, NPOW), 1).astype(_f32)
    pow_of_lane = lax.broadcasted_iota(jnp.int32, (STATE_W, cw), 0).astype(_f32)

    def spread(rows, sel):
        return jnp.dot(rows, sel, preferred_element_type=_f32, precision=hi)

    g_lag = [[None, None], [None, None]]
    for d in range(2):
        a_re, a_im = ar_re_ref[d, 0], ar_im_ref[d, 0]
        pw_re, pw_im = _cexp(k_sub, a_re, a_im, jnp.exp(ldr_ref[d, 0]))
        lam_re, lam_im = pw_re[1:2], pw_im[1:2]
        den = a_re * a_re + a_im * a_im
        nr, ni = lam_re - 1.0, lam_im
        zr = (nr * a_re + ni * a_im) / den
        zi = (ni * a_re - nr * a_im) / den
        bb_re = zr * bt_re_ref[d, 0] - zi * bt_im_ref[d, 0]
        bb_im = zr * bt_im_ref[d, 0] + zi * bt_re_ref[d, 0]
        sel_tok = (pow_of_row == ((t - 1.0) - tok_row if d == 0 else tok_row)).astype(_f32)
        p_re, p_im = spread(sel_tok, pw_re), spread(sel_tok, pw_im)
        bt_re = jnp.concatenate([bb_re] * t, axis=0)
        bt_im = jnp.concatenate([bb_im] * t, axis=0)
        w_re, w_im = _cmul(p_re, p_im, bt_re, bt_im)
        top = jnp.concatenate([jnp.where(low, w_re, 0.0), jnp.where(low, w_im, 0.0)], axis=1)
        bot = jnp.concatenate([jnp.where(low, 0.0, w_re), jnp.where(low, 0.0, w_im)], axis=1)
        w_ref[d, 0] = jnp.concatenate([top, bot], axis=0).astype(w_ref.dtype)
        lam_ref[d, 0] = jnp.broadcast_to(jnp.concatenate([pw_re[t:t + 1], pw_im[t:t + 1]], axis=1),
                                         (S5_SEGS, 2 * STATE_W))
        pc_re, pc_im = _cexp(k_lane, ac_re_ref[d, 0], ac_im_ref[d, 0], jnp.exp(ldc_ref[d, 0]))
        ct_re = spread(ct_re_ref[d, 0], tile_i)
        ct_im = spread(ct_im_ref[d, 0], tile_i)

        def c_pow(k_of_block):
            sel = (pow_of_lane == k_of_block).astype(_f32)
            return _cmul(ct_re, ct_im, spread(pc_re, sel), spread(pc_im, sel))

        vb_re, vb_im = c_pow(lag + 1.0 if d == 0 else float(t) - lag)
        zero = jnp.zeros((S5_STATE, cw), _f32)
        v_ref[d, 0] = jnp.concatenate([
            jnp.concatenate([vb_re[:S5_STATE], zero], axis=1), jnp.concatenate([zero, vb_re[S5_STATE:]], axis=1),
            jnp.concatenate([-vb_im[:S5_STATE], zero], axis=1), jnp.concatenate([zero, -vb_im[S5_STATE:]], axis=1),
        ], axis=0).astype(v_ref.dtype)
        ce_re, ce_im = c_pow(lag if d == 0 else (t - 1.0) - lag)
        for g2 in range(2):
            sel = low if g2 == 0 else ~low
            g_lag[d][g2] = (
                jnp.dot(jnp.where(sel, bb_re, 0.0), ce_re, preferred_element_type=_f32, precision=hi)
                - jnp.dot(jnp.where(sel, bb_im, 0.0), ce_im, preferred_element_type=_f32, precision=hi))
    for g2 in range(2):
        rows = []
        for s in range(t):
            fwd = g_lag[0][g2] if s == 0 else pltpu.roll(g_lag[0][g2], S5_GROUP * s, 1)
            fwd = jnp.where(lane_c >= S5_GROUP * s, fwd, 0.0)
            shift = (cw - S5_GROUP * (t - 1 - s)) % cw
            bwd = g_lag[1][g2] if shift == 0 else pltpu.roll(g_lag[1][g2], shift, 1)
            bwd = jnp.where(lane_c < S5_GROUP * (s + 1), bwd, 0.0)
            rows.append(fwd + bwd)
        m_ref[g2] = jnp.concatenate(rows, axis=0).astype(m_ref.dtype)


def _cmul(ar, ai, br, bi):
    return ar * br - ai * bi, ar * bi + ai * br


def _s5_kernel(chained, u_ref, m_ref, w_ref, v_ref, lam_ref, y_ref, x_scr, h_scr):
    rows = u_ref.shape[1]
    nc = rows // S5_SEGS
    rb = min(512, rows)
    sw = STATE_W

    def inc_body(r, carry):
        rs = pl.ds(pl.multiple_of(r * rb, rb), rb)
        u = u_ref[0, rs, :]
        for d in range(2):
            x_scr[d, rs, :] = jnp.dot(u, w_ref[d, 0], preferred_element_type=_f32)
        return carry

    lax.fori_loop(0, rows // rb, inc_body, 0)

    lam = [(lam_ref[d, 0, :, :sw], lam_ref[d, 0, :, sw:]) for d in range(2)]

    def row_of(d, c):
        cc = c if d == 0 else nc - 1 - c
        return pl.ds(pl.multiple_of(cc * S5_SEGS, S5_SEGS), S5_SEGS)

    def scan_body(c, carry):
        new = []
        for d in range(2):
            hr, hi = carry[d]
            rs = row_of(d, c)
            h_scr[d, rs, :sw] = hr
            h_scr[d, rs, sw:] = hi
            pr, pi = _cmul(lam[d][0], lam[d][1], hr, hi)
            new.append((pr + x_scr[d, rs, :sw], pi + x_scr[d, rs, sw:]))
        return tuple(new)

    zero = jnp.zeros((S5_SEGS, sw), _f32)
    ends = lax.fori_loop(0, nc, scan_body, ((zero, zero), (zero, zero)), unroll=4)

    if chained:
        one = jnp.ones((S5_SEGS, sw), _f32)
        h_in = []
        for d in range(2):
            er, ei = ends[d]
            dr, di = one[0:1], zero[0:1]
            br, bi = lam[d][0][0:1], lam[d][1][0:1]
            n = nc
            while n:
                if n & 1:
                    dr, di = _cmul(dr, di, br, bi)
                n >>= 1
                if n:
                    br, bi = _cmul(br, bi, br, bi)
            order = range(S5_SEGS) if d == 0 else range(S5_SEGS - 1, -1, -1)
            cr = jnp.zeros((1, sw), _f32)
            ci = jnp.zeros((1, sw), _f32)
            rows_r = [None] * S5_SEGS
            rows_i = [None] * S5_SEGS
            for sgm in order:
                rows_r[sgm], rows_i[sgm] = cr, ci
                pr, pi = _cmul(dr, di, cr, ci)
                cr, ci = pr + er[sgm:sgm + 1], pi + ei[sgm:sgm + 1]
            h_in.append((jnp.concatenate(rows_r, axis=0), jnp.concatenate(rows_i, axis=0)))

        def fix_body(c, carry):
            new = []
            for d in range(2):
                pr, pi = carry[d]
                rs = row_of(d, c)
                ar, ai = _cmul(pr, pi, h_in[d][0], h_in[d][1])
                h_scr[d, rs, :sw] = h_scr[d, rs, :sw] + ar
                h_scr[d, rs, sw:] = h_scr[d, rs, sw:] + ai
                new.append(_cmul(lam[d][0], lam[d][1], pr, pi))
            return tuple(new)

        lax.fori_loop(0, nc, fix_body, ((one, zero), (one, zero)), unroll=4)

    half = PAIR_W // 2

    def out_body(r, carry):
        rs = pl.ds(pl.multiple_of(r * rb, rb), rb)
        u = u_ref[0, rs, :]
        y = (jnp.dot(h_scr[0, rs, :].astype(_bf16), v_ref[0, 0], preferred_element_type=_f32)
             + jnp.dot(h_scr[1, rs, :].astype(_bf16), v_ref[1, 0], preferred_element_type=_f32))
        y0 = y[:, :half] + jnp.dot(u[:, :half], m_ref[0], preferred_element_type=_f32)
        y1 = y[:, half:] + jnp.dot(u[:, half:], m_ref[1], preferred_element_type=_f32)
        y_ref[0, rs, :half] = y0.astype(y_ref.dtype)
        y_ref[0, rs, half:] = y1.astype(y_ref.dtype)
        return carry

    lax.fori_loop(0, rows // rb, out_body, 0)


def _s5_core(u_t, tables, chained):
    m_sum, w_in, v_out, lam_t = tables
    _, rows, _ = u_t.shape
    return pl.pallas_call(
        functools.partial(_s5_kernel, chained),
        grid=(S5_PAIRS,),
        in_specs=[
            pl.BlockSpec((1, rows, PAIR_W), lambda p: (p, 0, 0)),
            pl.BlockSpec((2, PAIR_W // 2, PAIR_W // 2), lambda p: (p, 0, 0)),
            pl.BlockSpec((2, 1, PAIR_W, 2 * STATE_W), lambda p: (0, p, 0, 0)),
            pl.BlockSpec((2, 1, 2 * STATE_W, PAIR_W), lambda p: (0, p, 0, 0)),
            pl.BlockSpec((2, 1, S5_SEGS, 2 * STATE_W), lambda p: (0, p, 0, 0)),
        ],
        out_specs=pl.BlockSpec((1, rows, PAIR_W), lambda p: (p, 0, 0)),
        out_shape=jax.ShapeDtypeStruct((S5_PAIRS, rows, PAIR_W), _bf16),
        scratch_shapes=[pltpu.VMEM((2, rows, 2 * STATE_W), _f32),
                        pltpu.VMEM((2, rows, 2 * STATE_W), _f32)],
        compiler_params=_cparams(("parallel",)),
        name="s5_core",
    )(u_t, m_sum, w_in, v_out, lam_t)


def _gelu_tanh(x):
    return 0.5 * x * (1.0 + jnp.tanh(math.sqrt(2.0 / math.pi) * (x + 0.044715 * (x * x * x))))


def _post_kernel(mixer, *refs):
    if mixer == "attn":
        (x_ref, mix_ref, mod_ref, gain_ref, modn_ref, gainn_ref, perm_ref, wp_ref, w1_ref, w2_ref, o_ref, u_ref,
         x1_scr, h_scr, acc_ref, ubuf) = refs
    else:
        (x_ref, mix_ref, mod_ref, gain_ref, skip_ref, pt_ref, wp_ref, w1_ref, w2_ref, o_ref,
         x1_scr, h_scr, acc_ref, nat_scr) = refs
    tmc = x_ref.shape[1]

    def rows(seg):
        return slice(seg * tmc, (seg + 1) * tmc)

    if mixer == "attn":
        for seg in range(S5_SEGS):
            h_scr[rows(seg), :] = mix_ref[seg]
        y = jnp.dot(h_scr[...], wp_ref[...], preferred_element_type=_f32)
    else:
        nck = tmc // S5_CHUNK
        blk = nck * S5_SEGS
        a = jnp.concatenate(
            [jnp.concatenate([mix_ref[(8 * lt + gl) // 2, :, _chunk_col(8 * lt + gl, h):_chunk_col(8 * lt + gl, h) + 128]
                              for gl in range(8)], axis=1)
             for lt in range(LANE_TILES) for h in range(2)], axis=0)
        for sp in range(4):
            out = jnp.dot(a, pt_ref[:, sp * 256:(sp + 1) * 256], preferred_element_type=_f32)
            for lt in range(LANE_TILES):
                for h in range(2):
                    for s in (2 * sp, 2 * sp + 1):
                        for c in range(nck):
                            tok = c * S5_CHUNK + 8 * h + s
                            row = (lt * 2 + h) * blk + c * S5_SEGS
                            nat_scr[lt, tok * S5_SEGS:(tok + 1) * S5_SEGS, :] = (
                                out[row:row + S5_SEGS, (s % 2) * 128:(s % 2 + 1) * 128])
        for seg in range(S5_SEGS):
            mod = mod_ref[seg]
            u = _norm_mod(x_ref[seg], gain_ref[0:1, :], mod[1:2, :], mod[0:1, :])
            y_ssm = jnp.concatenate([nat_scr[lt, pl.ds(seg, tmc, stride=S5_SEGS), :]
                                     for lt in range(D_MODEL // 128)], axis=1)
            h_scr[rows(seg), :] = _gelu_tanh(skip_ref[...] * u + y_ssm).astype(_bf16)
        g = h_scr[...]
        val = jnp.dot(g, wp_ref[:, :D_MODEL], preferred_element_type=_f32)
        gate = jnp.dot(g, wp_ref[:, D_MODEL:], preferred_element_type=_f32)
        y = val * jax.nn.sigmoid(gate)
    for seg in range(S5_SEGS):
        mod = mod_ref[seg]
        x1 = x_ref[seg] + mod[2:3, :] * _rms(y[rows(seg), :], gain_ref[1:2, :])
        x1_scr[rows(seg), :] = x1
        h_scr[rows(seg), :] = _norm_mod(x1, gain_ref[2:3, :], mod[4:5, :], mod[3:4, :]).astype(_bf16)
    h2 = h_scr[...]
    tf = 512
    for kf in range(D_FF // tf):
        cols = slice(kf * tf, (kf + 1) * tf)
        a = jnp.dot(h2, w1_ref[:, cols], preferred_element_type=_f32)
        a = jnp.square(jnp.maximum(a, 0.0)).astype(_bf16)
        part = jnp.dot(a, w2_ref[cols, :], preferred_element_type=_f32)
        if kf == 0:
            acc_ref[...] = part
        else:
            acc_ref[...] += part
    for seg in range(S5_SEGS):
        mod = mod_ref[seg]
        out = x1_scr[rows(seg), :] + mod[5:6, :] * _rms(acc_ref[rows(seg), :], gain_ref[3:4, :])
        o_ref[seg] = out
        if mixer == "attn":
            modn = modn_ref[seg]
            _chunk_stage(_norm_mod(out, gainn_ref[0:1, :], modn[1:2, :], modn[0:1, :]), seg, ubuf)
    if mixer == "attn":
        _chunk_emit(ubuf, perm_ref, u_ref)


def _post_mixer(mixer, x8, mix, mod8, gains, w_proj, w1, w2, skip=None, perm=None, next_mod8=None, next_gains=None,
                tmc=64):
    _, ls, _ = x8.shape
    tok = lambda j: (0, j, 0)
    const2 = lambda j: (0, 0)
    m = S5_SEGS * tmc
    chunk_rows = tmc // S5_CHUNK * S5_SEGS
    mod_spec = _resident((S5_SEGS, 6, D_MODEL), lambda j: (0, 0, 0))
    in_specs = [pl.BlockSpec((S5_SEGS, tmc, D_MODEL), tok)]
    scratch = [pltpu.VMEM((m, D_MODEL), _f32), pltpu.VMEM((m, D_MODEL), _bf16), pltpu.VMEM((m, D_MODEL), _f32)]
    out_specs = pl.BlockSpec((S5_SEGS, tmc, D_MODEL), tok)
    out_shape = jax.ShapeDtypeStruct(x8.shape, _f32)
    if mixer == "attn":
        in_specs.append(pl.BlockSpec((S5_SEGS, tmc, D_MODEL), tok))
        scratch.append(pltpu.VMEM((LANE_TILES, m, 128), _f32))
        out_specs = (out_specs, pl.BlockSpec((S5_PAIRS, chunk_rows, PAIR_W), tok))
        out_shape = (out_shape, jax.ShapeDtypeStruct((S5_PAIRS, ls // S5_CHUNK * S5_SEGS, PAIR_W), _bf16))
    else:
        in_specs.append(pl.BlockSpec((S5_PAIRS, chunk_rows, PAIR_W), tok))
        scratch.append(pltpu.VMEM((LANE_TILES, m, 128), _f32))
    in_specs += [mod_spec, _resident((4, D_MODEL), const2)]
    args = [x8, mix, mod8, gains]
    if mixer == "attn":
        in_specs += [mod_spec, _resident((4, D_MODEL), const2), _resident((D_MODEL, D_MODEL), const2)]
        args += [next_mod8, next_gains, perm]
    else:
        in_specs += [_resident((1, D_MODEL), const2), _resident((D_MODEL, D_MODEL), const2)]
        args += [skip, perm]
    in_specs += [_resident(w_proj.shape, const2), _resident(w1.shape, const2), _resident(w2.shape, const2)]
    args += [w_proj, w1, w2]
    return pl.pallas_call(
        functools.partial(_post_kernel, mixer),
        grid=(ls // tmc,),
        in_specs=in_specs,
        out_specs=out_specs,
        out_shape=out_shape,
        scratch_shapes=scratch,
        compiler_params=_cparams(("parallel",)),
        name="post_" + mixer,
    )(*args)


def _trunk(x, mod, params, chained):
    b, l, _ = x.shape
    assert (b == 1) if chained else (b == S5_SEGS)
    ls = b * l // S5_SEGS
    gains = params["norm_gain"]
    mod8 = jnp.broadcast_to(mod, (DEPTH, S5_SEGS, 6, D_MODEL))
    qkv = _qkv_proj(x, mod[0], gains[0], params["w_qkv"])
    att = _attention(qkv, params["attn_bias"])
    x8, u = _post_mixer("attn", x.reshape(S5_SEGS, ls, D_MODEL), att.reshape(S5_SEGS, ls, D_MODEL), mod8[0],
                        gains[0], params["w_o"], params["ffn_w1"][0], params["ffn_w2"][0],
                        perm=_lane_perm(), next_mod8=mod8[1], next_gains=gains[1])
    y = _s5_core(u, params["s5_tables"], chained)
    x8 = _post_mixer("s5", x8, y, mod8[1], gains[1], params["w_glu"], params["ffn_w1"][1], params["ffn_w2"][1],
                     skip=params["s5_d"], perm=_lane_perm(inverse=True))
    return x8.reshape(b, l, D_MODEL)


def kernel(x_prompt, x_sample, c_prompt, c_sample, ada_w, ada_b, norm_gain, attn_w_qkv, attn_w_o, attn_rpb,
           s5_a_re, s5_a_im, s5_log_dt, s5_b_re, s5_b_im, s5_c_re, s5_c_im, s5_d, s5_w_glu, ffn_w1, ffn_w2):
    nbp, nbs = c_prompt.shape[0], c_sample.shape[0]
    nb = -(-(nbp + nbs) // 8) * 8
    c_all = jnp.concatenate([c_prompt, c_sample, jnp.zeros((nb - nbp - nbs, D_MODEL), _f32)], axis=0)
    mod = _modulation(c_all, ada_w, ada_b)
    mod_p = mod[:, :nbp].reshape(DEPTH, nbp, 6, D_MODEL)
    mod_s = mod[:, nbp:nbp + nbs].reshape(DEPTH, nbs, 6, D_MODEL)

    params = {
        "norm_gain": norm_gain,
        "w_qkv": attn_w_qkv[0].astype(_bf16),
        "w_o": attn_w_o[0].astype(_bf16),
        "attn_bias": _attn_bias_table(attn_rpb[0]),
        "s5_tables": _s5_tables(s5_a_re[0], s5_a_im[0], s5_log_dt[0], s5_b_re[0], s5_b_im[0],
                                s5_c_re[0], s5_c_im[0]),
        "s5_d": s5_d[0].reshape(1, D_MODEL),
        "w_glu": s5_w_glu[0].astype(_bf16),
        "ffn_w1": ffn_w1.astype(_bf16),
        "ffn_w2": ffn_w2.astype(_bf16),
    }
    y_prompt = _trunk(x_prompt, mod_p, params, chained=False)
    y_sample = _trunk(x_sample, mod_s, params, chained=True)
    return (y_prompt, y_sample)
```

```python
import functools
import math

import numpy as np
import jax
import jax.numpy as jnp
from jax import lax
from jax.experimental import pallas as pl
from jax.experimental.pallas import tpu as pltpu

D_MODEL = 1024
D_FF = 4 * D_MODEL
DEPTH = 2
EPS = 1e-6
NEG_INF = -1e30
LOG2E = math.log2(math.e)

GRID_W = 64
N_HEADS = 16
HEAD_DIM = D_MODEL // N_HEADS
WIN_H = 8
WIN_W = 16
ROWS_PER_TILE = 4
TILE_TOK = ROWS_PER_TILE * GRID_W
KEY_TILES = 3
S5_GROUP = 16
S5_GROUPS = D_MODEL // S5_GROUP
S5_STATE = 64
S5_CHUNK = 16
S5_SEGS = 8
S5_PAIRS = S5_GROUPS // 2
PAIR_W = 2 * S5_CHUNK * S5_GROUP
STATE_W = 2 * S5_STATE
NPOW = 32

VMEM_LIMIT = 56 * 1024 * 1024

_f32 = jnp.float32
_bf16 = jnp.bfloat16


def _cparams(sem):
    return pltpu.CompilerParams(dimension_semantics=sem, vmem_limit_bytes=VMEM_LIMIT)


def _resident(shape, index_map):
    return pl.BlockSpec(shape, index_map, pipeline_mode=pl.Buffered(1))


def _rms(x, gain):
    ms = jnp.mean(x * x, axis=-1, keepdims=True)
    return x * lax.rsqrt(ms + EPS) * gain


def _norm_mod(x, gain, scale, shift):
    return _rms(x, gain) * (1.0 + scale) + shift


def _mod_kernel(c_ref, w_ref, b_ref, o_ref):
    c = c_ref[...]
    act = c * jax.nn.sigmoid(c)
    o_ref[0] = jnp.dot(act, w_ref[0], preferred_element_type=_f32,
                       precision=lax.Precision.HIGHEST) + b_ref[0]


def _modulation(c_all, ada_w, ada_b):
    nb = c_all.shape[0]
    tn = 1536
    return pl.pallas_call(
        _mod_kernel,
        grid=(DEPTH, 6 * D_MODEL // tn),
        in_specs=[
            pl.BlockSpec((nb, D_MODEL), lambda i, n: (0, 0)),
            pl.BlockSpec((1, D_MODEL, tn), lambda i, n: (i, 0, n)),
            pl.BlockSpec((1, 1, tn), lambda i, n: (i, 0, n)),
        ],
        out_specs=pl.BlockSpec((1, nb, tn), lambda i, n: (i, 0, n)),
        out_shape=jax.ShapeDtypeStruct((DEPTH, nb, 6 * D_MODEL), _f32),
        compiler_params=_cparams(("arbitrary", "arbitrary")),
        name="adaln_mod",
    )(c_all, ada_w, ada_b.reshape(DEPTH, 1, 6 * D_MODEL))


def _qkv_kernel(x_ref, mod_ref, gain_ref, w_ref, o_ref):
    mod = mod_ref[0]
    h = _norm_mod(x_ref[0], gain_ref[0:1, :], mod[1:2, :], mod[0:1, :]).astype(_bf16)
    q_scale = HEAD_DIM ** -0.5 * LOG2E
    for part in range(3):
        cols = slice(part * D_MODEL, (part + 1) * D_MODEL)
        y = jnp.dot(h, w_ref[:, cols], preferred_element_type=_f32)
        if part == 0:
            y = y * q_scale
        o_ref[0, :, cols] = y.astype(_bf16)


def _qkv_proj(x, mod, gains, w_qkv, tm=1024):
    b, l, _ = x.shape
    return pl.pallas_call(
        _qkv_kernel,
        grid=(b, l // tm),
        in_specs=[
            pl.BlockSpec((1, tm, D_MODEL), lambda i, j: (i, j, 0)),
            pl.BlockSpec((1, 6, D_MODEL), lambda i, j: (i, 0, 0)),
            _resident((4, D_MODEL), lambda i, j: (0, 0)),
            _resident((D_MODEL, 3 * D_MODEL), lambda i, j: (0, 0)),
        ],
        out_specs=pl.BlockSpec((1, tm, 3 * D_MODEL), lambda i, j: (i, j, 0)),
        out_shape=jax.ShapeDtypeStruct((b, l, 3 * D_MODEL), _bf16),
        compiler_params=_cparams(("parallel", "parallel")),
        name="norm_qkv",
    )(x, mod, gains, w_qkv)


LANE_TILES = D_MODEL // 128


def _lane_perm(inverse=False):
    eye = np.eye
    p = np.einsum('sS,gG,iI->sgiGSI', eye(8), eye(8), eye(S5_GROUP)).reshape(D_MODEL, D_MODEL)
    return jnp.asarray(p.T if inverse else p, _bf16)


def _chunk_col(g, h):
    return (g % 2) * (PAIR_W // 2) + h * 128


def _chunk_stage(u, seg, ubuf):
    tmc = u.shape[0]
    for lt in range(LANE_TILES):
        ubuf[lt, seg * tmc:(seg + 1) * tmc, :] = u[:, lt * 128:(lt + 1) * 128]


def _chunk_emit(ubuf, p_ref, o_ref):
    tmc = ubuf.shape[1] // S5_SEGS
    nck = tmc // S5_CHUNK
    blk = nck * S5_SEGS
    a = jnp.concatenate(
        [jnp.concatenate([ubuf[lt, pl.ds(c * S5_CHUNK + 8 * h + s, S5_SEGS, stride=tmc), :]
                          for lt in range(LANE_TILES) for h in range(2) for c in range(nck)], axis=0)
         for s in range(8)], axis=1).astype(_bf16)
    for gp in range(4):
        out = jnp.dot(a, p_ref[:, gp * 256:(gp + 1) * 256], preferred_element_type=_f32)
        for lt in range(LANE_TILES):
            for h in range(2):
                for gl in (2 * gp, 2 * gp + 1):
                    g = 8 * lt + gl
                    col = _chunk_col(g, h)
                    piece = out[(lt * 2 + h) * blk:(lt * 2 + h + 1) * blk, (gl % 2) * 128:(gl % 2 + 1) * 128]
                    o_ref[g // 2, :, col:col + 128] = piece.astype(o_ref.dtype)


N_REL_ROWS = 2 * WIN_H - 1


def _bias_rows():
    i = np.arange(ROWS_PER_TILE)[:, None]
    kk = np.arange(KEY_TILES * ROWS_PER_TILE)[None, :]
    dr = np.stack([kk - i + 7, kk - i + 3, kk - i - 1])
    row_ok = np.stack([(kk < WIN_H) & (i >= 0), (kk >= i) & (kk < i + WIN_H), (kk >= 4) & (i >= 0)])
    return np.where(row_ok, dr, N_REL_ROWS)


def _bias_kernel(rp_ref, o_ref):
    dr_tab = _bias_rows()
    pair_w = 2 * GRID_W
    qc = lax.broadcasted_iota(jnp.int32, (GRID_W, pair_w), 0)
    lane = lax.broadcasted_iota(jnp.int32, (GRID_W, pair_w), 1)
    kc = jnp.bitwise_and(lane, GRID_W - 1)
    col_start = jnp.clip(qc - WIN_W // 2, 0, GRID_W - WIN_W)
    col_ok = jnp.logical_and(kc >= col_start, kc < col_start + WIN_W)
    left_half = lane < GRID_W
    shift = pair_w - (WIN_W - 1)
    left, right = [], []
    for dr in range(N_REL_ROWS):
        base = jnp.broadcast_to(rp_ref[0, dr:dr + 1, :], (GRID_W, pair_w))
        left.append(pltpu.roll(base, shift, 1, stride=1, stride_axis=0))
        right.append(pltpu.roll(base, (shift + GRID_W) % pair_w, 1, stride=1, stride_axis=0))
    masked = jnp.full((GRID_W, pair_w), NEG_INF, _f32)
    left.append(masked)
    right.append(masked)
    for pt in range(dr_tab.shape[0]):
        for i in range(ROWS_PER_TILE):
            for pr in range(KEY_TILES * 2):
                blk = jnp.where(left_half, left[dr_tab[pt, i, 2 * pr]], right[dr_tab[pt, i, 2 * pr + 1]])
                o_ref[pt, 0, i, pr] = jnp.where(col_ok, blk, NEG_INF).astype(o_ref.dtype)


def _attn_bias_table(rpb):
    rel_rows_pad = -(-N_REL_ROWS // 8) * 8
    rp = jnp.pad(rpb.astype(_f32) * LOG2E,
                 ((0, 0), (0, rel_rows_pad - N_REL_ROWS), (0, 2 * GRID_W - (2 * WIN_W - 1))))
    shape = (3, N_HEADS, ROWS_PER_TILE, KEY_TILES * 2, GRID_W, 2 * GRID_W)
    return pl.pallas_call(
        _bias_kernel,
        grid=(N_HEADS,),
        in_specs=[pl.BlockSpec((1, rel_rows_pad, 2 * GRID_W), lambda h: (h, 0, 0))],
        out_specs=pl.BlockSpec((3, 1) + shape[2:], lambda h: (0, h, 0, 0, 0, 0)),
        out_shape=jax.ShapeDtypeStruct(shape, _bf16),
        compiler_params=_cparams(("parallel",)),
        name="attn_bias",
    )(rp)


def _attn_heads(interior, q_ref, k_refs, v_refs, bias_ref, o_ref):
    lane = lax.broadcasted_iota(jnp.int32, (1, 128), 1)
    low = lane < HEAD_DIM
    head_mask = [low.astype(_bf16), (~low).astype(_bf16)]
    head_of_lane = jnp.right_shift(lax.broadcasted_iota(jnp.int32, (1, 256), 1), HEAD_DIM.bit_length() - 1)
    n_pairs = 2 * KEY_TILES
    zero_blk = jnp.zeros((GRID_W, 2 * GRID_W), _bf16)
    for hq in range(N_HEADS // 4):
        vcols = slice(hq * 256, (hq + 1) * 256)
        acc = None
        for h4 in range(4):
            head = 4 * hq + h4
            cols = slice((head // 2) * 128, (head // 2 + 1) * 128)
            qm = q_ref[0, :, cols] * head_mask[head % 2]
            s = [lax.dot_general(qm, k_refs[t][0, :, cols], (((1,), (1,)), ((), ())),
                                 preferred_element_type=_f32) for t in range(KEY_TILES)]
            p_rows, inv_l = [], []
            for i in range(ROWS_PER_TILE):
                pairs = range(i // 2, (i + WIN_H - 1) // 2 + 1) if interior else range(n_pairs)
                rows = slice(i * GRID_W, (i + 1) * GRID_W)
                blk = {pr: s[pr // 2][rows, (pr % 2) * 128:(pr % 2 + 1) * 128]
                       + bias_ref[0, head, i, pr].astype(_f32) for pr in pairs}
                m = None
                for pr in pairs:
                    m = blk[pr] if m is None else jnp.maximum(m, blk[pr])
                m = jnp.max(m, axis=-1, keepdims=True)
                e = {pr: jnp.exp2(blk[pr] - m) for pr in pairs}
                tot = None
                for pr in pairs:
                    tot = e[pr] if tot is None else tot + e[pr]
                inv_l.append(1.0 / jnp.sum(tot, axis=-1, keepdims=True))
                p_rows.append([e[pr].astype(_bf16) if pr in e else zero_blk for pr in range(n_pairs)])
            o = None
            for t in range(KEY_TILES):
                p_t = jnp.concatenate(
                    [jnp.concatenate([p_rows[i][2 * t], p_rows[i][2 * t + 1]], axis=1)
                     for i in range(ROWS_PER_TILE)], axis=0)
                part = jnp.dot(p_t, v_refs[t][0, :, vcols], preferred_element_type=_f32)
                o = part if o is None else o + part
            o = o * jnp.concatenate(inv_l, axis=0)
            acc = o if acc is None else jnp.where(head_of_lane == h4, o, acc)
        o_ref[0, :, vcols] = acc.astype(o_ref.dtype)


def _attn_kernel(q_ref, k0_ref, k1_ref, k2_ref, v0_ref, v1_ref, v2_ref, bias_ref, o_ref):
    j = pl.program_id(1)
    interior = jnp.logical_and(j > 0, j < pl.num_programs(1) - 1)
    args = (q_ref, (k0_ref, k1_ref, k2_ref), (v0_ref, v1_ref, v2_ref), bias_ref, o_ref)
    pl.when(interior)(functools.partial(_attn_heads, True, *args))
    pl.when(jnp.logical_not(interior))(functools.partial(_attn_heads, False, *args))


def _attention(qkv, bias):
    b, l, _ = qkv.shape
    nt = l // TILE_TOK
    assert nt >= KEY_TILES

    def base(j):
        return jnp.clip(j - 1, 0, nt - KEY_TILES)

    def pattern(j):
        return jnp.where(j == 0, 0, jnp.where(j == nt - 1, 2, 1))

    def kv_spec(part, t):
        return pl.BlockSpec((1, TILE_TOK, D_MODEL), lambda i, j: (i, base(j) + t, part))

    return pl.pallas_call(
        _attn_kernel,
        grid=(b, nt),
        in_specs=[pl.BlockSpec((1, TILE_TOK, D_MODEL), lambda i, j: (i, j, 0))]
        + [kv_spec(1, t) for t in range(KEY_TILES)]
        + [kv_spec(2, t) for t in range(KEY_TILES)]
        + [pl.BlockSpec((1, N_HEADS, ROWS_PER_TILE, KEY_TILES * 2, GRID_W, 2 * GRID_W),
                        lambda i, j: (pattern(j), 0, 0, 0, 0, 0))],
        out_specs=pl.BlockSpec((1, TILE_TOK, D_MODEL), lambda i, j: (i, j, 0)),
        out_shape=jax.ShapeDtypeStruct((b, l, D_MODEL), _bf16),
        compiler_params=_cparams(("parallel", "arbitrary")),
        name="nbr_attention",
    )(qkv, qkv, qkv, qkv, qkv, qkv, qkv, bias)


def _s5_tables(a_re, a_im, log_dt, b_re, b_im, c_re, c_im):
    f = lambda v: v.astype(_f32)
    pairs = (2, S5_PAIRS)
    a_row = [f(a).reshape(*pairs, 1, STATE_W) for a in (a_re, a_im)]
    a_col = [f(a).reshape(*pairs, STATE_W, 1) for a in (a_re, a_im)]
    ldt = jnp.repeat(f(log_dt), S5_STATE, axis=-1)
    ldt_row = ldt.reshape(*pairs, 1, STATE_W)
    ldt_col = ldt.reshape(*pairs, STATE_W, 1)
    b_t = [jnp.transpose(f(b).reshape(*pairs, 2, S5_STATE, S5_GROUP), (0, 1, 4, 2, 3))
           .reshape(*pairs, S5_GROUP, STATE_W) for b in (b_re, b_im)]
    c_t = [jnp.transpose(f(c), (0, 1, 3, 2)).reshape(*pairs, STATE_W, S5_GROUP) for c in (c_re, c_im)]

    def spec(shape):
        nd = len(shape)
        return pl.BlockSpec((2, 1) + tuple(shape[2:]), lambda p: (0, p) + (0,) * (nd - 2))

    ins = a_row + a_col + [ldt_row, ldt_col] + b_t + c_t
    out_shapes = (
        jax.ShapeDtypeStruct((S5_GROUPS, PAIR_W // 2, PAIR_W // 2), _bf16),
        jax.ShapeDtypeStruct((2, S5_PAIRS, PAIR_W, 2 * STATE_W), _bf16),
        jax.ShapeDtypeStruct((2, S5_PAIRS, 2 * STATE_W, PAIR_W), _bf16),
        jax.ShapeDtypeStruct((2, S5_PAIRS, S5_SEGS, 2 * STATE_W), _f32),
    )
    return pl.pallas_call(
        _s5_tables_kernel,
        grid=(S5_PAIRS,),
        in_specs=[spec(v.shape) for v in ins],
        out_specs=(pl.BlockSpec((2, PAIR_W // 2, PAIR_W // 2), lambda p: (p, 0, 0)),
                   spec(out_shapes[1].shape), spec(out_shapes[2].shape), spec(out_shapes[3].shape)),
        out_shape=out_shapes,
        compiler_params=_cparams(("parallel",)),
        name="s5_tables",
    )(*ins)


def _cexp(k, a_re, a_im, dt):
    mag = jnp.exp(k * (a_re * dt))
    ang = k * (a_im * dt)
    return mag * jnp.cos(ang), mag * jnp.sin(ang)


def _s5_tables_kernel(ar_re_ref, ar_im_ref, ac_re_ref, ac_im_ref, ldr_ref, ldc_ref,
                      bt_re_ref, bt_im_ref, ct_re_ref, ct_im_ref, m_ref, w_ref, v_ref, lam_ref):
    hi = lax.Precision.HIGHEST
    t = S5_CHUNK
    cw = t * S5_GROUP
    lane_c = lax.broadcasted_iota(jnp.int32, (1, cw), 1)
    log_group = S5_GROUP.bit_length() - 1
    lag = jnp.right_shift(lane_c, log_group).astype(_f32)
    tok_row = jnp.right_shift(lax.broadcasted_iota(jnp.int32, (cw, 1), 0), log_group).astype(_f32)
    low = lax.broadcasted_iota(jnp.int32, (1, STATE_W), 1) < S5_STATE
    tile_i = (lax.broadcasted_iota(jnp.int32, (S5_GROUP, cw), 0)
              == jnp.bitwise_and(lax.broadcasted_iota(jnp.int32, (S5_GROUP, cw), 1), S5_GROUP - 1)).astype(_f32)
    k_sub = jnp.minimum(lax.broadcasted_iota(jnp.int32, (NPOW, 1), 0), t).astype(_f32)
    k_lane = jnp.minimum(lax.broadcasted_iota(jnp.int32, (1, STATE_W), 1), t).astype(_f32)
    pow_of_row = lax.broadcasted_iota(jnp.int32, (cw, NPOW), 1).astype(_f32)
    pow_of_lane = lax.broadcasted_iota(jnp.int32, (STATE_W, cw), 0).astype(_f32)

    def spread(a, b, data_left=True):
        data, sel = (a, b) if data_left else (b, a)
        top = data.astype(_bf16)
        rest = data - top.astype(_f32)
        mid = rest.astype(_bf16)
        low = (rest - mid.astype(_f32)).astype(_bf16)
        sel = sel.astype(_bf16)
        parts = [jnp.dot(p, sel, preferred_element_type=_f32) if data_left
                 else jnp.dot(sel, p, preferred_element_type=_f32) for p in (top, mid, low)]
        return (parts[0] + parts[1]) + parts[2]

    g_lag = [[None, None], [None, None]]
    for d in range(2):
        a_re, a_im = ar_re_ref[d, 0], ar_im_ref[d, 0]
        pw_re, pw_im = _cexp(k_sub, a_re, a_im, jnp.exp(ldr_ref[d, 0]))
        lam_re, lam_im = pw_re[1:2], pw_im[1:2]
        den = a_re * a_re + a_im * a_im
        nr, ni = lam_re - 1.0, lam_im
        zr = (nr * a_re + ni * a_im) / den
        zi = (ni * a_re - nr * a_im) / den
        bb_re = zr * bt_re_ref[d, 0] - zi * bt_im_ref[d, 0]
        bb_im = zr * bt_im_ref[d, 0] + zi * bt_re_ref[d, 0]
        sel_tok = (pow_of_row == ((t - 1.0) - tok_row if d == 0 else tok_row)).astype(_f32)
        p_re = spread(sel_tok, pw_re, data_left=False)
        p_im = spread(sel_tok, pw_im, data_left=False)
        bt_re = jnp.concatenate([bb_re] * t, axis=0)
        bt_im = jnp.concatenate([bb_im] * t, axis=0)
        w_re, w_im = _cmul(p_re, p_im, bt_re, bt_im)
        top = jnp.concatenate([jnp.where(low, w_re, 0.0), jnp.where(low, w_im, 0.0)], axis=1)
        bot = jnp.concatenate([jnp.where(low, 0.0, w_re), jnp.where(low, 0.0, w_im)], axis=1)
        w_ref[d, 0] = jnp.concatenate([top, bot], axis=0).astype(w_ref.dtype)
        lam_ref[d, 0] = jnp.broadcast_to(jnp.concatenate([pw_re[t:t + 1], pw_im[t:t + 1]], axis=1),
                                         (S5_SEGS, 2 * STATE_W))
        pc_re, pc_im = _cexp(k_lane, ac_re_ref[d, 0], ac_im_ref[d, 0], jnp.exp(ldc_ref[d, 0]))
        ct_re = spread(ct_re_ref[d, 0], tile_i)
        ct_im = spread(ct_im_ref[d, 0], tile_i)

        def c_pow(k_of_block):
            sel = (pow_of_lane == k_of_block).astype(_f32)
            return _cmul(ct_re, ct_im, spread(pc_re, sel), spread(pc_im, sel))

        vb_re, vb_im = c_pow(lag + 1.0 if d == 0 else float(t) - lag)
        zero = jnp.zeros((S5_STATE, cw), _f32)
        v_ref[d, 0] = jnp.concatenate([
            jnp.concatenate([vb_re[:S5_STATE], zero], axis=1), jnp.concatenate([zero, vb_re[S5_STATE:]], axis=1),
            jnp.concatenate([-vb_im[:S5_STATE], zero], axis=1), jnp.concatenate([zero, -vb_im[S5_STATE:]], axis=1),
        ], axis=0).astype(v_ref.dtype)
        ce_re, ce_im = c_pow(lag if d == 0 else (t - 1.0) - lag)
        for g2 in range(2):
            sel = low if g2 == 0 else ~low
            g_lag[d][g2] = (
                jnp.dot(jnp.where(sel, bb_re, 0.0), ce_re, preferred_element_type=_f32, precision=hi)
                - jnp.dot(jnp.where(sel, bb_im, 0.0), ce_im, preferred_element_type=_f32, precision=hi))
    for g2 in range(2):
        rows = []
        for s in range(t):
            fwd = g_lag[0][g2] if s == 0 else pltpu.roll(g_lag[0][g2], S5_GROUP * s, 1)
            fwd = jnp.where(lane_c >= S5_GROUP * s, fwd, 0.0)
            shift = (cw - S5_GROUP * (t - 1 - s)) % cw
            bwd = g_lag[1][g2] if shift == 0 else pltpu.roll(g_lag[1][g2], shift, 1)
            bwd = jnp.where(lane_c < S5_GROUP * (s + 1), bwd, 0.0)
            rows.append(fwd + bwd)
        m_ref[g2] = jnp.concatenate(rows, axis=0).astype(m_ref.dtype)


def _cmul(ar, ai, br, bi):
    return ar * br - ai * bi, ar * bi + ai * br


def _s5_kernel(chained, u_ref, m_ref, w_ref, v_ref, lam_ref, y_ref, x_scr, h_scr):
    rows = u_ref.shape[1]
    nc = rows // S5_SEGS
    rb = min(512, rows)
    sw = STATE_W

    def inc_body(r, carry):
        rs = pl.ds(pl.multiple_of(r * rb, rb), rb)
        u = u_ref[0, rs, :]
        for d in range(2):
            x_scr[d, rs, :] = jnp.dot(u, w_ref[d, 0], preferred_element_type=_f32)
        return carry

    lax.fori_loop(0, rows // rb, inc_body, 0)

    lam = [(lam_ref[d, 0, :, :sw], lam_ref[d, 0, :, sw:]) for d in range(2)]

    def row_of(d, c):
        cc = c if d == 0 else nc - 1 - c
        return pl.ds(pl.multiple_of(cc * S5_SEGS, S5_SEGS), S5_SEGS)

    def scan_body(c, carry):
        new = []
        for d in range(2):
            hr, hi = carry[d]
            rs = row_of(d, c)
            h_scr[d, rs, :sw] = hr
            h_scr[d, rs, sw:] = hi
            pr, pi = _cmul(lam[d][0], lam[d][1], hr, hi)
            new.append((pr + x_scr[d, rs, :sw], pi + x_scr[d, rs, sw:]))
        return tuple(new)

    zero = jnp.zeros((S5_SEGS, sw), _f32)
    ends = lax.fori_loop(0, nc, scan_body, ((zero, zero), (zero, zero)), unroll=4)

    if chained:
        one = jnp.ones((S5_SEGS, sw), _f32)
        h_in = []
        for d in range(2):
            er, ei = ends[d]
            dr, di = one[0:1], zero[0:1]
            br, bi = lam[d][0][0:1], lam[d][1][0:1]
            n = nc
            while n:
                if n & 1:
                    dr, di = _cmul(dr, di, br, bi)
                n >>= 1
                if n:
                    br, bi = _cmul(br, bi, br, bi)
            order = range(S5_SEGS) if d == 0 else range(S5_SEGS - 1, -1, -1)
            cr = jnp.zeros((1, sw), _f32)
            ci = jnp.zeros((1, sw), _f32)
            rows_r = [None] * S5_SEGS
            rows_i = [None] * S5_SEGS
            for sgm in order:
                rows_r[sgm], rows_i[sgm] = cr, ci
                pr, pi = _cmul(dr, di, cr, ci)
                cr, ci = pr + er[sgm:sgm + 1], pi + ei[sgm:sgm + 1]
            h_in.append((jnp.concatenate(rows_r, axis=0), jnp.concatenate(rows_i, axis=0)))

        def fix_body(c, carry):
            new = []
            for d in range(2):
                pr, pi = carry[d]
                rs = row_of(d, c)
                ar, ai = _cmul(pr, pi, h_in[d][0], h_in[d][1])
                h_scr[d, rs, :sw] = h_scr[d, rs, :sw] + ar
                h_scr[d, rs, sw:] = h_scr[d, rs, sw:] + ai
                new.append(_cmul(lam[d][0], lam[d][1], pr, pi))
            return tuple(new)

        lax.fori_loop(0, nc, fix_body, ((one, zero), (one, zero)), unroll=4)

    half = PAIR_W // 2

    def out_body(r, carry):
        rs = pl.ds(pl.multiple_of(r * rb, rb), rb)
        u = u_ref[0, rs, :]
        y = (jnp.dot(h_scr[0, rs, :].astype(_bf16), v_ref[0, 0], preferred_element_type=_f32)
             + jnp.dot(h_scr[1, rs, :].astype(_bf16), v_ref[1, 0], preferred_element_type=_f32))
        y0 = y[:, :half] + jnp.dot(u[:, :half], m_ref[0], preferred_element_type=_f32)
        y1 = y[:, half:] + jnp.dot(u[:, half:], m_ref[1], preferred_element_type=_f32)
        y_ref[0, rs, :half] = y0.astype(y_ref.dtype)
        y_ref[0, rs, half:] = y1.astype(y_ref.dtype)
        return carry

    lax.fori_loop(0, rows // rb, out_body, 0)


def _s5_core(u_t, tables, chained):
    m_sum, w_in, v_out, lam_t = tables
    _, rows, _ = u_t.shape
    return pl.pallas_call(
        functools.partial(_s5_kernel, chained),
        grid=(S5_PAIRS,),
        in_specs=[
            pl.BlockSpec((1, rows, PAIR_W), lambda p: (p, 0, 0)),
            pl.BlockSpec((2, PAIR_W // 2, PAIR_W // 2), lambda p: (p, 0, 0)),
            pl.BlockSpec((2, 1, PAIR_W, 2 * STATE_W), lambda p: (0, p, 0, 0)),
            pl.BlockSpec((2, 1, 2 * STATE_W, PAIR_W), lambda p: (0, p, 0, 0)),
            pl.BlockSpec((2, 1, S5_SEGS, 2 * STATE_W), lambda p: (0, p, 0, 0)),
        ],
        out_specs=pl.BlockSpec((1, rows, PAIR_W), lambda p: (p, 0, 0)),
        out_shape=jax.ShapeDtypeStruct((S5_PAIRS, rows, PAIR_W), _bf16),
        scratch_shapes=[pltpu.VMEM((2, rows, 2 * STATE_W), _f32),
                        pltpu.VMEM((2, rows, 2 * STATE_W), _f32)],
        compiler_params=_cparams(("parallel",)),
        name="s5_core",
    )(u_t, m_sum, w_in, v_out, lam_t)


def _gelu_tanh(x):
    return 0.5 * x * (1.0 + jnp.tanh(math.sqrt(2.0 / math.pi) * (x + 0.044715 * (x * x * x))))


def _post_kernel(mixer, *refs):
    if mixer == "attn":
        (x_ref, mix_ref, mod_ref, gain_ref, modn_ref, gainn_ref, perm_ref, wp_ref, w1_ref, w2_ref, o_ref, u_ref,
         x1_scr, h_scr, acc_ref, ubuf) = refs
    else:
        (x_ref, mix_ref, mod_ref, gain_ref, skip_ref, pt_ref, wp_ref, w1_ref, w2_ref, o_ref,
         x1_scr, h_scr, acc_ref, nat_scr) = refs
    tmc = x_ref.shape[1]

    def rows(seg):
        return slice(seg * tmc, (seg + 1) * tmc)

    if mixer == "attn":
        for seg in range(S5_SEGS):
            h_scr[rows(seg), :] = mix_ref[seg]
        y = jnp.dot(h_scr[...], wp_ref[...], preferred_element_type=_f32)
    else:
        nck = tmc // S5_CHUNK
        blk = nck * S5_SEGS
        a = jnp.concatenate(
            [jnp.concatenate([mix_ref[(8 * lt + gl) // 2, :, _chunk_col(8 * lt + gl, h):_chunk_col(8 * lt + gl, h) + 128]
                              for gl in range(8)], axis=1)
             for lt in range(LANE_TILES) for h in range(2)], axis=0)
        for sp in range(4):
            out = jnp.dot(a, pt_ref[:, sp * 256:(sp + 1) * 256], preferred_element_type=_f32)
            for lt in range(LANE_TILES):
                for h in range(2):
                    for s in (2 * sp, 2 * sp + 1):
                        for c in range(nck):
                            tok = c * S5_CHUNK + 8 * h + s
                            row = (lt * 2 + h) * blk + c * S5_SEGS
                            nat_scr[lt, tok * S5_SEGS:(tok + 1) * S5_SEGS, :] = (
                                out[row:row + S5_SEGS, (s % 2) * 128:(s % 2 + 1) * 128])
        for seg in range(S5_SEGS):
            mod = mod_ref[seg]
            u = _norm_mod(x_ref[seg], gain_ref[0:1, :], mod[1:2, :], mod[0:1, :])
            y_ssm = jnp.concatenate([nat_scr[lt, pl.ds(seg, tmc, stride=S5_SEGS), :]
                                     for lt in range(D_MODEL // 128)], axis=1)
            h_scr[rows(seg), :] = _gelu_tanh(skip_ref[...] * u + y_ssm).astype(_bf16)
        g = h_scr[...]
        val = jnp.dot(g, wp_ref[:, :D_MODEL], preferred_element_type=_f32)
        gate = jnp.dot(g, wp_ref[:, D_MODEL:], preferred_element_type=_f32)
        y = val * jax.nn.sigmoid(gate)
    for seg in range(S5_SEGS):
        mod = mod_ref[seg]
        x1 = x_ref[seg] + mod[2:3, :] * _rms(y[rows(seg), :], gain_ref[1:2, :])
        x1_scr[rows(seg), :] = x1
        h_scr[rows(seg), :] = _norm_mod(x1, gain_ref[2:3, :], mod[4:5, :], mod[3:4, :]).astype(_bf16)
    h2 = h_scr[...]
    tf = 512
    for kf in range(D_FF // tf):
        cols = slice(kf * tf, (kf + 1) * tf)
        a = jnp.dot(h2, w1_ref[:, cols], preferred_element_type=_f32)
        a = jnp.square(jnp.maximum(a, 0.0)).astype(_bf16)
        part = jnp.dot(a, w2_ref[cols, :], preferred_element_type=_f32)
        if kf == 0:
            acc_ref[...] = part
        else:
            acc_ref[...] += part
    for seg in range(S5_SEGS):
        mod = mod_ref[seg]
        out = x1_scr[rows(seg), :] + mod[5:6, :] * _rms(acc_ref[rows(seg), :], gain_ref[3:4, :])
        o_ref[seg] = out
        if mixer == "attn":
            modn = modn_ref[seg]
            _chunk_stage(_norm_mod(out, gainn_ref[0:1, :], modn[1:2, :], modn[0:1, :]), seg, ubuf)
    if mixer == "attn":
        _chunk_emit(ubuf, perm_ref, u_ref)


def _post_mixer(mixer, x8, mix, mod8, gains, w_proj, w1, w2, skip=None, perm=None, next_mod8=None, next_gains=None,
                tmc=64):
    _, ls, _ = x8.shape
    tok = lambda j: (0, j, 0)
    const2 = lambda j: (0, 0)
    m = S5_SEGS * tmc
    chunk_rows = tmc // S5_CHUNK * S5_SEGS
    mod_spec = _resident((S5_SEGS, 6, D_MODEL), lambda j: (0, 0, 0))
    in_specs = [pl.BlockSpec((S5_SEGS, tmc, D_MODEL), tok)]
    scratch = [pltpu.VMEM((m, D_MODEL), _f32), pltpu.VMEM((m, D_MODEL), _bf16), pltpu.VMEM((m, D_MODEL), _f32)]
    out_specs = pl.BlockSpec((S5_SEGS, tmc, D_MODEL), tok)
    out_shape = jax.ShapeDtypeStruct(x8.shape, _f32)
    if mixer == "attn":
        in_specs.append(pl.BlockSpec((S5_SEGS, tmc, D_MODEL), tok))
        scratch.append(pltpu.VMEM((LANE_TILES, m, 128), _f32))
        out_specs = (out_specs, pl.BlockSpec((S5_PAIRS, chunk_rows, PAIR_W), tok))
        out_shape = (out_shape, jax.ShapeDtypeStruct((S5_PAIRS, ls // S5_CHUNK * S5_SEGS, PAIR_W), _bf16))
    else:
        in_specs.append(pl.BlockSpec((S5_PAIRS, chunk_rows, PAIR_W), tok))
        scratch.append(pltpu.VMEM((LANE_TILES, m, 128), _f32))
    in_specs += [mod_spec, _resident((4, D_MODEL), const2)]
    args = [x8, mix, mod8, gains]
    if mixer == "attn":
        in_specs += [mod_spec, _resident((4, D_MODEL), const2), _resident((D_MODEL, D_MODEL), const2)]
        args += [next_mod8, next_gains, perm]
    else:
        in_specs += [_resident((1, D_MODEL), const2), _resident((D_MODEL, D_MODEL), const2)]
        args += [skip, perm]
    in_specs += [_resident(w_proj.shape, const2), _resident(w1.shape, const2), _resident(w2.shape, const2)]
    args += [w_proj, w1, w2]
    return pl.pallas_call(
        functools.partial(_post_kernel, mixer),
        grid=(ls // tmc,),
        in_specs=in_specs,
        out_specs=out_specs,
        out_shape=out_shape,
        scratch_shapes=scratch,
        compiler_params=_cparams(("parallel",)),
        name="post_" + mixer,
    )(*args)


def _trunk(x, mod, params, chained):
    b, l, _ = x.shape
    assert (b == 1) if chained else (b == S5_SEGS)
    ls = b * l // S5_SEGS
    gains = params["norm_gain"]
    mod8 = jnp.broadcast_to(mod, (DEPTH, S5_SEGS, 6, D_MODEL))
    qkv = _qkv_proj(x, mod[0], gains[0], params["w_qkv"])
    att = _attention(qkv, params["attn_bias"])
    x8, u = _post_mixer("attn", x.reshape(S5_SEGS, ls, D_MODEL), att.reshape(S5_SEGS, ls, D_MODEL), mod8[0],
                        gains[0], params["w_o"], params["ffn_w1"][0], params["ffn_w2"][0],
                        perm=_lane_perm(), next_mod8=mod8[1], next_gains=gains[1])
    y = _s5_core(u, params["s5_tables"], chained)
    x8 = _post_mixer("s5", x8, y, mod8[1], gains[1], params["w_glu"], params["ffn_w1"][1], params["ffn_w2"][1],
                     skip=params["s5_d"], perm=_lane_perm(inverse=True))
    return x8.reshape(b, l, D_MODEL)


def kernel(x_prompt, x_sample, c_prompt, c_sample, ada_w, ada_b, norm_gain, attn_w_qkv, attn_w_o, attn_rpb,
           s5_a_re, s5_a_im, s5_log_dt, s5_b_re, s5_b_im, s5_c_re, s5_c_im, s5_d, s5_w_glu, ffn_w1, ffn_w2):
    nbp, nbs = c_prompt.shape[0], c_sample.shape[0]
    nb = -(-(nbp + nbs) // 8) * 8
    c_all = jnp.concatenate([c_prompt, c_sample, jnp.zeros((nb - nbp - nbs, D_MODEL), _f32)], axis=0)
    mod = _modulation(c_all, ada_w, ada_b)
    mod_p = mod[:, :nbp].reshape(DEPTH, nbp, 6, D_MODEL)
    mod_s = mod[:, nbp:nbp + nbs].reshape(DEPTH, nbs, 6, D_MODEL)

    params = {
        "norm_gain": norm_gain,
        "w_qkv": attn_w_qkv[0].astype(_bf16),
        "w_o": attn_w_o[0].astype(_bf16),
        "attn_bias": _attn_bias_table(attn_rpb[0]),
        "s5_tables": _s5_tables(s5_a_re[0], s5_a_im[0], s5_log_dt[0], s5_b_re[0], s5_b_im[0],
                                s5_c_re[0], s5_c_im[0]),
        "s5_d": s5_d[0].reshape(1, D_MODEL),
        "w_glu": s5_w_glu[0].astype(_bf16),
        "ffn_w1": ffn_w1.astype(_bf16),
        "ffn_w2": ffn_w2.astype(_bf16),
    }
    y_prompt = _trunk(x_prompt, mod_p, params, chained=False)
    y_sample = _trunk(x_sample, mod_s, params, chained=True)
    return (y_prompt, y_sample)
```

```python
import functools
import math

import numpy as np
import jax
import jax.numpy as jnp
from jax import lax
from jax.experimental import pallas as pl
from jax.experimental.pallas import tpu as pltpu

D_MODEL = 1024
D_FF = 4 * D_MODEL
DEPTH = 2
EPS = 1e-6
NEG_INF = -1e30
LOG2E = math.log2(math.e)

GRID_W = 64
N_HEADS = 16
HEAD_DIM = D_MODEL // N_HEADS
WIN_H = 8
WIN_W = 16
ROWS_PER_TILE = 4
TILE_TOK = ROWS_PER_TILE * GRID_W
KEY_TILES = 3
S5_GROUP = 16
S5_GROUPS = D_MODEL // S5_GROUP
S5_STATE = 64
S5_CHUNK = 16
S5_SEGS = 8
S5_PAIRS = S5_GROUPS // 2
PAIR_W = 2 * S5_CHUNK * S5_GROUP
STATE_W = 2 * S5_STATE
NPOW = 32

VMEM_LIMIT = 56 * 1024 * 1024

_f32 = jnp.float32
_bf16 = jnp.bfloat16


def _cparams(sem):
    return pltpu.CompilerParams(dimension_semantics=sem, vmem_limit_bytes=VMEM_LIMIT)


def _resident(shape, index_map):
    return pl.BlockSpec(shape, index_map, pipeline_mode=pl.Buffered(1))


def _rms(x, gain):
    ms = jnp.mean(x * x, axis=-1, keepdims=True)
    return x * lax.rsqrt(ms + EPS) * gain


def _norm_mod(x, gain, scale, shift):
    return _rms(x, gain) * (1.0 + scale) + shift


def _mod_kernel(c_ref, w_ref, b_ref, o_ref):
    c = c_ref[...]
    act = c * jax.nn.sigmoid(c)
    o_ref[0] = jnp.dot(act, w_ref[0], preferred_element_type=_f32,
                       precision=lax.Precision.HIGHEST) + b_ref[0]


def _modulation(c_all, ada_w, ada_b):
    nb = c_all.shape[0]
    tn = 1536
    return pl.pallas_call(
        _mod_kernel,
        grid=(DEPTH, 6 * D_MODEL // tn),
        in_specs=[
            pl.BlockSpec((nb, D_MODEL), lambda i, n: (0, 0)),
            pl.BlockSpec((1, D_MODEL, tn), lambda i, n: (i, 0, n)),
            pl.BlockSpec((1, 1, tn), lambda i, n: (i, 0, n)),
        ],
        out_specs=pl.BlockSpec((1, nb, tn), lambda i, n: (i, 0, n)),
        out_shape=jax.ShapeDtypeStruct((DEPTH, nb, 6 * D_MODEL), _f32),
        compiler_params=_cparams(("arbitrary", "arbitrary")),
        name="adaln_mod",
    )(c_all, ada_w, ada_b.reshape(DEPTH, 1, 6 * D_MODEL))


def _qkv_kernel(x_ref, mod_ref, gain_ref, w_ref, o_ref):
    mod = mod_ref[0]
    h = _norm_mod(x_ref[0], gain_ref[0:1, :], mod[1:2, :], mod[0:1, :]).astype(_bf16)
    q_scale = HEAD_DIM ** -0.5 * LOG2E
    for part in range(3):
        cols = slice(part * D_MODEL, (part + 1) * D_MODEL)
        y = jnp.dot(h, w_ref[:, cols], preferred_element_type=_f32)
        if part == 0:
            y = y * q_scale
        o_ref[0, :, cols] = y.astype(_bf16)


def _qkv_proj(x, mod, gains, w_qkv, tm=1024):
    b, l, _ = x.shape
    return pl.pallas_call(
        _qkv_kernel,
        grid=(b, l // tm),
        in_specs=[
            pl.BlockSpec((1, tm, D_MODEL), lambda i, j: (i, j, 0)),
            pl.BlockSpec((1, 6, D_MODEL), lambda i, j: (i, 0, 0)),
            _resident((4, D_MODEL), lambda i, j: (0, 0)),
            _resident((D_MODEL, 3 * D_MODEL), lambda i, j: (0, 0)),
        ],
        out_specs=pl.BlockSpec((1, tm, 3 * D_MODEL), lambda i, j: (i, j, 0)),
        out_shape=jax.ShapeDtypeStruct((b, l, 3 * D_MODEL), _bf16),
        compiler_params=_cparams(("parallel", "parallel")),
        name="norm_qkv",
    )(x, mod, gains, w_qkv)


LANE_TILES = D_MODEL // 128


def _lane_perm(inverse=False):
    eye = np.eye
    p = np.einsum('sS,gG,iI->sgiGSI', eye(8), eye(8), eye(S5_GROUP)).reshape(D_MODEL, D_MODEL)
    return jnp.asarray(p.T if inverse else p, _bf16)


def _chunk_col(g, h):
    return (g % 2) * (PAIR_W // 2) + h * 128


def _chunk_stage(u, seg, ubuf):
    tmc = u.shape[0]
    for lt in range(LANE_TILES):
        ubuf[lt, seg * tmc:(seg + 1) * tmc, :] = u[:, lt * 128:(lt + 1) * 128]


def _chunk_emit(ubuf, p_ref, o_ref):
    tmc = ubuf.shape[1] // S5_SEGS
    nck = tmc // S5_CHUNK
    blk = nck * S5_SEGS
    a = jnp.concatenate(
        [jnp.concatenate([ubuf[lt, pl.ds(c * S5_CHUNK + 8 * h + s, S5_SEGS, stride=tmc), :]
                          for lt in range(LANE_TILES) for h in range(2) for c in range(nck)], axis=0)
         for s in range(8)], axis=1).astype(_bf16)
    for gp in range(4):
        out = jnp.dot(a, p_ref[:, gp * 256:(gp + 1) * 256], preferred_element_type=_f32)
        for lt in range(LANE_TILES):
            for h in range(2):
                for gl in (2 * gp, 2 * gp + 1):
                    g = 8 * lt + gl
                    col = _chunk_col(g, h)
                    piece = out[(lt * 2 + h) * blk:(lt * 2 + h + 1) * blk, (gl % 2) * 128:(gl % 2 + 1) * 128]
                    o_ref[g // 2, :, col:col + 128] = piece.astype(o_ref.dtype)


N_REL_ROWS = 2 * WIN_H - 1


def _bias_rows():
    i = np.arange(ROWS_PER_TILE)[:, None]
    kk = np.arange(KEY_TILES * ROWS_PER_TILE)[None, :]
    dr = np.stack([kk - i + 7, kk - i + 3, kk - i - 1])
    row_ok = np.stack([(kk < WIN_H) & (i >= 0), (kk >= i) & (kk < i + WIN_H), (kk >= 4) & (i >= 0)])
    return np.where(row_ok, dr, N_REL_ROWS)


def _bias_kernel(rp_ref, o_ref):
    dr_tab = _bias_rows()
    pair_w = 2 * GRID_W
    qc = lax.broadcasted_iota(jnp.int32, (GRID_W, pair_w), 0)
    lane = lax.broadcasted_iota(jnp.int32, (GRID_W, pair_w), 1)
    kc = jnp.bitwise_and(lane, GRID_W - 1)
    col_start = jnp.clip(qc - WIN_W // 2, 0, GRID_W - WIN_W)
    col_ok = jnp.logical_and(kc >= col_start, kc < col_start + WIN_W)
    left_half = lane < GRID_W
    shift = pair_w - (WIN_W - 1)
    left, right = [], []
    for dr in range(N_REL_ROWS):
        base = jnp.broadcast_to(rp_ref[0, dr:dr + 1, :], (GRID_W, pair_w))
        left.append(pltpu.roll(base, shift, 1, stride=1, stride_axis=0))
        right.append(pltpu.roll(base, (shift + GRID_W) % pair_w, 1, stride=1, stride_axis=0))
    masked = jnp.full((GRID_W, pair_w), NEG_INF, _f32)
    left.append(masked)
    right.append(masked)
    for pt in range(dr_tab.shape[0]):
        for i in range(ROWS_PER_TILE):
            for pr in range(KEY_TILES * 2):
                blk = jnp.where(left_half, left[dr_tab[pt, i, 2 * pr]], right[dr_tab[pt, i, 2 * pr + 1]])
                o_ref[pt, 0, i, pr] = jnp.where(col_ok, blk, NEG_INF).astype(o_ref.dtype)


def _attn_bias_table(rpb):
    rel_rows_pad = -(-N_REL_ROWS // 8) * 8
    rp = jnp.pad(rpb.astype(_f32) * LOG2E,
                 ((0, 0), (0, rel_rows_pad - N_REL_ROWS), (0, 2 * GRID_W - (2 * WIN_W - 1))))
    shape = (3, N_HEADS, ROWS_PER_TILE, KEY_TILES * 2, GRID_W, 2 * GRID_W)
    return pl.pallas_call(
        _bias_kernel,
        grid=(N_HEADS,),
        in_specs=[pl.BlockSpec((1, rel_rows_pad, 2 * GRID_W), lambda h: (h, 0, 0))],
        out_specs=pl.BlockSpec((3, 1) + shape[2:], lambda h: (0, h, 0, 0, 0, 0)),
        out_shape=jax.ShapeDtypeStruct(shape, _bf16),
        compiler_params=_cparams(("parallel",)),
        name="attn_bias",
    )(rp)


def _attn_heads(interior, q_ref, k_refs, v_refs, bias_ref, o_ref):
    lane = lax.broadcasted_iota(jnp.int32, (1, 128), 1)
    low = lane < HEAD_DIM
    head_mask = [low.astype(_bf16), (~low).astype(_bf16)]
    head_of_lane = jnp.right_shift(lax.broadcasted_iota(jnp.int32, (1, 256), 1), HEAD_DIM.bit_length() - 1)
    n_pairs = 2 * KEY_TILES
    zero_blk = jnp.zeros((GRID_W, 2 * GRID_W), _bf16)
    for hq in range(N_HEADS // 4):
        vcols = slice(hq * 256, (hq + 1) * 256)
        acc = None
        for h4 in range(4):
            head = 4 * hq + h4
            cols = slice((head // 2) * 128, (head // 2 + 1) * 128)
            qm = q_ref[0, :, cols] * head_mask[head % 2]
            s = [lax.dot_general(qm, k_refs[t][0, :, cols], (((1,), (1,)), ((), ())),
                                 preferred_element_type=_f32) for t in range(KEY_TILES)]
            p_rows, inv_l = [], []
            for i in range(ROWS_PER_TILE):
                pairs = range(i // 2, (i + WIN_H - 1) // 2 + 1) if interior else range(n_pairs)
                rows = slice(i * GRID_W, (i + 1) * GRID_W)
                blk = {pr: s[pr // 2][rows, (pr % 2) * 128:(pr % 2 + 1) * 128]
                       + bias_ref[0, head, i, pr].astype(_f32) for pr in pairs}
                m = None
                for pr in pairs:
                    m = blk[pr] if m is None else jnp.maximum(m, blk[pr])
                m = jnp.max(m, axis=-1, keepdims=True)
                e = {pr: jnp.exp2(blk[pr] - m) for pr in pairs}
                tot = None
                for pr in pairs:
                    tot = e[pr] if tot is None else tot + e[pr]
                inv_l.append(1.0 / jnp.sum(tot, axis=-1, keepdims=True))
                p_rows.append([e[pr].astype(_bf16) if pr in e else zero_blk for pr in range(n_pairs)])
            o = None
            for t in range(KEY_TILES):
                p_t = jnp.concatenate(
                    [jnp.concatenate([p_rows[i][2 * t], p_rows[i][2 * t + 1]], axis=1)
                     for i in range(ROWS_PER_TILE)], axis=0)
                part = jnp.dot(p_t, v_refs[t][0, :, vcols], preferred_element_type=_f32)
                o = part if o is None else o + part
            o = o * jnp.concatenate(inv_l, axis=0)
            acc = o if acc is None else jnp.where(head_of_lane == h4, o, acc)
        o_ref[0, :, vcols] = acc.astype(o_ref.dtype)


def _attn_kernel(q_ref, k0_ref, k1_ref, k2_ref, v0_ref, v1_ref, v2_ref, bias_ref, o_ref):
    j = pl.program_id(1)
    interior = jnp.logical_and(j > 0, j < pl.num_programs(1) - 1)
    args = (q_ref, (k0_ref, k1_ref, k2_ref), (v0_ref, v1_ref, v2_ref), bias_ref, o_ref)
    pl.when(interior)(functools.partial(_attn_heads, True, *args))
    pl.when(jnp.logical_not(interior))(functools.partial(_attn_heads, False, *args))


def _attention(qkv, bias):
    b, l, _ = qkv.shape
    nt = l // TILE_TOK
    assert nt >= KEY_TILES

    def base(j):
        return jnp.clip(j - 1, 0, nt - KEY_TILES)

    def pattern(j):
        return jnp.where(j == 0, 0, jnp.where(j == nt - 1, 2, 1))

    def kv_spec(part, t):
        return pl.BlockSpec((1, TILE_TOK, D_MODEL), lambda i, j: (i, base(j) + t, part))

    return pl.pallas_call(
        _attn_kernel,
        grid=(b, nt),
        in_specs=[pl.BlockSpec((1, TILE_TOK, D_MODEL), lambda i, j: (i, j, 0))]
        + [kv_spec(1, t) for t in range(KEY_TILES)]
        + [kv_spec(2, t) for t in range(KEY_TILES)]
        + [pl.BlockSpec((1, N_HEADS, ROWS_PER_TILE, KEY_TILES * 2, GRID_W, 2 * GRID_W),
                        lambda i, j: (pattern(j), 0, 0, 0, 0, 0))],
        out_specs=pl.BlockSpec((1, TILE_TOK, D_MODEL), lambda i, j: (i, j, 0)),
        out_shape=jax.ShapeDtypeStruct((b, l, D_MODEL), _bf16),
        compiler_params=_cparams(("parallel", "arbitrary")),
        name="nbr_attention",
    )(qkv, qkv, qkv, qkv, qkv, qkv, qkv, bias)


def _s5_tables(a_re, a_im, log_dt, b_re, b_im, c_re, c_im):
    f = lambda v: v.astype(_f32)
    pairs = (2, S5_PAIRS)
    a_row = [f(a).reshape(*pairs, 1, STATE_W) for a in (a_re, a_im)]
    a_col = [f(a).reshape(*pairs, STATE_W, 1) for a in (a_re, a_im)]
    ldt = jnp.repeat(f(log_dt), S5_STATE, axis=-1)
    ldt_row = ldt.reshape(*pairs, 1, STATE_W)
    ldt_col = ldt.reshape(*pairs, STATE_W, 1)
    b_t = [jnp.transpose(f(b).reshape(*pairs, 2, S5_STATE, S5_GROUP), (0, 1, 4, 2, 3))
           .reshape(*pairs, S5_GROUP, STATE_W) for b in (b_re, b_im)]
    c_t = [jnp.transpose(f(c), (0, 1, 3, 2)).reshape(*pairs, STATE_W, S5_GROUP) for c in (c_re, c_im)]

    def spec(shape):
        nd = len(shape)
        return pl.BlockSpec((2, 1) + tuple(shape[2:]), lambda p: (0, p) + (0,) * (nd - 2))

    ins = a_row + a_col + [ldt_row, ldt_col] + b_t + c_t
    out_shapes = (
        jax.ShapeDtypeStruct((S5_GROUPS, PAIR_W // 2, PAIR_W // 2), _bf16),
        jax.ShapeDtypeStruct((2, S5_PAIRS, PAIR_W, 2 * STATE_W), _bf16),
        jax.ShapeDtypeStruct((2, S5_PAIRS, 2 * STATE_W, PAIR_W), _bf16),
        jax.ShapeDtypeStruct((2, S5_PAIRS, S5_SEGS, 2 * STATE_W), _f32),
    )
    return pl.pallas_call(
        _s5_tables_kernel,
        grid=(S5_PAIRS,),
        in_specs=[spec(v.shape) for v in ins],
        out_specs=(pl.BlockSpec((2, PAIR_W // 2, PAIR_W // 2), lambda p: (p, 0, 0)),
                   spec(out_shapes[1].shape), spec(out_shapes[2].shape), spec(out_shapes[3].shape)),
        out_shape=out_shapes,
        compiler_params=_cparams(("parallel",)),
        name="s5_tables",
    )(*ins)


def _cexp(k, a_re, a_im, dt):
    mag = jnp.exp(k * (a_re * dt))
    ang = k * (a_im * dt)
    return mag * jnp.cos(ang), mag * jnp.sin(ang)


def _s5_tables_kernel(ar_re_ref, ar_im_ref, ac_re_ref, ac_im_ref, ldr_ref, ldc_ref,
                      bt_re_ref, bt_im_ref, ct_re_ref, ct_im_ref, m_ref, w_ref, v_ref, lam_ref):
    hi = lax.Precision.HIGHEST
    t = S5_CHUNK
    cw = t * S5_GROUP
    lane_c = lax.broadcasted_iota(jnp.int32, (1, cw), 1)
    log_group = S5_GROUP.bit_length() - 1
    lag = jnp.right_shift(lane_c, log_group).astype(_f32)
    tok_row = jnp.right_shift(lax.broadcasted_iota(jnp.int32, (cw, 1), 0), log_group).astype(_f32)
    low = lax.broadcasted_iota(jnp.int32, (1, STATE_W), 1) < S5_STATE
    tile_i = (lax.broadcasted_iota(jnp.int32, (S5_GROUP, cw), 0)
              == jnp.bitwise_and(lax.broadcasted_iota(jnp.int32, (S5_GROUP, cw), 1), S5_GROUP - 1)).astype(_f32)
    k_sub = jnp.minimum(lax.broadcasted_iota(jnp.int32, (NPOW, 1), 0), t).astype(_f32)
    k_lane = jnp.minimum(lax.broadcasted_iota(jnp.int32, (1, STATE_W), 1), t).astype(_f32)
    pow_of_row = lax.broadcasted_iota(jnp.int32, (cw, NPOW), 1).astype(_f32)
    pow_of_lane = lax.broadcasted_iota(jnp.int32, (STATE_W, cw), 0).astype(_f32)

    def spread(a, b, data_left=True):
        data, sel = (a, b) if data_left else (b, a)
        top = data.astype(_bf16)
        rest = data - top.astype(_f32)
        mid = rest.astype(_bf16)
        low = (rest - mid.astype(_f32)).astype(_bf16)
        sel = sel.astype(_bf16)
        parts = [jnp.dot(p, sel, preferred_element_type=_f32) if data_left
                 else jnp.dot(sel, p, preferred_element_type=_f32) for p in (top, mid, low)]
        return (parts[0] + parts[1]) + parts[2]

    g_lag = [[None, None], [None, None]]
    for d in range(2):
        a_re, a_im = ar_re_ref[d, 0], ar_im_ref[d, 0]
        pw_re, pw_im = _cexp(k_sub, a_re, a_im, jnp.exp(ldr_ref[d, 0]))
        lam_re, lam_im = pw_re[1:2], pw_im[1:2]
        den = a_re * a_re + a_im * a_im
        nr, ni = lam_re - 1.0, lam_im
        zr = (nr * a_re + ni * a_im) / den
        zi = (ni * a_re - nr * a_im) / den
        bb_re = zr * bt_re_ref[d, 0] - zi * bt_im_ref[d, 0]
        bb_im = zr * bt_im_ref[d, 0] + zi * bt_re_ref[d, 0]
        sel_tok = (pow_of_row == ((t - 1.0) - tok_row if d == 0 else tok_row)).astype(_f32)
        p_re = spread(sel_tok, pw_re, data_left=False)
        p_im = spread(sel_tok, pw_im, data_left=False)
        bt_re = jnp.concatenate([bb_re] * t, axis=0)
        bt_im = jnp.concatenate([bb_im] * t, axis=0)
        w_re, w_im = _cmul(p_re, p_im, bt_re, bt_im)
        top = jnp.concatenate([jnp.where(low, w_re, 0.0), jnp.where(low, w_im, 0.0)], axis=1)
        bot = jnp.concatenate([jnp.where(low, 0.0, w_re), jnp.where(low, 0.0, w_im)], axis=1)
        w_ref[d, 0] = jnp.concatenate([top, bot], axis=0).astype(w_ref.dtype)
        lam_ref[d, 0] = jnp.broadcast_to(jnp.concatenate([pw_re[t:t + 1], pw_im[t:t + 1]], axis=1),
                                         (S5_SEGS, 2 * STATE_W))
        pc_re, pc_im = _cexp(k_lane, ac_re_ref[d, 0], ac_im_ref[d, 0], jnp.exp(ldc_ref[d, 0]))
        ct_re = spread(ct_re_ref[d, 0], tile_i)
        ct_im = spread(ct_im_ref[d, 0], tile_i)

        def c_pow(k_of_block):
            sel = (pow_of_lane == k_of_block).astype(_f32)
            return _cmul(ct_re, ct_im, spread(pc_re, sel), spread(pc_im, sel))

        vb_re, vb_im = c_pow(lag + 1.0 if d == 0 else float(t) - lag)
        zero = jnp.zeros((S5_STATE, cw), _f32)
        v_ref[d, 0] = jnp.concatenate([
            jnp.concatenate([vb_re[:S5_STATE], zero], axis=1), jnp.concatenate([zero, vb_re[S5_STATE:]], axis=1),
            jnp.concatenate([-vb_im[:S5_STATE], zero], axis=1), jnp.concatenate([zero, -vb_im[S5_STATE:]], axis=1),
        ], axis=0).astype(v_ref.dtype)
        ce_re, ce_im = c_pow(lag if d == 0 else (t - 1.0) - lag)
        for g2 in range(2):
            sel = low if g2 == 0 else ~low
            g_lag[d][g2] = (
                jnp.dot(jnp.where(sel, bb_re, 0.0), ce_re, preferred_element_type=_f32, precision=hi)
                - jnp.dot(jnp.where(sel, bb_im, 0.0), ce_im, preferred_element_type=_f32, precision=hi))
    for g2 in range(2):
        rows = []
        for s in range(t):
            fwd = g_lag[0][g2] if s == 0 else pltpu.roll(g_lag[0][g2], S5_GROUP * s, 1)
            fwd = jnp.where(lane_c >= S5_GROUP * s, fwd, 0.0)
            shift = (cw - S5_GROUP * (t - 1 - s)) % cw
            bwd = g_lag[1][g2] if shift == 0 else pltpu.roll(g_lag[1][g2], shift, 1)
            bwd = jnp.where(lane_c < S5_GROUP * (s + 1), bwd, 0.0)
            rows.append(fwd + bwd)
        m_ref[g2] = jnp.concatenate(rows, axis=0).astype(m_ref.dtype)


def _cmul(ar, ai, br, bi):
    return ar * br - ai * bi, ar * bi + ai * br


def _s5_kernel(chained, u_ref, m_ref, w_ref, v_ref, lam_ref, y_ref, x_scr, h_scr):
    rows = u_ref.shape[1]
    nc = rows // S5_SEGS
    rb = min(512, rows)
    sw = STATE_W

    for r in range(rows // rb):
        rs = slice(r * rb, (r + 1) * rb)
        u = u_ref[0, rs, :]
        for d in range(2):
            x_scr[d, rs, :] = jnp.dot(u, w_ref[d, 0], preferred_element_type=_f32)

    lam = [(lam_ref[d, 0, :, :sw], lam_ref[d, 0, :, sw:]) for d in range(2)]

    def row_of(d, c):
        cc = c if d == 0 else nc - 1 - c
        return pl.ds(pl.multiple_of(cc * S5_SEGS, S5_SEGS), S5_SEGS)

    def scan_body(c, carry):
        new = []
        for d in range(2):
            hr, hi = carry[d]
            rs = row_of(d, c)
            h_scr[d, rs, :sw] = hr
            h_scr[d, rs, sw:] = hi
            pr, pi = _cmul(lam[d][0], lam[d][1], hr, hi)
            new.append((pr + x_scr[d, rs, :sw], pi + x_scr[d, rs, sw:]))
        return tuple(new)

    zero = jnp.zeros((S5_SEGS, sw), _f32)
    ends = lax.fori_loop(0, nc, scan_body, ((zero, zero), (zero, zero)), unroll=4)

    if chained:
        one = jnp.ones((S5_SEGS, sw), _f32)
        h_in = []
        for d in range(2):
            er, ei = ends[d]
            dr, di = one[0:1], zero[0:1]
            br, bi = lam[d][0][0:1], lam[d][1][0:1]
            n = nc
            while n:
                if n & 1:
                    dr, di = _cmul(dr, di, br, bi)
                n >>= 1
                if n:
                    br, bi = _cmul(br, bi, br, bi)
            order = range(S5_SEGS) if d == 0 else range(S5_SEGS - 1, -1, -1)
            cr = jnp.zeros((1, sw), _f32)
            ci = jnp.zeros((1, sw), _f32)
            rows_r = [None] * S5_SEGS
            rows_i = [None] * S5_SEGS
            for sgm in order:
                rows_r[sgm], rows_i[sgm] = cr, ci
                pr, pi = _cmul(dr, di, cr, ci)
                cr, ci = pr + er[sgm:sgm + 1], pi + ei[sgm:sgm + 1]
            h_in.append((jnp.concatenate(rows_r, axis=0), jnp.concatenate(rows_i, axis=0)))

        def fix_body(c, carry):
            new = []
            for d in range(2):
                pr, pi = carry[d]
                rs = row_of(d, c)
                ar, ai = _cmul(pr, pi, h_in[d][0], h_in[d][1])
                h_scr[d, rs, :sw] = h_scr[d, rs, :sw] + ar
                h_scr[d, rs, sw:] = h_scr[d, rs, sw:] + ai
                new.append(_cmul(lam[d][0], lam[d][1], pr, pi))
            return tuple(new)

        lax.fori_loop(0, nc, fix_body, ((one, zero), (one, zero)), unroll=4)

    half = PAIR_W // 2

    for r in range(rows // rb):
        rs = slice(r * rb, (r + 1) * rb)
        u = u_ref[0, rs, :]
        y = (jnp.dot(h_scr[0, rs, :].astype(_bf16), v_ref[0, 0], preferred_element_type=_f32)
             + jnp.dot(h_scr[1, rs, :].astype(_bf16), v_ref[1, 0], preferred_element_type=_f32))
        y0 = y[:, :half] + jnp.dot(u[:, :half], m_ref[0], preferred_element_type=_f32)
        y1 = y[:, half:] + jnp.dot(u[:, half:], m_ref[1], preferred_element_type=_f32)
        y_ref[0, rs, :half] = y0.astype(y_ref.dtype)
        y_ref[0, rs, half:] = y1.astype(y_ref.dtype)


def _s5_core(u_t, tables, chained):
    m_sum, w_in, v_out, lam_t = tables
    _, rows, _ = u_t.shape
    return pl.pallas_call(
        functools.partial(_s5_kernel, chained),
        grid=(S5_PAIRS,),
        in_specs=[
            pl.BlockSpec((1, rows, PAIR_W), lambda p: (p, 0, 0)),
            pl.BlockSpec((2, PAIR_W // 2, PAIR_W // 2), lambda p: (p, 0, 0)),
            pl.BlockSpec((2, 1, PAIR_W, 2 * STATE_W), lambda p: (0, p, 0, 0)),
            pl.BlockSpec((2, 1, 2 * STATE_W, PAIR_W), lambda p: (0, p, 0, 0)),
            pl.BlockSpec((2, 1, S5_SEGS, 2 * STATE_W), lambda p: (0, p, 0, 0)),
        ],
        out_specs=pl.BlockSpec((1, rows, PAIR_W), lambda p: (p, 0, 0)),
        out_shape=jax.ShapeDtypeStruct((S5_PAIRS, rows, PAIR_W), _bf16),
        scratch_shapes=[pltpu.VMEM((2, rows, 2 * STATE_W), _f32),
                        pltpu.VMEM((2, rows, 2 * STATE_W), _f32)],
        compiler_params=_cparams(("parallel",)),
        name="s5_core",
    )(u_t, m_sum, w_in, v_out, lam_t)


def _gelu_tanh(x):
    return 0.5 * x * (1.0 + jnp.tanh(math.sqrt(2.0 / math.pi) * (x + 0.044715 * (x * x * x))))


def _post_kernel(mixer, *refs):
    if mixer == "attn":
        (x_ref, mix_ref, mod_ref, gain_ref, modn_ref, gainn_ref, perm_ref, wp_ref, w1_ref, w2_ref, o_ref, u_ref,
         x1_scr, h_scr, acc_ref, ubuf) = refs
    else:
        (x_ref, mix_ref, mod_ref, gain_ref, skip_ref, pt_ref, wp_ref, w1_ref, w2_ref, o_ref,
         x1_scr, h_scr, acc_ref, nat_scr) = refs
    tmc = x_ref.shape[1]

    def rows(seg):
        return slice(seg * tmc, (seg + 1) * tmc)

    if mixer == "attn":
        for seg in range(S5_SEGS):
            h_scr[rows(seg), :] = mix_ref[seg]
        y = jnp.dot(h_scr[...], wp_ref[...], preferred_element_type=_f32)
    else:
        nck = tmc // S5_CHUNK
        blk = nck * S5_SEGS
        a = jnp.concatenate(
            [jnp.concatenate([mix_ref[(8 * lt + gl) // 2, :, _chunk_col(8 * lt + gl, h):_chunk_col(8 * lt + gl, h) + 128]
                              for gl in range(8)], axis=1)
             for lt in range(LANE_TILES) for h in range(2)], axis=0)
        for sp in range(4):
            out = jnp.dot(a, pt_ref[:, sp * 256:(sp + 1) * 256], preferred_element_type=_f32)
            for lt in range(LANE_TILES):
                for h in range(2):
                    for s in (2 * sp, 2 * sp + 1):
                        for c in range(nck):
                            tok = c * S5_CHUNK + 8 * h + s
                            row = (lt * 2 + h) * blk + c * S5_SEGS
                            nat_scr[lt, tok * S5_SEGS:(tok + 1) * S5_SEGS, :] = (
                                out[row:row + S5_SEGS, (s % 2) * 128:(s % 2 + 1) * 128])
        for seg in range(S5_SEGS):
            mod = mod_ref[seg]
            u = _norm_mod(x_ref[seg], gain_ref[0:1, :], mod[1:2, :], mod[0:1, :])
            y_ssm = jnp.concatenate([nat_scr[lt, pl.ds(seg, tmc, stride=S5_SEGS), :]
                                     for lt in range(D_MODEL // 128)], axis=1)
            h_scr[rows(seg), :] = _gelu_tanh(skip_ref[...] * u + y_ssm).astype(_bf16)
        g = h_scr[...]
        val = jnp.dot(g, wp_ref[:, :D_MODEL], preferred_element_type=_f32)
        gate = jnp.dot(g, wp_ref[:, D_MODEL:], preferred_element_type=_f32)
        y = val * jax.nn.sigmoid(gate)
    for seg in range(S5_SEGS):
        mod = mod_ref[seg]
        x1 = x_ref[seg] + mod[2:3, :] * _rms(y[rows(seg), :], gain_ref[1:2, :])
        x1_scr[rows(seg), :] = x1
        h_scr[rows(seg), :] = _norm_mod(x1, gain_ref[2:3, :], mod[4:5, :], mod[3:4, :]).astype(_bf16)
    h2 = h_scr[...]
    tf = 512
    for kf in range(D_FF // tf):
        cols = slice(kf * tf, (kf + 1) * tf)
        a = jnp.dot(h2, w1_ref[:, cols], preferred_element_type=_f32)
        a = jnp.square(jnp.maximum(a, 0.0)).astype(_bf16)
        part = jnp.dot(a, w2_ref[cols, :], preferred_element_type=_f32)
        if kf == 0:
            acc_ref[...] = part
        else:
            acc_ref[...] += part
    for seg in range(S5_SEGS):
        mod = mod_ref[seg]
        out = x1_scr[rows(seg), :] + mod[5:6, :] * _rms(acc_ref[rows(seg), :], gain_ref[3:4, :])
        o_ref[seg] = out
        if mixer == "attn":
            modn = modn_ref[seg]
            _chunk_stage(_norm_mod(out, gainn_ref[0:1, :], modn[1:2, :], modn[0:1, :]), seg, ubuf)
    if mixer == "attn":
        _chunk_emit(ubuf, perm_ref, u_ref)


def _post_mixer(mixer, x8, mix, mod8, gains, w_proj, w1, w2, skip=None, perm=None, next_mod8=None, next_gains=None,
                tmc=64):
    _, ls, _ = x8.shape
    tok = lambda j: (0, j, 0)
    const2 = lambda j: (0, 0)
    m = S5_SEGS * tmc
    chunk_rows = tmc // S5_CHUNK * S5_SEGS
    mod_spec = _resident((S5_SEGS, 6, D_MODEL), lambda j: (0, 0, 0))
    in_specs = [pl.BlockSpec((S5_SEGS, tmc, D_MODEL), tok)]
    scratch = [pltpu.VMEM((m, D_MODEL), _f32), pltpu.VMEM((m, D_MODEL), _bf16), pltpu.VMEM((m, D_MODEL), _f32)]
    out_specs = pl.BlockSpec((S5_SEGS, tmc, D_MODEL), tok)
    out_shape = jax.ShapeDtypeStruct(x8.shape, _f32)
    if mixer == "attn":
        in_specs.append(pl.BlockSpec((S5_SEGS, tmc, D_MODEL), tok))
        scratch.append(pltpu.VMEM((LANE_TILES, m, 128), _f32))
        out_specs = (out_specs, pl.BlockSpec((S5_PAIRS, chunk_rows, PAIR_W), tok))
        out_shape = (out_shape, jax.ShapeDtypeStruct((S5_PAIRS, ls // S5_CHUNK * S5_SEGS, PAIR_W), _bf16))
    else:
        in_specs.append(pl.BlockSpec((S5_PAIRS, chunk_rows, PAIR_W), tok))
        scratch.append(pltpu.VMEM((LANE_TILES, m, 128), _f32))
    in_specs += [mod_spec, _resident((4, D_MODEL), const2)]
    args = [x8, mix, mod8, gains]
    if mixer == "attn":
        in_specs += [mod_spec, _resident((4, D_MODEL), const2), _resident((D_MODEL, D_MODEL), const2)]
        args += [next_mod8, next_gains, perm]
    else:
        in_specs += [_resident((1, D_MODEL), const2), _resident((D_MODEL, D_MODEL), const2)]
        args += [skip, perm]
    in_specs += [_resident(w_proj.shape, const2), _resident(w1.shape, const2), _resident(w2.shape, const2)]
    args += [w_proj, w1, w2]
    return pl.pallas_call(
        functools.partial(_post_kernel, mixer),
        grid=(ls // tmc,),
        in_specs=in_specs,
        out_specs=out_specs,
        out_shape=out_shape,
        scratch_shapes=scratch,
        compiler_params=_cparams(("parallel",)),
        name="post_" + mixer,
    )(*args)


def _trunk(x, mod, params, chained):
    b, l, _ = x.shape
    assert (b == 1) if chained else (b == S5_SEGS)
    ls = b * l // S5_SEGS
    gains = params["norm_gain"]
    mod8 = jnp.broadcast_to(mod, (DEPTH, S5_SEGS, 6, D_MODEL))
    qkv = _qkv_proj(x, mod[0], gains[0], params["w_qkv"])
    att = _attention(qkv, params["attn_bias"])
    x8, u = _post_mixer("attn", x.reshape(S5_SEGS, ls, D_MODEL), att.reshape(S5_SEGS, ls, D_MODEL), mod8[0],
                        gains[0], params["w_o"], params["ffn_w1"][0], params["ffn_w2"][0],
                        perm=_lane_perm(), next_mod8=mod8[1], next_gains=gains[1])
    y = _s5_core(u, params["s5_tables"], chained)
    x8 = _post_mixer("s5", x8, y, mod8[1], gains[1], params["w_glu"], params["ffn_w1"][1], params["ffn_w2"][1],
                     skip=params["s5_d"], perm=_lane_perm(inverse=True))
    return x8.reshape(b, l, D_MODEL)


def kernel(x_prompt, x_sample, c_prompt, c_sample, ada_w, ada_b, norm_gain, attn_w_qkv, attn_w_o, attn_rpb,
           s5_a_re, s5_a_im, s5_log_dt, s5_b_re, s5_b_im, s5_c_re, s5_c_im, s5_d, s5_w_glu, ffn_w1, ffn_w2):
    nbp, nbs = c_prompt.shape[0], c_sample.shape[0]
    nb = -(-(nbp + nbs) // 8) * 8
    c_all = jnp.concatenate([c_prompt, c_sample, jnp.zeros((nb - nbp - nbs, D_MODEL), _f32)], axis=0)
    mod = _modulation(c_all, ada_w, ada_b)
    mod_p = mod[:, :nbp].reshape(DEPTH, nbp, 6, D_MODEL)
    mod_s = mod[:, nbp:nbp + nbs].reshape(DEPTH, nbs, 6, D_MODEL)

    params = {
        "norm_gain": norm_gain,
        "w_qkv": attn_w_qkv[0].astype(_bf16),
        "w_o": attn_w_o[0].astype(_bf16),
        "attn_bias": _attn_bias_table(attn_rpb[0]),
        "s5_tables": _s5_tables(s5_a_re[0], s5_a_im[0], s5_log_dt[0], s5_b_re[0], s5_b_im[0],
                                s5_c_re[0], s5_c_im[0]),
        "s5_d": s5_d[0].reshape(1, D_MODEL),
        "w_glu": s5_w_glu[0].astype(_bf16),
        "ffn_w1": ffn_w1.astype(_bf16),
        "ffn_w2": ffn_w2.astype(_bf16),
    }
    y_prompt = _trunk(x_prompt, mod_p, params, chained=False)
    y_sample = _trunk(x_sample, mod_s, params, chained=True)
    return (y_prompt, y_sample)
```

```python
import functools
import math

import numpy as np
import jax
import jax.numpy as jnp
from jax import lax
from jax.experimental import pallas as pl
from jax.experimental.pallas import tpu as pltpu

D_MODEL = 1024
D_FF = 4 * D_MODEL
DEPTH = 2
EPS = 1e-6
NEG_INF = -1e30
LOG2E = math.log2(math.e)

GRID_W = 64
N_HEADS = 16
HEAD_DIM = D_MODEL // N_HEADS
WIN_H = 8
WIN_W = 16
ROWS_PER_TILE = 4
TILE_TOK = ROWS_PER_TILE * GRID_W
KEY_TILES = 3
S5_GROUP = 16
S5_GROUPS = D_MODEL // S5_GROUP
S5_STATE = 64
S5_CHUNK = 16
S5_SEGS = 8
S5_PAIRS = S5_GROUPS // 2
PAIR_W = 2 * S5_CHUNK * S5_GROUP
STATE_W = 2 * S5_STATE
NPOW = 32

VMEM_LIMIT = 56 * 1024 * 1024

_f32 = jnp.float32
_bf16 = jnp.bfloat16


def _cparams(sem):
    return pltpu.CompilerParams(dimension_semantics=sem, vmem_limit_bytes=VMEM_LIMIT)


def _resident(shape, index_map):
    return pl.BlockSpec(shape, index_map, pipeline_mode=pl.Buffered(1))


def _rms(x, gain):
    ms = jnp.mean(x * x, axis=-1, keepdims=True)
    return x * lax.rsqrt(ms + EPS) * gain


def _norm_mod(x, gain, scale, shift):
    return _rms(x, gain * (1.0 + scale)) + shift


def _mod_kernel(c_ref, w_ref, b_ref, o_ref):
    c = c_ref[...]
    act = c * jax.nn.sigmoid(c)
    o_ref[0] = jnp.dot(act, w_ref[0], preferred_element_type=_f32,
                       precision=lax.Precision.HIGHEST) + b_ref[0]


def _modulation(c_all, ada_w, ada_b):
    nb = c_all.shape[0]
    tn = 1536
    return pl.pallas_call(
        _mod_kernel,
        grid=(DEPTH, 6 * D_MODEL // tn),
        in_specs=[
            pl.BlockSpec((nb, D_MODEL), lambda i, n: (0, 0)),
            pl.BlockSpec((1, D_MODEL, tn), lambda i, n: (i, 0, n)),
            pl.BlockSpec((1, 1, tn), lambda i, n: (i, 0, n)),
        ],
        out_specs=pl.BlockSpec((1, nb, tn), lambda i, n: (i, 0, n)),
        out_shape=jax.ShapeDtypeStruct((DEPTH, nb, 6 * D_MODEL), _f32),
        compiler_params=_cparams(("arbitrary", "arbitrary")),
        name="adaln_mod",
    )(c_all, ada_w, ada_b.reshape(DEPTH, 1, 6 * D_MODEL))


def _qkv_kernel(x_ref, mod_ref, gain_ref, w_ref, o_ref):
    mod = mod_ref[0]
    h = _norm_mod(x_ref[0], gain_ref[0:1, :], mod[1:2, :], mod[0:1, :]).astype(_bf16)
    q_scale = HEAD_DIM ** -0.5 * LOG2E
    for part in range(3):
        cols = slice(part * D_MODEL, (part + 1) * D_MODEL)
        y = jnp.dot(h, w_ref[:, cols], preferred_element_type=_f32)
        if part == 0:
            y = y * q_scale
        o_ref[0, :, cols] = y.astype(_bf16)


def _qkv_proj(x, mod, gains, w_qkv, tm=1024):
    b, l, _ = x.shape
    return pl.pallas_call(
        _qkv_kernel,
        grid=(b, l // tm),
        in_specs=[
            pl.BlockSpec((1, tm, D_MODEL), lambda i, j: (i, j, 0)),
            pl.BlockSpec((1, 6, D_MODEL), lambda i, j: (i, 0, 0)),
            _resident((4, D_MODEL), lambda i, j: (0, 0)),
            _resident((D_MODEL, 3 * D_MODEL), lambda i, j: (0, 0)),
        ],
        out_specs=pl.BlockSpec((1, tm, 3 * D_MODEL), lambda i, j: (i, j, 0)),
        out_shape=jax.ShapeDtypeStruct((b, l, 3 * D_MODEL), _bf16),
        compiler_params=_cparams(("parallel", "parallel")),
        name="norm_qkv",
    )(x, mod, gains, w_qkv)


LANE_TILES = D_MODEL // 128


def _lane_perm(inverse=False):
    eye = np.eye
    p = np.einsum('sS,gG,iI->sgiGSI', eye(8), eye(8), eye(S5_GROUP)).reshape(D_MODEL, D_MODEL)
    return jnp.asarray(p.T if inverse else p, _bf16)


def _chunk_col(g, h):
    return (g % 2) * (PAIR_W // 2) + h * 128


def _chunk_stage(u, seg, ubuf):
    tmc = u.shape[0]
    for lt in range(LANE_TILES):
        ubuf[lt, seg * tmc:(seg + 1) * tmc, :] = u[:, lt * 128:(lt + 1) * 128]


def _chunk_emit(ubuf, p_ref, o_ref):
    tmc = ubuf.shape[1] // S5_SEGS
    nck = tmc // S5_CHUNK
    blk = nck * S5_SEGS
    a = jnp.concatenate(
        [jnp.concatenate([ubuf[lt, pl.ds(c * S5_CHUNK + 8 * h + s, S5_SEGS, stride=tmc), :]
                          for lt in range(LANE_TILES) for h in range(2) for c in range(nck)], axis=0)
         for s in range(8)], axis=1).astype(_bf16)
    for gp in range(4):
        out = jnp.dot(a, p_ref[:, gp * 256:(gp + 1) * 256], preferred_element_type=_f32)
        for lt in range(LANE_TILES):
            for h in range(2):
                for gl in (2 * gp, 2 * gp + 1):
                    g = 8 * lt + gl
                    col = _chunk_col(g, h)
                    piece = out[(lt * 2 + h) * blk:(lt * 2 + h + 1) * blk, (gl % 2) * 128:(gl % 2 + 1) * 128]
                    o_ref[g // 2, :, col:col + 128] = piece.astype(o_ref.dtype)


N_REL_ROWS = 2 * WIN_H - 1


def _bias_rows():
    i = np.arange(ROWS_PER_TILE)[:, None]
    kk = np.arange(KEY_TILES * ROWS_PER_TILE)[None, :]
    dr = np.stack([kk - i + 7, kk - i + 3, kk - i - 1])
    row_ok = np.stack([(kk < WIN_H) & (i >= 0), (kk >= i) & (kk < i + WIN_H), (kk >= 4) & (i >= 0)])
    return np.where(row_ok, dr, N_REL_ROWS)


def _bias_kernel(rp_ref, o_ref):
    dr_tab = _bias_rows()
    pair_w = 2 * GRID_W
    qc = lax.broadcasted_iota(jnp.int32, (GRID_W, pair_w), 0)
    lane = lax.broadcasted_iota(jnp.int32, (GRID_W, pair_w), 1)
    kc = jnp.bitwise_and(lane, GRID_W - 1)
    col_start = jnp.clip(qc - WIN_W // 2, 0, GRID_W - WIN_W)
    col_ok = jnp.logical_and(kc >= col_start, kc < col_start + WIN_W)
    left_half = lane < GRID_W
    shift = pair_w - (WIN_W - 1)
    left, right = [], []
    for dr in range(N_REL_ROWS):
        base = jnp.broadcast_to(rp_ref[0, dr:dr + 1, :], (GRID_W, pair_w))
        left.append(pltpu.roll(base, shift, 1, stride=1, stride_axis=0))
        right.append(pltpu.roll(base, (shift + GRID_W) % pair_w, 1, stride=1, stride_axis=0))
    masked = jnp.full((GRID_W, pair_w), NEG_INF, _f32)
    left.append(masked)
    right.append(masked)
    for pt in range(dr_tab.shape[0]):
        for i in range(ROWS_PER_TILE):
            for pr in range(KEY_TILES * 2):
                blk = jnp.where(left_half, left[dr_tab[pt, i, 2 * pr]], right[dr_tab[pt, i, 2 * pr + 1]])
                o_ref[pt, 0, i, pr] = jnp.where(col_ok, blk, NEG_INF).astype(o_ref.dtype)


def _attn_bias_table(rpb):
    rel_rows_pad = -(-N_REL_ROWS // 8) * 8
    rp = jnp.pad(rpb.astype(_f32) * LOG2E,
                 ((0, 0), (0, rel_rows_pad - N_REL_ROWS), (0, 2 * GRID_W - (2 * WIN_W - 1))))
    shape = (3, N_HEADS, ROWS_PER_TILE, KEY_TILES * 2, GRID_W, 2 * GRID_W)
    return pl.pallas_call(
        _bias_kernel,
        grid=(N_HEADS,),
        in_specs=[pl.BlockSpec((1, rel_rows_pad, 2 * GRID_W), lambda h: (h, 0, 0))],
        out_specs=pl.BlockSpec((3, 1) + shape[2:], lambda h: (0, h, 0, 0, 0, 0)),
        out_shape=jax.ShapeDtypeStruct(shape, _bf16),
        compiler_params=_cparams(("parallel",)),
        name="attn_bias",
    )(rp)


def _attn_heads(interior, q_ref, k_refs, v_refs, bias_ref, o_ref):
    lane = lax.broadcasted_iota(jnp.int32, (1, 128), 1)
    low = lane < HEAD_DIM
    head_mask = [low.astype(_bf16), (~low).astype(_bf16)]
    head_of_lane = jnp.right_shift(lax.broadcasted_iota(jnp.int32, (1, 256), 1), HEAD_DIM.bit_length() - 1)
    n_pairs = 2 * KEY_TILES
    zero_blk = jnp.zeros((GRID_W, 2 * GRID_W), _bf16)
    for hq in range(N_HEADS // 4):
        vcols = slice(hq * 256, (hq + 1) * 256)
        acc = None
        for h4 in range(4):
            head = 4 * hq + h4
            cols = slice((head // 2) * 128, (head // 2 + 1) * 128)
            qm = q_ref[0, :, cols] * head_mask[head % 2]
            s = [lax.dot_general(qm, k_refs[t][0, :, cols], (((1,), (1,)), ((), ())),
                                 preferred_element_type=_f32) for t in range(KEY_TILES)]
            p_rows, inv_l = [], []
            for i in range(ROWS_PER_TILE):
                pairs = range(i // 2, (i + WIN_H - 1) // 2 + 1) if interior else range(n_pairs)
                rows = slice(i * GRID_W, (i + 1) * GRID_W)
                blk = {pr: s[pr // 2][rows, (pr % 2) * 128:(pr % 2 + 1) * 128]
                       + bias_ref[0, head, i, pr].astype(_f32) for pr in pairs}
                m = None
                for pr in pairs:
                    m = blk[pr] if m is None else jnp.maximum(m, blk[pr])
                m = jnp.max(m, axis=-1, keepdims=True)
                e = {pr: jnp.exp2(blk[pr] - m) for pr in pairs}
                tot = None
                for pr in pairs:
                    tot = e[pr] if tot is None else tot + e[pr]
                inv_l.append(1.0 / jnp.sum(tot, axis=-1, keepdims=True))
                p_rows.append([e[pr].astype(_bf16) if pr in e else zero_blk for pr in range(n_pairs)])
            o = None
            for t in range(KEY_TILES):
                p_t = jnp.concatenate(
                    [jnp.concatenate([p_rows[i][2 * t], p_rows[i][2 * t + 1]], axis=1)
                     for i in range(ROWS_PER_TILE)], axis=0)
                part = jnp.dot(p_t, v_refs[t][0, :, vcols], preferred_element_type=_f32)
                o = part if o is None else o + part
            o = o * jnp.concatenate(inv_l, axis=0)
            acc = o if acc is None else jnp.where(head_of_lane == h4, o, acc)
        o_ref[0, :, vcols] = acc.astype(o_ref.dtype)


def _attn_kernel(q_ref, k0_ref, k1_ref, k2_ref, v0_ref, v1_ref, v2_ref, bias_ref, o_ref):
    j = pl.program_id(1)
    interior = jnp.logical_and(j > 0, j < pl.num_programs(1) - 1)
    args = (q_ref, (k0_ref, k1_ref, k2_ref), (v0_ref, v1_ref, v2_ref), bias_ref, o_ref)
    pl.when(interior)(functools.partial(_attn_heads, True, *args))
    pl.when(jnp.logical_not(interior))(functools.partial(_attn_heads, False, *args))


def _attention(qkv, bias):
    b, l, _ = qkv.shape
    nt = l // TILE_TOK
    assert nt >= KEY_TILES

    def base(j):
        return jnp.clip(j - 1, 0, nt - KEY_TILES)

    def pattern(j):
        return jnp.where(j == 0, 0, jnp.where(j == nt - 1, 2, 1))

    def kv_spec(part, t):
        return pl.BlockSpec((1, TILE_TOK, D_MODEL), lambda i, j: (i, base(j) + t, part))

    return pl.pallas_call(
        _attn_kernel,
        grid=(b, nt),
        in_specs=[pl.BlockSpec((1, TILE_TOK, D_MODEL), lambda i, j: (i, j, 0))]
        + [kv_spec(1, t) for t in range(KEY_TILES)]
        + [kv_spec(2, t) for t in range(KEY_TILES)]
        + [pl.BlockSpec((1, N_HEADS, ROWS_PER_TILE, KEY_TILES * 2, GRID_W, 2 * GRID_W),
                        lambda i, j: (pattern(j), 0, 0, 0, 0, 0))],
        out_specs=pl.BlockSpec((1, TILE_TOK, D_MODEL), lambda i, j: (i, j, 0)),
        out_shape=jax.ShapeDtypeStruct((b, l, D_MODEL), _bf16),
        compiler_params=_cparams(("parallel", "arbitrary")),
        name="nbr_attention",
    )(qkv, qkv, qkv, qkv, qkv, qkv, qkv, bias)


def _s5_tables(a_re, a_im, log_dt, b_re, b_im, c_re, c_im):
    f = lambda v: v.astype(_f32)
    pairs = (2, S5_PAIRS)
    a_row = [f(a).reshape(*pairs, 1, STATE_W) for a in (a_re, a_im)]
    a_col = [f(a).reshape(*pairs, STATE_W, 1) for a in (a_re, a_im)]
    ldt = jnp.repeat(f(log_dt), S5_STATE, axis=-1)
    ldt_row = ldt.reshape(*pairs, 1, STATE_W)
    ldt_col = ldt.reshape(*pairs, STATE_W, 1)
    b_t = [jnp.transpose(f(b).reshape(*pairs, 2, S5_STATE, S5_GROUP), (0, 1, 4, 2, 3))
           .reshape(*pairs, S5_GROUP, STATE_W) for b in (b_re, b_im)]
    c_t = [jnp.transpose(f(c), (0, 1, 3, 2)).reshape(*pairs, STATE_W, S5_GROUP) for c in (c_re, c_im)]

    def spec(shape):
        nd = len(shape)
        return pl.BlockSpec((2, 1) + tuple(shape[2:]), lambda p: (0, p) + (0,) * (nd - 2))

    ins = a_row + a_col + [ldt_row, ldt_col] + b_t + c_t
    out_shapes = (
        jax.ShapeDtypeStruct((S5_GROUPS, PAIR_W // 2, PAIR_W // 2), _bf16),
        jax.ShapeDtypeStruct((2, S5_PAIRS, PAIR_W, 2 * STATE_W), _bf16),
        jax.ShapeDtypeStruct((2, S5_PAIRS, 2 * STATE_W, PAIR_W), _bf16),
        jax.ShapeDtypeStruct((2, S5_PAIRS, S5_SEGS, 2 * STATE_W), _f32),
    )
    return pl.pallas_call(
        _s5_tables_kernel,
        grid=(S5_PAIRS,),
        in_specs=[spec(v.shape) for v in ins],
        out_specs=(pl.BlockSpec((2, PAIR_W // 2, PAIR_W // 2), lambda p: (p, 0, 0)),
                   spec(out_shapes[1].shape), spec(out_shapes[2].shape), spec(out_shapes[3].shape)),
        out_shape=out_shapes,
        compiler_params=_cparams(("parallel",)),
        name="s5_tables",
    )(*ins)


def _cexp(k, a_re, a_im, dt):
    mag = jnp.exp(k * (a_re * dt))
    ang = k * (a_im * dt)
    return mag * jnp.cos(ang), mag * jnp.sin(ang)


def _s5_tables_kernel(ar_re_ref, ar_im_ref, ac_re_ref, ac_im_ref, ldr_ref, ldc_ref,
                      bt_re_ref, bt_im_ref, ct_re_ref, ct_im_ref, m_ref, w_ref, v_ref, lam_ref):
    hi = lax.Precision.HIGHEST
    t = S5_CHUNK
    cw = t * S5_GROUP
    lane_c = lax.broadcasted_iota(jnp.int32, (1, cw), 1)
    log_group = S5_GROUP.bit_length() - 1
    lag = jnp.right_shift(lane_c, log_group).astype(_f32)
    tok_row = jnp.right_shift(lax.broadcasted_iota(jnp.int32, (cw, 1), 0), log_group).astype(_f32)
    low = lax.broadcasted_iota(jnp.int32, (1, STATE_W), 1) < S5_STATE
    tile_i = (lax.broadcasted_iota(jnp.int32, (S5_GROUP, cw), 0)
              == jnp.bitwise_and(lax.broadcasted_iota(jnp.int32, (S5_GROUP, cw), 1), S5_GROUP - 1)).astype(_f32)
    k_sub = jnp.minimum(lax.broadcasted_iota(jnp.int32, (NPOW, 1), 0), t).astype(_f32)
    k_lane = jnp.minimum(lax.broadcasted_iota(jnp.int32, (1, STATE_W), 1), t).astype(_f32)
    pow_of_row = lax.broadcasted_iota(jnp.int32, (cw, NPOW), 1).astype(_f32)
    pow_of_lane = lax.broadcasted_iota(jnp.int32, (STATE_W, cw), 0).astype(_f32)

    def spread(a, b, data_left=True):
        data, sel = (a, b) if data_left else (b, a)
        top = data.astype(_bf16)
        rest = data - top.astype(_f32)
        mid = rest.astype(_bf16)
        low = (rest - mid.astype(_f32)).astype(_bf16)
        sel = sel.astype(_bf16)
        parts = [jnp.dot(p, sel, preferred_element_type=_f32) if data_left
                 else jnp.dot(sel, p, preferred_element_type=_f32) for p in (top, mid, low)]
        return (parts[0] + parts[1]) + parts[2]

    g_lag = [[None, None], [None, None]]
    for d in range(2):
        a_re, a_im = ar_re_ref[d, 0], ar_im_ref[d, 0]
        pw_re, pw_im = _cexp(k_sub, a_re, a_im, jnp.exp(ldr_ref[d, 0]))
        lam_re, lam_im = pw_re[1:2], pw_im[1:2]
        den = a_re * a_re + a_im * a_im
        nr, ni = lam_re - 1.0, lam_im
        zr = (nr * a_re + ni * a_im) / den
        zi = (ni * a_re - nr * a_im) / den
        bb_re = zr * bt_re_ref[d, 0] - zi * bt_im_ref[d, 0]
        bb_im = zr * bt_im_ref[d, 0] + zi * bt_re_ref[d, 0]
        sel_tok = (pow_of_row == ((t - 1.0) - tok_row if d == 0 else tok_row)).astype(_f32)
        p_re = spread(sel_tok, pw_re, data_left=False)
        p_im = spread(sel_tok, pw_im, data_left=False)
        bt_re = jnp.concatenate([bb_re] * t, axis=0)
        bt_im = jnp.concatenate([bb_im] * t, axis=0)
        w_re, w_im = _cmul(p_re, p_im, bt_re, bt_im)
        top = jnp.concatenate([jnp.where(low, w_re, 0.0), jnp.where(low, w_im, 0.0)], axis=1)
        bot = jnp.concatenate([jnp.where(low, 0.0, w_re), jnp.where(low, 0.0, w_im)], axis=1)
        w_ref[d, 0] = jnp.concatenate([top, bot], axis=0).astype(w_ref.dtype)
        lam_ref[d, 0] = jnp.broadcast_to(jnp.concatenate([pw_re[t:t + 1], pw_im[t:t + 1]], axis=1),
                                         (S5_SEGS, 2 * STATE_W))
        pc_re, pc_im = _cexp(k_lane, ac_re_ref[d, 0], ac_im_ref[d, 0], jnp.exp(ldc_ref[d, 0]))
        ct_re = spread(ct_re_ref[d, 0], tile_i)
        ct_im = spread(ct_im_ref[d, 0], tile_i)

        def c_pow(k_of_block):
            sel = (pow_of_lane == k_of_block).astype(_f32)
            return _cmul(ct_re, ct_im, spread(pc_re, sel), spread(pc_im, sel))

        vb_re, vb_im = c_pow(lag + 1.0 if d == 0 else float(t) - lag)
        zero = jnp.zeros((S5_STATE, cw), _f32)
        v_ref[d, 0] = jnp.concatenate([
            jnp.concatenate([vb_re[:S5_STATE], zero], axis=1), jnp.concatenate([zero, vb_re[S5_STATE:]], axis=1),
            jnp.concatenate([-vb_im[:S5_STATE], zero], axis=1), jnp.concatenate([zero, -vb_im[S5_STATE:]], axis=1),
        ], axis=0).astype(v_ref.dtype)
        ce_re, ce_im = c_pow(lag if d == 0 else (t - 1.0) - lag)
        for g2 in range(2):
            sel = low if g2 == 0 else ~low
            g_lag[d][g2] = (
                jnp.dot(jnp.where(sel, bb_re, 0.0), ce_re, preferred_element_type=_f32, precision=hi)
                - jnp.dot(jnp.where(sel, bb_im, 0.0), ce_im, preferred_element_type=_f32, precision=hi))
    for g2 in range(2):
        rows = []
        for s in range(t):
            fwd = g_lag[0][g2] if s == 0 else pltpu.roll(g_lag[0][g2], S5_GROUP * s, 1)
            fwd = jnp.where(lane_c >= S5_GROUP * s, fwd, 0.0)
            shift = (cw - S5_GROUP * (t - 1 - s)) % cw
            bwd = g_lag[1][g2] if shift == 0 else pltpu.roll(g_lag[1][g2], shift, 1)
            bwd = jnp.where(lane_c < S5_GROUP * (s + 1), bwd, 0.0)
            rows.append(fwd + bwd)
        m_ref[g2] = jnp.concatenate(rows, axis=0).astype(m_ref.dtype)


def _cmul(ar, ai, br, bi):
    return ar * br - ai * bi, ar * bi + ai * br


def _s5_kernel(chained, u_ref, m_ref, w_ref, v_ref, lam_ref, y_ref, x_scr, h_scr):
    rows = u_ref.shape[1]
    nc = rows // S5_SEGS
    rb = min(512, rows)
    sw = STATE_W

    for r in range(rows // rb):
        rs = slice(r * rb, (r + 1) * rb)
        u = u_ref[0, rs, :]
        for d in range(2):
            x_scr[d, rs, :] = jnp.dot(u, w_ref[d, 0], preferred_element_type=_f32)

    lam = [(lam_ref[d, 0, :, :sw], lam_ref[d, 0, :, sw:]) for d in range(2)]

    def row_of(d, c):
        cc = c if d == 0 else nc - 1 - c
        return pl.ds(pl.multiple_of(cc * S5_SEGS, S5_SEGS), S5_SEGS)

    def scan(init, store):
        def body(c, carry):
            new = []
            for d in range(2):
                hr, hi = carry[d]
                rs = row_of(d, c)
                if store:
                    h_scr[d, rs, :sw] = hr
                    h_scr[d, rs, sw:] = hi
                pr, pi = _cmul(lam[d][0], lam[d][1], hr, hi)
                new.append((pr + x_scr[d, rs, :sw], pi + x_scr[d, rs, sw:]))
            return tuple(new)

        return lax.fori_loop(0, nc, body, init, unroll=4)

    zero = jnp.zeros((S5_SEGS, sw), _f32)
    if not chained:
        scan(((zero, zero), (zero, zero)), store=True)
    else:
        ends = scan(((zero, zero), (zero, zero)), store=False)
        one = jnp.ones((S5_SEGS, sw), _f32)
        h_in = []
        for d in range(2):
            er, ei = ends[d]
            dr, di = one[0:1], zero[0:1]
            br, bi = lam[d][0][0:1], lam[d][1][0:1]
            n = nc
            while n:
                if n & 1:
                    dr, di = _cmul(dr, di, br, bi)
                n >>= 1
                if n:
                    br, bi = _cmul(br, bi, br, bi)
            order = range(S5_SEGS) if d == 0 else range(S5_SEGS - 1, -1, -1)
            cr = jnp.zeros((1, sw), _f32)
            ci = jnp.zeros((1, sw), _f32)
            rows_r = [None] * S5_SEGS
            rows_i = [None] * S5_SEGS
            for sgm in order:
                rows_r[sgm], rows_i[sgm] = cr, ci
                pr, pi = _cmul(dr, di, cr, ci)
                cr, ci = pr + er[sgm:sgm + 1], pi + ei[sgm:sgm + 1]
            h_in.append((jnp.concatenate(rows_r, axis=0), jnp.concatenate(rows_i, axis=0)))
        scan(tuple(h_in), store=True)

    half = PAIR_W // 2

    for r in range(rows // rb):
        rs = slice(r * rb, (r + 1) * rb)
        u = u_ref[0, rs, :]
        y = (jnp.dot(h_scr[0, rs, :].astype(_bf16), v_ref[0, 0], preferred_element_type=_f32)
             + jnp.dot(h_scr[1, rs, :].astype(_bf16), v_ref[1, 0], preferred_element_type=_f32))
        y0 = y[:, :half] + jnp.dot(u[:, :half], m_ref[0], preferred_element_type=_f32)
        y1 = y[:, half:] + jnp.dot(u[:, half:], m_ref[1], preferred_element_type=_f32)
        y_ref[0, rs, :half] = y0.astype(y_ref.dtype)
        y_ref[0, rs, half:] = y1.astype(y_ref.dtype)


def _s5_core(u_t, tables, chained):
    m_sum, w_in, v_out, lam_t = tables
    _, rows, _ = u_t.shape
    return pl.pallas_call(
        functools.partial(_s5_kernel, chained),
        grid=(S5_PAIRS,),
        in_specs=[
            pl.BlockSpec((1, rows, PAIR_W), lambda p: (p, 0, 0)),
            pl.BlockSpec((2, PAIR_W // 2, PAIR_W // 2), lambda p: (p, 0, 0)),
            pl.BlockSpec((2, 1, PAIR_W, 2 * STATE_W), lambda p: (0, p, 0, 0)),
            pl.BlockSpec((2, 1, 2 * STATE_W, PAIR_W), lambda p: (0, p, 0, 0)),
            pl.BlockSpec((2, 1, S5_SEGS, 2 * STATE_W), lambda p: (0, p, 0, 0)),
        ],
        out_specs=pl.BlockSpec((1, rows, PAIR_W), lambda p: (p, 0, 0)),
        out_shape=jax.ShapeDtypeStruct((S5_PAIRS, rows, PAIR_W), _bf16),
        scratch_shapes=[pltpu.VMEM((2, rows, 2 * STATE_W), _f32),
                        pltpu.VMEM((2, rows, 2 * STATE_W), _f32)],
        compiler_params=_cparams(("parallel",)),
        name="s5_core",
    )(u_t, m_sum, w_in, v_out, lam_t)


def _gelu_tanh(x):
    return 0.5 * x * (1.0 + jnp.tanh(math.sqrt(2.0 / math.pi) * (x + 0.044715 * (x * x * x))))


def _post_kernel(mixer, *refs):
    if mixer == "attn":
        (x_ref, mix_ref, mod_ref, gain_ref, modn_ref, gainn_ref, perm_ref, wp_ref, w1_ref, w2_ref, o_ref, u_ref,
         x1_scr, h_scr, acc_ref, ubuf) = refs
    else:
        (x_ref, mix_ref, mod_ref, gain_ref, skip_ref, pt_ref, wp_ref, w1_ref, w2_ref, o_ref,
         x1_scr, h_scr, acc_ref, nat_scr) = refs
    tmc = x_ref.shape[1]

    def rows(seg):
        return slice(seg * tmc, (seg + 1) * tmc)

    if mixer == "attn":
        for seg in range(S5_SEGS):
            h_scr[rows(seg), :] = mix_ref[seg]
        y = jnp.dot(h_scr[...], wp_ref[...], preferred_element_type=_f32)
    else:
        nck = tmc // S5_CHUNK
        blk = nck * S5_SEGS
        a = jnp.concatenate(
            [jnp.concatenate([mix_ref[(8 * lt + gl) // 2, :, _chunk_col(8 * lt + gl, h):_chunk_col(8 * lt + gl, h) + 128]
                              for gl in range(8)], axis=1)
             for lt in range(LANE_TILES) for h in range(2)], axis=0)
        for sp in range(4):
            out = jnp.dot(a, pt_ref[:, sp * 256:(sp + 1) * 256], preferred_element_type=_f32)
            for lt in range(LANE_TILES):
                for h in range(2):
                    for s in (2 * sp, 2 * sp + 1):
                        for c in range(nck):
                            tok = c * S5_CHUNK + 8 * h + s
                            row = (lt * 2 + h) * blk + c * S5_SEGS
                            nat_scr[lt, tok * S5_SEGS:(tok + 1) * S5_SEGS, :] = (
                                out[row:row + S5_SEGS, (s % 2) * 128:(s % 2 + 1) * 128])
        for seg in range(S5_SEGS):
            mod = mod_ref[seg]
            u = _norm_mod(x_ref[seg], gain_ref[0:1, :], mod[1:2, :], mod[0:1, :])
            y_ssm = jnp.concatenate([nat_scr[lt, pl.ds(seg, tmc, stride=S5_SEGS), :]
                                     for lt in range(D_MODEL // 128)], axis=1)
            h_scr[rows(seg), :] = _gelu_tanh(skip_ref[...] * u + y_ssm).astype(_bf16)
        g = h_scr[...]
        val = jnp.dot(g, wp_ref[:, :D_MODEL], preferred_element_type=_f32)
        gate = jnp.dot(g, wp_ref[:, D_MODEL:], preferred_element_type=_f32)
        y = val * jax.nn.sigmoid(gate)
    for seg in range(S5_SEGS):
        mod = mod_ref[seg]
        x1 = x_ref[seg] + mod[2:3, :] * _rms(y[rows(seg), :], gain_ref[1:2, :])
        x1_scr[rows(seg), :] = x1
        h_scr[rows(seg), :] = _norm_mod(x1, gain_ref[2:3, :], mod[4:5, :], mod[3:4, :]).astype(_bf16)
    h2 = h_scr[...]
    tf = 512
    for kf in range(D_FF // tf):
        cols = slice(kf * tf, (kf + 1) * tf)
        a = jnp.dot(h2, w1_ref[:, cols], preferred_element_type=_f32)
        a = jnp.square(jnp.maximum(a, 0.0)).astype(_bf16)
        part = jnp.dot(a, w2_ref[cols, :], preferred_element_type=_f32)
        if kf == 0:
            acc_ref[...] = part
        else:
            acc_ref[...] += part
    for seg in range(S5_SEGS):
        mod = mod_ref[seg]
        out = x1_scr[rows(seg), :] + mod[5:6, :] * _rms(acc_ref[rows(seg), :], gain_ref[3:4, :])
        o_ref[seg] = out
        if mixer == "attn":
            modn = modn_ref[seg]
            _chunk_stage(_norm_mod(out, gainn_ref[0:1, :], modn[1:2, :], modn[0:1, :]), seg, ubuf)
    if mixer == "attn":
        _chunk_emit(ubuf, perm_ref, u_ref)


def _post_mixer(mixer, x8, mix, mod8, gains, w_proj, w1, w2, skip=None, perm=None, next_mod8=None, next_gains=None,
                tmc=64):
    _, ls, _ = x8.shape
    tok = lambda j: (0, j, 0)
    const2 = lambda j: (0, 0)
    m = S5_SEGS * tmc
    chunk_rows = tmc // S5_CHUNK * S5_SEGS
    mod_spec = _resident((S5_SEGS, 6, D_MODEL), lambda j: (0, 0, 0))
    in_specs = [pl.BlockSpec((S5_SEGS, tmc, D_MODEL), tok)]
    scratch = [pltpu.VMEM((m, D_MODEL), _f32), pltpu.VMEM((m, D_MODEL), _bf16), pltpu.VMEM((m, D_MODEL), _f32)]
    out_specs = pl.BlockSpec((S5_SEGS, tmc, D_MODEL), tok)
    out_shape = jax.ShapeDtypeStruct(x8.shape, _f32)
    if mixer == "attn":
        in_specs.append(pl.BlockSpec((S5_SEGS, tmc, D_MODEL), tok))
        scratch.append(pltpu.VMEM((LANE_TILES, m, 128), _f32))
        out_specs = (out_specs, pl.BlockSpec((S5_PAIRS, chunk_rows, PAIR_W), tok))
        out_shape = (out_shape, jax.ShapeDtypeStruct((S5_PAIRS, ls // S5_CHUNK * S5_SEGS, PAIR_W), _bf16))
    else:
        in_specs.append(pl.BlockSpec((S5_PAIRS, chunk_rows, PAIR_W), tok))
        scratch.append(pltpu.VMEM((LANE_TILES, m, 128), _f32))
    in_specs += [mod_spec, _resident((4, D_MODEL), const2)]
    args = [x8, mix, mod8, gains]
    if mixer == "attn":
        in_specs += [mod_spec, _resident((4, D_MODEL), const2), _resident((D_MODEL, D_MODEL), const2)]
        args += [next_mod8, next_gains, perm]
    else:
        in_specs += [_resident((1, D_MODEL), const2), _resident((D_MODEL, D_MODEL), const2)]
        args += [skip, perm]
    in_specs += [_resident(w_proj.shape, const2), _resident(w1.shape, const2), _resident(w2.shape, const2)]
    args += [w_proj, w1, w2]
    return pl.pallas_call(
        functools.partial(_post_kernel, mixer),
        grid=(ls // tmc,),
        in_specs=in_specs,
        out_specs=out_specs,
        out_shape=out_shape,
        scratch_shapes=scratch,
        compiler_params=_cparams(("parallel",)),
        name="post_" + mixer,
    )(*args)


def _trunk(x, mod, params, chained):
    b, l, _ = x.shape
    assert (b == 1) if chained else (b == S5_SEGS)
    ls = b * l // S5_SEGS
    gains = params["norm_gain"]
    mod8 = jnp.broadcast_to(mod, (DEPTH, S5_SEGS, 6, D_MODEL))
    qkv = _qkv_proj(x, mod[0], gains[0], params["w_qkv"])
    att = _attention(qkv, params["attn_bias"])
    x8, u = _post_mixer("attn", x.reshape(S5_SEGS, ls, D_MODEL), att.reshape(S5_SEGS, ls, D_MODEL), mod8[0],
                        gains[0], params["w_o"], params["ffn_w1"][0], params["ffn_w2"][0],
                        perm=_lane_perm(), next_mod8=mod8[1], next_gains=gains[1])
    y = _s5_core(u, params["s5_tables"], chained)
    x8 = _post_mixer("s5", x8, y, mod8[1], gains[1], params["w_glu"], params["ffn_w1"][1], params["ffn_w2"][1],
                     skip=params["s5_d"], perm=_lane_perm(inverse=True))
    return x8.reshape(b, l, D_MODEL)


def kernel(x_prompt, x_sample, c_prompt, c_sample, ada_w, ada_b, norm_gain, attn_w_qkv, attn_w_o, attn_rpb,
           s5_a_re, s5_a_im, s5_log_dt, s5_b_re, s5_b_im, s5_c_re, s5_c_im, s5_d, s5_w_glu, ffn_w1, ffn_w2):
    nbp, nbs = c_prompt.shape[0], c_sample.shape[0]
    nb = -(-(nbp + nbs) // 8) * 8
    c_all = jnp.concatenate([c_prompt, c_sample, jnp.zeros((nb - nbp - nbs, D_MODEL), _f32)], axis=0)
    mod = _modulation(c_all, ada_w, ada_b)
    mod_p = mod[:, :nbp].reshape(DEPTH, nbp, 6, D_MODEL)
    mod_s = mod[:, nbp:nbp + nbs].reshape(DEPTH, nbs, 6, D_MODEL)

    params = {
        "norm_gain": norm_gain,
        "w_qkv": attn_w_qkv[0].astype(_bf16),
        "w_o": attn_w_o[0].astype(_bf16),
        "attn_bias": _attn_bias_table(attn_rpb[0]),
        "s5_tables": _s5_tables(s5_a_re[0], s5_a_im[0], s5_log_dt[0], s5_b_re[0], s5_b_im[0],
                                s5_c_re[0], s5_c_im[0]),
        "s5_d": s5_d[0].reshape(1, D_MODEL),
        "w_glu": s5_w_glu[0].astype(_bf16),
        "ffn_w1": ffn_w1.astype(_bf16),
        "ffn_w2": ffn_w2.astype(_bf16),
    }
    y_prompt = _trunk(x_prompt, mod_p, params, chained=False)
    y_sample = _trunk(x_sample, mod_s, params, chained=True)
    return (y_prompt, y_sample)
```

```python
import functools
import math

import numpy as np
import jax
import jax.numpy as jnp
from jax import lax
from jax.experimental import pallas as pl
from jax.experimental.pallas import tpu as pltpu

D_MODEL = 1024
D_FF = 4 * D_MODEL
DEPTH = 2
EPS = 1e-6
NEG_INF = -1e30
LOG2E = math.log2(math.e)

GRID_W = 64
N_HEADS = 16
HEAD_DIM = D_MODEL // N_HEADS
WIN_H = 8
WIN_W = 16
ROWS_PER_TILE = 4
TILE_TOK = ROWS_PER_TILE * GRID_W
KEY_TILES = 3
S5_GROUP = 16
S5_GROUPS = D_MODEL // S5_GROUP
S5_STATE = 64
S5_CHUNK = 16
S5_SEGS = 8
S5_PAIRS = S5_GROUPS // 2
PAIR_W = 2 * S5_CHUNK * S5_GROUP
STATE_W = 2 * S5_STATE
NPOW = 32

VMEM_LIMIT = 56 * 1024 * 1024

_f32 = jnp.float32
_bf16 = jnp.bfloat16


def _cparams(sem):
    return pltpu.CompilerParams(dimension_semantics=sem, vmem_limit_bytes=VMEM_LIMIT)


def _resident(shape, index_map):
    return pl.BlockSpec(shape, index_map, pipeline_mode=pl.Buffered(1))


def _rms(x, gain):
    ms = jnp.mean(x * x, axis=-1, keepdims=True)
    return x * lax.rsqrt(ms + EPS) * gain


def _norm_mod(x, gain, scale, shift):
    return _rms(x, gain * (1.0 + scale)) + shift


def _mod_kernel(c_ref, w_ref, b_ref, o_ref):
    c = c_ref[...]
    act = c * jax.nn.sigmoid(c)
    o_ref[0] = jnp.dot(act, w_ref[0], preferred_element_type=_f32,
                       precision=lax.Precision.HIGHEST) + b_ref[0]


def _modulation(c_all, ada_w, ada_b):
    nb = c_all.shape[0]
    tn = 1536
    return pl.pallas_call(
        _mod_kernel,
        grid=(DEPTH, 6 * D_MODEL // tn),
        in_specs=[
            pl.BlockSpec((nb, D_MODEL), lambda i, n: (0, 0)),
            pl.BlockSpec((1, D_MODEL, tn), lambda i, n: (i, 0, n)),
            pl.BlockSpec((1, 1, tn), lambda i, n: (i, 0, n)),
        ],
        out_specs=pl.BlockSpec((1, nb, tn), lambda i, n: (i, 0, n)),
        out_shape=jax.ShapeDtypeStruct((DEPTH, nb, 6 * D_MODEL), _f32),
        compiler_params=_cparams(("arbitrary", "arbitrary")),
        name="adaln_mod",
    )(c_all, ada_w, ada_b.reshape(DEPTH, 1, 6 * D_MODEL))


def _qkv_kernel(x_ref, mod_ref, gain_ref, w_ref, o_ref):
    mod = mod_ref[0]
    h = _norm_mod(x_ref[0], gain_ref[0:1, :], mod[1:2, :], mod[0:1, :]).astype(_bf16)
    q_scale = HEAD_DIM ** -0.5 * LOG2E
    for part in range(3):
        cols = slice(part * D_MODEL, (part + 1) * D_MODEL)
        y = jnp.dot(h, w_ref[:, cols], preferred_element_type=_f32)
        if part == 0:
            y = y * q_scale
        o_ref[0, :, cols] = y.astype(_bf16)


def _qkv_proj(x, mod, gains, w_qkv, tm=1024):
    b, l, _ = x.shape
    return pl.pallas_call(
        _qkv_kernel,
        grid=(b, l // tm),
        in_specs=[
            pl.BlockSpec((1, tm, D_MODEL), lambda i, j: (i, j, 0)),
            pl.BlockSpec((1, 6, D_MODEL), lambda i, j: (i, 0, 0)),
            _resident((4, D_MODEL), lambda i, j: (0, 0)),
            _resident((D_MODEL, 3 * D_MODEL), lambda i, j: (0, 0)),
        ],
        out_specs=pl.BlockSpec((1, tm, 3 * D_MODEL), lambda i, j: (i, j, 0)),
        out_shape=jax.ShapeDtypeStruct((b, l, 3 * D_MODEL), _bf16),
        compiler_params=_cparams(("parallel", "parallel")),
        name="norm_qkv",
    )(x, mod, gains, w_qkv)


LANE_TILES = D_MODEL // 128


def _lane_perm(inverse=False):
    eye = np.eye
    p = np.einsum('sS,gG,iI->sgiGSI', eye(8), eye(8), eye(S5_GROUP)).reshape(D_MODEL, D_MODEL)
    return jnp.asarray(p.T if inverse else p, _bf16)


def _chunk_col(g, h):
    return (g % 2) * (PAIR_W // 2) + h * 128


def _chunk_stage(u, seg, ubuf):
    tmc = u.shape[0]
    for lt in range(LANE_TILES):
        ubuf[lt, seg * tmc:(seg + 1) * tmc, :] = u[:, lt * 128:(lt + 1) * 128]


def _chunk_emit(ubuf, p_ref, o_ref):
    tmc = ubuf.shape[1] // S5_SEGS
    nck = tmc // S5_CHUNK
    blk = nck * S5_SEGS
    a = jnp.concatenate(
        [jnp.concatenate([ubuf[lt, pl.ds(c * S5_CHUNK + 8 * h + s, S5_SEGS, stride=tmc), :]
                          for lt in range(LANE_TILES) for h in range(2) for c in range(nck)], axis=0)
         for s in range(8)], axis=1).astype(_bf16)
    for gp in range(4):
        out = jnp.dot(a, p_ref[:, gp * 256:(gp + 1) * 256], preferred_element_type=_f32)
        for lt in range(LANE_TILES):
            for h in range(2):
                for gl in (2 * gp, 2 * gp + 1):
                    g = 8 * lt + gl
                    col = _chunk_col(g, h)
                    piece = out[(lt * 2 + h) * blk:(lt * 2 + h + 1) * blk, (gl % 2) * 128:(gl % 2 + 1) * 128]
                    o_ref[g // 2, :, col:col + 128] = piece.astype(o_ref.dtype)


N_REL_ROWS = 2 * WIN_H - 1


def _bias_rows():
    i = np.arange(ROWS_PER_TILE)[:, None]
    kk = np.arange(KEY_TILES * ROWS_PER_TILE)[None, :]
    dr = np.stack([kk - i + 7, kk - i + 3, kk - i - 1])
    row_ok = np.stack([(kk < WIN_H) & (i >= 0), (kk >= i) & (kk < i + WIN_H), (kk >= 4) & (i >= 0)])
    return np.where(row_ok, dr, N_REL_ROWS)


def _bias_kernel(rp_ref, o_ref):
    dr_tab = _bias_rows()
    pair_w = 2 * GRID_W
    qc = lax.broadcasted_iota(jnp.int32, (GRID_W, pair_w), 0)
    lane = lax.broadcasted_iota(jnp.int32, (GRID_W, pair_w), 1)
    kc = jnp.bitwise_and(lane, GRID_W - 1)
    col_start = jnp.clip(qc - WIN_W // 2, 0, GRID_W - WIN_W)
    col_ok = jnp.logical_and(kc >= col_start, kc < col_start + WIN_W)
    left_half = lane < GRID_W
    shift = pair_w - (WIN_W - 1)
    left, right = [], []
    for dr in range(N_REL_ROWS):
        base = jnp.broadcast_to(rp_ref[0, dr:dr + 1, :], (GRID_W, pair_w))
        left.append(pltpu.roll(base, shift, 1, stride=1, stride_axis=0))
        right.append(pltpu.roll(base, (shift + GRID_W) % pair_w, 1, stride=1, stride_axis=0))
    masked = jnp.full((GRID_W, pair_w), NEG_INF, _f32)
    left.append(masked)
    right.append(masked)
    for pt in range(dr_tab.shape[0]):
        for i in range(ROWS_PER_TILE):
            for pr in range(KEY_TILES * 2):
                blk = jnp.where(left_half, left[dr_tab[pt, i, 2 * pr]], right[dr_tab[pt, i, 2 * pr + 1]])
                o_ref[pt, 0, i, pr] = jnp.where(col_ok, blk, NEG_INF).astype(o_ref.dtype)


def _attn_bias_table(rpb):
    rel_rows_pad = -(-N_REL_ROWS // 8) * 8
    rp = jnp.pad(rpb.astype(_f32) * LOG2E,
                 ((0, 0), (0, rel_rows_pad - N_REL_ROWS), (0, 2 * GRID_W - (2 * WIN_W - 1))))
    shape = (3, N_HEADS, ROWS_PER_TILE, KEY_TILES * 2, GRID_W, 2 * GRID_W)
    return pl.pallas_call(
        _bias_kernel,
        grid=(N_HEADS,),
        in_specs=[pl.BlockSpec((1, rel_rows_pad, 2 * GRID_W), lambda h: (h, 0, 0))],
        out_specs=pl.BlockSpec((3, 1) + shape[2:], lambda h: (0, h, 0, 0, 0, 0)),
        out_shape=jax.ShapeDtypeStruct(shape, _bf16),
        compiler_params=_cparams(("parallel",)),
        name="attn_bias",
    )(rp)


def _attn_heads(interior, q_ref, k_refs, v_refs, bias_ref, o_ref):
    lane = lax.broadcasted_iota(jnp.int32, (1, 128), 1)
    low = lane < HEAD_DIM
    head_mask = [low.astype(_bf16), (~low).astype(_bf16)]
    head_of_lane = jnp.right_shift(lax.broadcasted_iota(jnp.int32, (1, 256), 1), HEAD_DIM.bit_length() - 1)
    n_pairs = 2 * KEY_TILES
    zero_blk = jnp.zeros((GRID_W, 2 * GRID_W), _bf16)
    for hq in range(N_HEADS // 4):
        vcols = slice(hq * 256, (hq + 1) * 256)
        acc = None
        for h4 in range(4):
            head = 4 * hq + h4
            cols = slice((head // 2) * 128, (head // 2 + 1) * 128)
            qm = q_ref[0, :, cols] * head_mask[head % 2]
            s = [lax.dot_general(qm, k_refs[t][0, :, cols], (((1,), (1,)), ((), ())),
                                 preferred_element_type=_f32) for t in range(KEY_TILES)]
            p_rows, inv_l = [], []
            for i in range(ROWS_PER_TILE):
                pairs = range(i // 2, (i + WIN_H - 1) // 2 + 1) if interior else range(n_pairs)
                rows = slice(i * GRID_W, (i + 1) * GRID_W)
                blk = {pr: s[pr // 2][rows, (pr % 2) * 128:(pr % 2 + 1) * 128]
                       + bias_ref[0, head, i, pr].astype(_f32) for pr in pairs}
                m = None
                for pr in pairs:
                    m = blk[pr] if m is None else jnp.maximum(m, blk[pr])
                m = jnp.max(m, axis=-1, keepdims=True)
                e = {pr: jnp.exp2(blk[pr] - m) for pr in pairs}
                tot = None
                for pr in pairs:
                    tot = e[pr] if tot is None else tot + e[pr]
                inv_l.append(1.0 / jnp.sum(tot, axis=-1, keepdims=True))
                p_rows.append([e[pr].astype(_bf16) if pr in e else zero_blk for pr in range(n_pairs)])
            o = None
            for t in range(KEY_TILES):
                p_t = jnp.concatenate(
                    [jnp.concatenate([p_rows[i][2 * t], p_rows[i][2 * t + 1]], axis=1)
                     for i in range(ROWS_PER_TILE)], axis=0)
                part = jnp.dot(p_t, v_refs[t][0, :, vcols], preferred_element_type=_f32)
                o = part if o is None else o + part
            o = o * jnp.concatenate(inv_l, axis=0)
            acc = o if acc is None else jnp.where(head_of_lane == h4, o, acc)
        o_ref[0, :, vcols] = acc.astype(o_ref.dtype)


def _attn_kernel(q_ref, k0_ref, k1_ref, k2_ref, v0_ref, v1_ref, v2_ref, bias_ref, o_ref):
    j = pl.program_id(1)
    interior = jnp.logical_and(j > 0, j < pl.num_programs(1) - 1)
    args = (q_ref, (k0_ref, k1_ref, k2_ref), (v0_ref, v1_ref, v2_ref), bias_ref, o_ref)
    pl.when(interior)(functools.partial(_attn_heads, True, *args))
    pl.when(jnp.logical_not(interior))(functools.partial(_attn_heads, False, *args))


def _attention(qkv, bias):
    b, l, _ = qkv.shape
    nt = l // TILE_TOK
    assert nt >= KEY_TILES

    def base(j):
        return jnp.clip(j - 1, 0, nt - KEY_TILES)

    def pattern(j):
        return jnp.where(j == 0, 0, jnp.where(j == nt - 1, 2, 1))

    def kv_spec(part, t):
        return pl.BlockSpec((1, TILE_TOK, D_MODEL), lambda i, j: (i, base(j) + t, part))

    return pl.pallas_call(
        _attn_kernel,
        grid=(b, nt),
        in_specs=[pl.BlockSpec((1, TILE_TOK, D_MODEL), lambda i, j: (i, j, 0))]
        + [kv_spec(1, t) for t in range(KEY_TILES)]
        + [kv_spec(2, t) for t in range(KEY_TILES)]
        + [pl.BlockSpec((1, N_HEADS, ROWS_PER_TILE, KEY_TILES * 2, GRID_W, 2 * GRID_W),
                        lambda i, j: (pattern(j), 0, 0, 0, 0, 0))],
        out_specs=pl.BlockSpec((1, TILE_TOK, D_MODEL), lambda i, j: (i, j, 0)),
        out_shape=jax.ShapeDtypeStruct((b, l, D_MODEL), _bf16),
        compiler_params=_cparams(("parallel", "arbitrary")),
        name="nbr_attention",
    )(qkv, qkv, qkv, qkv, qkv, qkv, qkv, bias)


def _s5_tables(a_re, a_im, log_dt, b_re, b_im, c_re, c_im):
    f = lambda v: v.astype(_f32)
    pairs = (2, S5_PAIRS)
    a_row = [f(a).reshape(*pairs, 1, STATE_W) for a in (a_re, a_im)]
    a_col = [f(a).reshape(*pairs, STATE_W, 1) for a in (a_re, a_im)]
    ldt = jnp.repeat(f(log_dt), S5_STATE, axis=-1)
    ldt_row = ldt.reshape(*pairs, 1, STATE_W)
    ldt_col = ldt.reshape(*pairs, STATE_W, 1)
    b_t = [jnp.transpose(f(b).reshape(*pairs, 2, S5_STATE, S5_GROUP), (0, 1, 4, 2, 3))
           .reshape(*pairs, S5_GROUP, STATE_W) for b in (b_re, b_im)]
    c_t = [jnp.transpose(f(c), (0, 1, 3, 2)).reshape(*pairs, STATE_W, S5_GROUP) for c in (c_re, c_im)]

    def spec(shape):
        nd = len(shape)
        return pl.BlockSpec((2, 1) + tuple(shape[2:]), lambda p: (0, p) + (0,) * (nd - 2))

    ins = a_row + a_col + [ldt_row, ldt_col] + b_t + c_t
    out_shapes = (
        jax.ShapeDtypeStruct((S5_GROUPS, PAIR_W // 2, PAIR_W // 2), _bf16),
        jax.ShapeDtypeStruct((2, S5_PAIRS, PAIR_W, 2 * STATE_W), _bf16),
        jax.ShapeDtypeStruct((2, S5_PAIRS, 2 * STATE_W, PAIR_W), _bf16),
        jax.ShapeDtypeStruct((2, S5_PAIRS, S5_SEGS, 2 * STATE_W), _f32),
    )
    return pl.pallas_call(
        _s5_tables_kernel,
        grid=(S5_PAIRS,),
        in_specs=[spec(v.shape) for v in ins],
        out_specs=(pl.BlockSpec((2, PAIR_W // 2, PAIR_W // 2), lambda p: (p, 0, 0)),
                   spec(out_shapes[1].shape), spec(out_shapes[2].shape), spec(out_shapes[3].shape)),
        out_shape=out_shapes,
        compiler_params=_cparams(("parallel",)),
        name="s5_tables",
    )(*ins)


def _cexp(k, a_re, a_im, dt):
    mag = jnp.exp(k * (a_re * dt))
    ang = k * (a_im * dt)
    return mag * jnp.cos(ang), mag * jnp.sin(ang)


def _s5_tables_kernel(ar_re_ref, ar_im_ref, ac_re_ref, ac_im_ref, ldr_ref, ldc_ref,
                      bt_re_ref, bt_im_ref, ct_re_ref, ct_im_ref, m_ref, w_ref, v_ref, lam_ref):
    hi = lax.Precision.HIGHEST
    t = S5_CHUNK
    cw = t * S5_GROUP
    lane_c = lax.broadcasted_iota(jnp.int32, (1, cw), 1)
    log_group = S5_GROUP.bit_length() - 1
    lag = jnp.right_shift(lane_c, log_group).astype(_f32)
    tok_row = jnp.right_shift(lax.broadcasted_iota(jnp.int32, (cw, 1), 0), log_group).astype(_f32)
    low = lax.broadcasted_iota(jnp.int32, (1, STATE_W), 1) < S5_STATE
    tile_i = (lax.broadcasted_iota(jnp.int32, (S5_GROUP, cw), 0)
              == jnp.bitwise_and(lax.broadcasted_iota(jnp.int32, (S5_GROUP, cw), 1), S5_GROUP - 1)).astype(_f32)
    k_sub = jnp.minimum(lax.broadcasted_iota(jnp.int32, (NPOW, 1), 0), t).astype(_f32)
    k_lane = jnp.minimum(lax.broadcasted_iota(jnp.int32, (1, STATE_W), 1), t).astype(_f32)
    pow_of_row = lax.broadcasted_iota(jnp.int32, (cw, NPOW), 1).astype(_f32)
    pow_of_lane = lax.broadcasted_iota(jnp.int32, (STATE_W, cw), 0).astype(_f32)

    def spread(a, b, data_left=True):
        data, sel = (a, b) if data_left else (b, a)
        top = data.astype(_bf16)
        rest = data - top.astype(_f32)
        mid = rest.astype(_bf16)
        low = (rest - mid.astype(_f32)).astype(_bf16)
        sel = sel.astype(_bf16)
        parts = [jnp.dot(p, sel, preferred_element_type=_f32) if data_left
                 else jnp.dot(sel, p, preferred_element_type=_f32) for p in (top, mid, low)]
        return (parts[0] + parts[1]) + parts[2]

    g_lag = [[None, None], [None, None]]
    for d in range(2):
        a_re, a_im = ar_re_ref[d, 0], ar_im_ref[d, 0]
        pw_re, pw_im = _cexp(k_sub, a_re, a_im, jnp.exp(ldr_ref[d, 0]))
        lam_re, lam_im = pw_re[1:2], pw_im[1:2]
        den = a_re * a_re + a_im * a_im
        nr, ni = lam_re - 1.0, lam_im
        zr = (nr * a_re + ni * a_im) / den
        zi = (ni * a_re - nr * a_im) / den
        bb_re = zr * bt_re_ref[d, 0] - zi * bt_im_ref[d, 0]
        bb_im = zr * bt_im_ref[d, 0] + zi * bt_re_ref[d, 0]
        sel_tok = (pow_of_row == ((t - 1.0) - tok_row if d == 0 else tok_row)).astype(_f32)
        p_re = spread(sel_tok, pw_re, data_left=False)
        p_im = spread(sel_tok, pw_im, data_left=False)
        bt_re = jnp.concatenate([bb_re] * t, axis=0)
        bt_im = jnp.concatenate([bb_im] * t, axis=0)
        w_re, w_im = _cmul(p_re, p_im, bt_re, bt_im)
        top = jnp.concatenate([jnp.where(low, w_re, 0.0), jnp.where(low, w_im, 0.0)], axis=1)
        bot = jnp.concatenate([jnp.where(low, 0.0, w_re), jnp.where(low, 0.0, w_im)], axis=1)
        w_ref[d, 0] = jnp.concatenate([top, bot], axis=0).astype(w_ref.dtype)
        lam_ref[d, 0] = jnp.broadcast_to(jnp.concatenate([pw_re[t:t + 1], pw_im[t:t + 1]], axis=1),
                                         (S5_SEGS, 2 * STATE_W))
        pc_re, pc_im = _cexp(k_lane, ac_re_ref[d, 0], ac_im_ref[d, 0], jnp.exp(ldc_ref[d, 0]))
        ct_re = spread(ct_re_ref[d, 0], tile_i)
        ct_im = spread(ct_im_ref[d, 0], tile_i)

        def c_pow(k_of_block):
            sel = (pow_of_lane == k_of_block).astype(_f32)
            return _cmul(ct_re, ct_im, spread(pc_re, sel), spread(pc_im, sel))

        vb_re, vb_im = c_pow(lag + 1.0 if d == 0 else float(t) - lag)
        zero = jnp.zeros((S5_STATE, cw), _f32)
        v_ref[d, 0] = jnp.concatenate([
            jnp.concatenate([vb_re[:S5_STATE], zero], axis=1), jnp.concatenate([zero, vb_re[S5_STATE:]], axis=1),
            jnp.concatenate([-vb_im[:S5_STATE], zero], axis=1), jnp.concatenate([zero, -vb_im[S5_STATE:]], axis=1),
        ], axis=0).astype(v_ref.dtype)
        ce_re, ce_im = c_pow(lag if d == 0 else (t - 1.0) - lag)
        for g2 in range(2):
            sel = low if g2 == 0 else ~low
            g_lag[d][g2] = (
                jnp.dot(jnp.where(sel, bb_re, 0.0), ce_re, preferred_element_type=_f32, precision=hi)
                - jnp.dot(jnp.where(sel, bb_im, 0.0), ce_im, preferred_element_type=_f32, precision=hi))
    for g2 in range(2):
        rows = []
        for s in range(t):
            fwd = g_lag[0][g2] if s == 0 else pltpu.roll(g_lag[0][g2], S5_GROUP * s, 1)
            fwd = jnp.where(lane_c >= S5_GROUP * s, fwd, 0.0)
            shift = (cw - S5_GROUP * (t - 1 - s)) % cw
            bwd = g_lag[1][g2] if shift == 0 else pltpu.roll(g_lag[1][g2], shift, 1)
            bwd = jnp.where(lane_c < S5_GROUP * (s + 1), bwd, 0.0)
            rows.append(fwd + bwd)
        m_ref[g2] = jnp.concatenate(rows, axis=0).astype(m_ref.dtype)


def _cmul(ar, ai, br, bi):
    return ar * br - ai * bi, ar * bi + ai * br


S5_PAIRS_PER_STEP = 2


def _s5_kernel(chained, u_ref, m_ref, w_ref, v_ref, lam_ref, y_ref, x_scr, h_scr):
    rows = u_ref.shape[1]
    nc = rows // S5_SEGS
    rb = min(512, rows)
    sw = STATE_W
    chains = [(q, d) for q in range(S5_PAIRS_PER_STEP) for d in range(2)]

    for q in range(S5_PAIRS_PER_STEP):
        for r in range(rows // rb):
            rs = slice(r * rb, (r + 1) * rb)
            u = u_ref[q, rs, :]
            for d in range(2):
                x_scr[q, d, rs, :] = jnp.dot(u, w_ref[d, q], preferred_element_type=_f32)

    lam = {(q, d): (lam_ref[d, q, :, :sw], lam_ref[d, q, :, sw:]) for q, d in chains}

    def row_of(d, c):
        cc = c if d == 0 else nc - 1 - c
        return pl.ds(pl.multiple_of(cc * S5_SEGS, S5_SEGS), S5_SEGS)

    def scan(init, store):
        def body(c, carry):
            new = []
            for (q, d), (hr, hi) in zip(chains, carry):
                rs = row_of(d, c)
                if store:
                    h_scr[q, d, rs, :sw] = hr
                    h_scr[q, d, rs, sw:] = hi
                pr, pi = _cmul(lam[q, d][0], lam[q, d][1], hr, hi)
                new.append((pr + x_scr[q, d, rs, :sw], pi + x_scr[q, d, rs, sw:]))
            return tuple(new)

        return lax.fori_loop(0, nc, body, init, unroll=4)

    zero = jnp.zeros((S5_SEGS, sw), _f32)
    start = tuple((zero, zero) for _ in chains)
    if not chained:
        scan(start, store=True)
    else:
        ends = scan(start, store=False)
        one = jnp.ones((S5_SEGS, sw), _f32)
        h_in = []
        for (q, d), (er, ei) in zip(chains, ends):
            dr, di = one[0:1], zero[0:1]
            br, bi = lam[q, d][0][0:1], lam[q, d][1][0:1]
            n = nc
            while n:
                if n & 1:
                    dr, di = _cmul(dr, di, br, bi)
                n >>= 1
                if n:
                    br, bi = _cmul(br, bi, br, bi)
            order = range(S5_SEGS) if d == 0 else range(S5_SEGS - 1, -1, -1)
            cr = jnp.zeros((1, sw), _f32)
            ci = jnp.zeros((1, sw), _f32)
            rows_r = [None] * S5_SEGS
            rows_i = [None] * S5_SEGS
            for sgm in order:
                rows_r[sgm], rows_i[sgm] = cr, ci
                pr, pi = _cmul(dr, di, cr, ci)
                cr, ci = pr + er[sgm:sgm + 1], pi + ei[sgm:sgm + 1]
            h_in.append((jnp.concatenate(rows_r, axis=0), jnp.concatenate(rows_i, axis=0)))
        scan(tuple(h_in), store=True)

    half = PAIR_W // 2

    for q in range(S5_PAIRS_PER_STEP):
        for r in range(rows // rb):
            rs = slice(r * rb, (r + 1) * rb)
            u = u_ref[q, rs, :]
            y = (jnp.dot(h_scr[q, 0, rs, :].astype(_bf16), v_ref[0, q], preferred_element_type=_f32)
                 + jnp.dot(h_scr[q, 1, rs, :].astype(_bf16), v_ref[1, q], preferred_element_type=_f32))
            y0 = y[:, :half] + jnp.dot(u[:, :half], m_ref[2 * q], preferred_element_type=_f32)
            y1 = y[:, half:] + jnp.dot(u[:, half:], m_ref[2 * q + 1], preferred_element_type=_f32)
            y_ref[q, rs, :half] = y0.astype(y_ref.dtype)
            y_ref[q, rs, half:] = y1.astype(y_ref.dtype)


def _s5_core(u_t, tables, chained):
    m_sum, w_in, v_out, lam_t = tables
    _, rows, _ = u_t.shape
    nq = S5_PAIRS_PER_STEP
    return pl.pallas_call(
        functools.partial(_s5_kernel, chained),
        grid=(S5_PAIRS // nq,),
        in_specs=[
            pl.BlockSpec((nq, rows, PAIR_W), lambda p: (p, 0, 0)),
            pl.BlockSpec((2 * nq, PAIR_W // 2, PAIR_W // 2), lambda p: (p, 0, 0)),
            pl.BlockSpec((2, nq, PAIR_W, 2 * STATE_W), lambda p: (0, p, 0, 0)),
            pl.BlockSpec((2, nq, 2 * STATE_W, PAIR_W), lambda p: (0, p, 0, 0)),
            pl.BlockSpec((2, nq, S5_SEGS, 2 * STATE_W), lambda p: (0, p, 0, 0)),
        ],
        out_specs=pl.BlockSpec((nq, rows, PAIR_W), lambda p: (p, 0, 0)),
        out_shape=jax.ShapeDtypeStruct((S5_PAIRS, rows, PAIR_W), _bf16),
        scratch_shapes=[pltpu.VMEM((nq, 2, rows, 2 * STATE_W), _f32),
                        pltpu.VMEM((nq, 2, rows, 2 * STATE_W), _f32)],
        compiler_params=_cparams(("parallel",)),
        name="s5_core",
    )(u_t, m_sum, w_in, v_out, lam_t)


def _gelu_tanh(x):
    return 0.5 * x * (1.0 + jnp.tanh(math.sqrt(2.0 / math.pi) * (x + 0.044715 * (x * x * x))))


def _post_kernel(mixer, *refs):
    if mixer == "attn":
        (x_ref, mix_ref, mod_ref, gain_ref, modn_ref, gainn_ref, perm_ref, wp_ref, w1_ref, w2_ref, o_ref, u_ref,
         x1_scr, h_scr, acc_ref, ubuf) = refs
    else:
        (x_ref, mix_ref, mod_ref, gain_ref, skip_ref, pt_ref, wp_ref, w1_ref, w2_ref, o_ref,
         x1_scr, h_scr, acc_ref, nat_scr) = refs
    tmc = x_ref.shape[1]

    def rows(seg):
        return slice(seg * tmc, (seg + 1) * tmc)

    if mixer == "attn":
        for seg in range(S5_SEGS):
            h_scr[rows(seg), :] = mix_ref[seg]
        y = jnp.dot(h_scr[...], wp_ref[...], preferred_element_type=_f32)
    else:
        nck = tmc // S5_CHUNK
        blk = nck * S5_SEGS
        a = jnp.concatenate(
            [jnp.concatenate([mix_ref[(8 * lt + gl) // 2, :, _chunk_col(8 * lt + gl, h):_chunk_col(8 * lt + gl, h) + 128]
                              for gl in range(8)], axis=1)
             for lt in range(LANE_TILES) for h in range(2)], axis=0)
        for sp in range(4):
            out = jnp.dot(a, pt_ref[:, sp * 256:(sp + 1) * 256], preferred_element_type=_f32)
            for lt in range(LANE_TILES):
                for h in range(2):
                    for s in (2 * sp, 2 * sp + 1):
                        for c in range(nck):
                            tok = c * S5_CHUNK + 8 * h + s
                            row = (lt * 2 + h) * blk + c * S5_SEGS
                            nat_scr[lt, tok * S5_SEGS:(tok + 1) * S5_SEGS, :] = (
                                out[row:row + S5_SEGS, (s % 2) * 128:(s % 2 + 1) * 128])
        for seg in range(S5_SEGS):
            mod = mod_ref[seg]
            u = _norm_mod(x_ref[seg], gain_ref[0:1, :], mod[1:2, :], mod[0:1, :])
            y_ssm = jnp.concatenate([nat_scr[lt, pl.ds(seg, tmc, stride=S5_SEGS), :]
                                     for lt in range(D_MODEL // 128)], axis=1)
            h_scr[rows(seg), :] = _gelu_tanh(skip_ref[...] * u + y_ssm).astype(_bf16)
        g = h_scr[...]
        val = jnp.dot(g, wp_ref[:, :D_MODEL], preferred_element_type=_f32)
        gate = jnp.dot(g, wp_ref[:, D_MODEL:], preferred_element_type=_f32)
        y = val * jax.nn.sigmoid(gate)
    for seg in range(S5_SEGS):
        mod = mod_ref[seg]
        x1 = x_ref[seg] + mod[2:3, :] * _rms(y[rows(seg), :], gain_ref[1:2, :])
        x1_scr[rows(seg), :] = x1
        h_scr[rows(seg), :] = _norm_mod(x1, gain_ref[2:3, :], mod[4:5, :], mod[3:4, :]).astype(_bf16)
    h2 = h_scr[...]
    tf = 512
    for kf in range(D_FF // tf):
        cols = slice(kf * tf, (kf + 1) * tf)
        a = jnp.dot(h2, w1_ref[:, cols], preferred_element_type=_f32)
        a = jnp.square(jnp.maximum(a, 0.0)).astype(_bf16)
        part = jnp.dot(a, w2_ref[cols, :], preferred_element_type=_f32)
        if kf == 0:
            acc_ref[...] = part
        else:
            acc_ref[...] += part
    for seg in range(S5_SEGS):
        mod = mod_ref[seg]
        out = x1_scr[rows(seg), :] + mod[5:6, :] * _rms(acc_ref[rows(seg), :], gain_ref[3:4, :])
        o_ref[seg] = out
        if mixer == "attn":
            modn = modn_ref[seg]
            _chunk_stage(_norm_mod(out, gainn_ref[0:1, :], modn[1:2, :], modn[0:1, :]), seg, ubuf)
    if mixer == "attn":
        _chunk_emit(ubuf, perm_ref, u_ref)


def _post_mixer(mixer, x8, mix, mod8, gains, w_proj, w1, w2, skip=None, perm=None, next_mod8=None, next_gains=None,
                tmc=64):
    _, ls, _ = x8.shape
    tok = lambda j: (0, j, 0)
    const2 = lambda j: (0, 0)
    m = S5_SEGS * tmc
    chunk_rows = tmc // S5_CHUNK * S5_SEGS
    mod_spec = _resident((S5_SEGS, 6, D_MODEL), lambda j: (0, 0, 0))
    in_specs = [pl.BlockSpec((S5_SEGS, tmc, D_MODEL), tok)]
    scratch = [pltpu.VMEM((m, D_MODEL), _f32), pltpu.VMEM((m, D_MODEL), _bf16), pltpu.VMEM((m, D_MODEL), _f32)]
    out_specs = pl.BlockSpec((S5_SEGS, tmc, D_MODEL), tok)
    out_shape = jax.ShapeDtypeStruct(x8.shape, _f32)
    if mixer == "attn":
        in_specs.append(pl.BlockSpec((S5_SEGS, tmc, D_MODEL), tok))
        scratch.append(pltpu.VMEM((LANE_TILES, m, 128), _f32))
        out_specs = (out_specs, pl.BlockSpec((S5_PAIRS, chunk_rows, PAIR_W), tok))
        out_shape = (out_shape, jax.ShapeDtypeStruct((S5_PAIRS, ls // S5_CHUNK * S5_SEGS, PAIR_W), _bf16))
    else:
        in_specs.append(pl.BlockSpec((S5_PAIRS, chunk_rows, PAIR_W), tok))
        scratch.append(pltpu.VMEM((LANE_TILES, m, 128), _f32))
    in_specs += [mod_spec, _resident((4, D_MODEL), const2)]
    args = [x8, mix, mod8, gains]
    if mixer == "attn":
        in_specs += [mod_spec, _resident((4, D_MODEL), const2), _resident((D_MODEL, D_MODEL), const2)]
        args += [next_mod8, next_gains, perm]
    else:
        in_specs += [_resident((1, D_MODEL), const2), _resident((D_MODEL, D_MODEL), const2)]
        args += [skip, perm]
    in_specs += [_resident(w_proj.shape, const2), _resident(w1.shape, const2), _resident(w2.shape, const2)]
    args += [w_proj, w1, w2]
    return pl.pallas_call(
        functools.partial(_post_kernel, mixer),
        grid=(ls // tmc,),
        in_specs=in_specs,
        out_specs=out_specs,
        out_shape=out_shape,
        scratch_shapes=scratch,
        compiler_params=_cparams(("parallel",)),
        name="post_" + mixer,
    )(*args)


def _trunk(x, mod, params, chained):
    b, l, _ = x.shape
    assert (b == 1) if chained else (b == S5_SEGS)
    ls = b * l // S5_SEGS
    gains = params["norm_gain"]
    mod8 = jnp.broadcast_to(mod, (DEPTH, S5_SEGS, 6, D_MODEL))
    qkv = _qkv_proj(x, mod[0], gains[0], params["w_qkv"])
    att = _attention(qkv, params["attn_bias"])
    x8, u = _post_mixer("attn", x.reshape(S5_SEGS, ls, D_MODEL), att.reshape(S5_SEGS, ls, D_MODEL), mod8[0],
                        gains[0], params["w_o"], params["ffn_w1"][0], params["ffn_w2"][0],
                        perm=_lane_perm(), next_mod8=mod8[1], next_gains=gains[1])
    y = _s5_core(u, params["s5_tables"], chained)
    x8 = _post_mixer("s5", x8, y, mod8[1], gains[1], params["w_glu"], params["ffn_w1"][1], params["ffn_w2"][1],
                     skip=params["s5_d"], perm=_lane_perm(inverse=True))
    return x8.reshape(b, l, D_MODEL)


def kernel(x_prompt, x_sample, c_prompt, c_sample, ada_w, ada_b, norm_gain, attn_w_qkv, attn_w_o, attn_rpb,
           s5_a_re, s5_a_im, s5_log_dt, s5_b_re, s5_b_im, s5_c_re, s5_c_im, s5_d, s5_w_glu, ffn_w1, ffn_w2):
    nbp, nbs = c_prompt.shape[0], c_sample.shape[0]
    nb = -(-(nbp + nbs) // 8) * 8
    c_all = jnp.concatenate([c_prompt, c_sample, jnp.zeros((nb - nbp - nbs, D_MODEL), _f32)], axis=0)
    mod = _modulation(c_all, ada_w, ada_b)
    mod_p = mod[:, :nbp].reshape(DEPTH, nbp, 6, D_MODEL)
    mod_s = mod[:, nbp:nbp + nbs].reshape(DEPTH, nbs, 6, D_MODEL)

    params = {
        "norm_gain": norm_gain,
        "w_qkv": attn_w_qkv[0].astype(_bf16),
        "w_o": attn_w_o[0].astype(_bf16),
        "attn_bias": _attn_bias_table(attn_rpb[0]),
        "s5_tables": _s5_tables(s5_a_re[0], s5_a_im[0], s5_log_dt[0], s5_b_re[0], s5_b_im[0],
                                s5_c_re[0], s5_c_im[0]),
        "s5_d": s5_d[0].reshape(1, D_MODEL),
        "w_glu": s5_w_glu[0].astype(_bf16),
        "ffn_w1": ffn_w1.astype(_bf16),
        "ffn_w2": ffn_w2.astype(_bf16),
    }
    y_prompt = _trunk(x_prompt, mod_p, params, chained=False)
    y_sample = _trunk(x_sample, mod_s, params, chained=True)
    return (y_prompt, y_sample)
```

```python
import functools
import math

import numpy as np
import jax
import jax.numpy as jnp
from jax import lax
from jax.experimental import pallas as pl
from jax.experimental.pallas import tpu as pltpu

D_MODEL = 1024
D_FF = 4 * D_MODEL
DEPTH = 2
EPS = 1e-6
NEG_INF = -1e30
LOG2E = math.log2(math.e)

GRID_W = 64
N_HEADS = 16
HEAD_DIM = D_MODEL // N_HEADS
WIN_H = 8
WIN_W = 16
ROWS_PER_TILE = 4
TILE_TOK = ROWS_PER_TILE * GRID_W
KEY_TILES = 3
S5_GROUP = 16
S5_GROUPS = D_MODEL // S5_GROUP
S5_STATE = 64
S5_CHUNK = 16
S5_SEGS = 8
S5_PAIRS = S5_GROUPS // 2
PAIR_W = 2 * S5_CHUNK * S5_GROUP
STATE_W = 2 * S5_STATE
NPOW = 32

LANES = 128
MXU_EDGE = 256

VMEM_LIMIT = 56 * 1024 * 1024

_f32 = jnp.float32
_bf16 = jnp.bfloat16


def _cparams(sem):
    return pltpu.CompilerParams(dimension_semantics=sem, vmem_limit_bytes=VMEM_LIMIT)


def _resident(shape, index_map):
    return pl.BlockSpec(shape, index_map, pipeline_mode=pl.Buffered(1))


def _rms(x, gain):
    ms = jnp.mean(x * x, axis=-1, keepdims=True)
    return x * lax.rsqrt(ms + EPS) * gain


def _norm_mod(x, gain, scale, shift):
    return _rms(x, gain * (1.0 + scale)) + shift


def _mod_kernel(c_ref, w_ref, b_ref, o_ref):
    c = c_ref[...]
    act = c * jax.nn.sigmoid(c)
    o_ref[0] = jnp.dot(act, w_ref[0], preferred_element_type=_f32,
                       precision=lax.Precision.HIGHEST) + b_ref[0]


def _modulation(c_all, ada_w, ada_b):
    nb = c_all.shape[0]
    tn = 1536
    return pl.pallas_call(
        _mod_kernel,
        grid=(DEPTH, 6 * D_MODEL // tn),
        in_specs=[
            pl.BlockSpec((nb, D_MODEL), lambda i, n: (0, 0)),
            pl.BlockSpec((1, D_MODEL, tn), lambda i, n: (i, 0, n)),
            pl.BlockSpec((1, 1, tn), lambda i, n: (i, 0, n)),
        ],
        out_specs=pl.BlockSpec((1, nb, tn), lambda i, n: (i, 0, n)),
        out_shape=jax.ShapeDtypeStruct((DEPTH, nb, 6 * D_MODEL), _f32),
        compiler_params=_cparams(("arbitrary", "arbitrary")),
        name="adaln_mod",
    )(c_all, ada_w, ada_b.reshape(DEPTH, 1, 6 * D_MODEL))


def _qkv_kernel(x_ref, mod_ref, gain_ref, w_ref, o_ref):
    mod = mod_ref[0]
    h = _norm_mod(x_ref[0], gain_ref[0:1, :], mod[1:2, :], mod[0:1, :]).astype(_bf16)
    q_scale = HEAD_DIM ** -0.5 * LOG2E
    for part in range(3):
        cols = slice(part * D_MODEL, (part + 1) * D_MODEL)
        y = jnp.dot(h, w_ref[:, cols], preferred_element_type=_f32)
        if part == 0:
            y = y * q_scale
        o_ref[0, :, cols] = y.astype(_bf16)


def _qkv_proj(x, mod, gains, w_qkv, tm=1024):
    b, l, _ = x.shape
    return pl.pallas_call(
        _qkv_kernel,
        grid=(b, l // tm),
        in_specs=[
            pl.BlockSpec((1, tm, D_MODEL), lambda i, j: (i, j, 0)),
            pl.BlockSpec((1, 6, D_MODEL), lambda i, j: (i, 0, 0)),
            _resident((4, D_MODEL), lambda i, j: (0, 0)),
            _resident((D_MODEL, 3 * D_MODEL), lambda i, j: (0, 0)),
        ],
        out_specs=pl.BlockSpec((1, tm, 3 * D_MODEL), lambda i, j: (i, j, 0)),
        out_shape=jax.ShapeDtypeStruct((b, l, 3 * D_MODEL), _bf16),
        compiler_params=_cparams(("parallel", "parallel")),
        name="norm_qkv",
    )(x, mod, gains, w_qkv)


LANE_TILES = D_MODEL // LANES


def _lane_perm(inverse=False):
    eye = np.eye
    p = np.einsum('sS,gG,iI->sgiGSI', eye(8), eye(8), eye(S5_GROUP)).reshape(D_MODEL, D_MODEL)
    return jnp.asarray(p.T if inverse else p, _bf16)


def _chunk_col(g, h):
    return (g % 2) * (PAIR_W // 2) + h * LANES


def _chunk_stage(u, seg, ubuf):
    tmc = u.shape[0]
    for lt in range(LANE_TILES):
        ubuf[lt, seg * tmc:(seg + 1) * tmc, :] = u[:, lt * LANES:(lt + 1) * LANES]


def _chunk_emit(ubuf, p_ref, o_ref):
    tmc = ubuf.shape[1] // S5_SEGS
    nck = tmc // S5_CHUNK
    blk = nck * S5_SEGS
    a = jnp.concatenate(
        [jnp.concatenate([ubuf[lt, pl.ds(c * S5_CHUNK + 8 * h + s, S5_SEGS, stride=tmc), :]
                          for lt in range(LANE_TILES) for h in range(2) for c in range(nck)], axis=0)
         for s in range(8)], axis=1).astype(_bf16)
    for gp in range(4):
        out = jnp.dot(a, p_ref[:, gp * MXU_EDGE:(gp + 1) * MXU_EDGE], preferred_element_type=_f32)
        for lt in range(LANE_TILES):
            for h in range(2):
                for gl in (2 * gp, 2 * gp + 1):
                    g = 8 * lt + gl
                    col = _chunk_col(g, h)
                    piece = out[(lt * 2 + h) * blk:(lt * 2 + h + 1) * blk, (gl % 2) * LANES:(gl % 2 + 1) * LANES]
                    o_ref[g // 2, :, col:col + LANES] = piece.astype(o_ref.dtype)


N_REL_ROWS = 2 * WIN_H - 1


def _bias_rows():
    i = np.arange(ROWS_PER_TILE)[:, None]
    kk = np.arange(KEY_TILES * ROWS_PER_TILE)[None, :]
    dr = np.stack([kk - i + 7, kk - i + 3, kk - i - 1])
    row_ok = np.stack([(kk < WIN_H) & (i >= 0), (kk >= i) & (kk < i + WIN_H), (kk >= 4) & (i >= 0)])
    return np.where(row_ok, dr, N_REL_ROWS)


def _bias_kernel(rp_ref, o_ref):
    dr_tab = _bias_rows()
    pair_w = 2 * GRID_W
    qc = lax.broadcasted_iota(jnp.int32, (GRID_W, pair_w), 0)
    lane = lax.broadcasted_iota(jnp.int32, (GRID_W, pair_w), 1)
    kc = jnp.bitwise_and(lane, GRID_W - 1)
    col_start = jnp.clip(qc - WIN_W // 2, 0, GRID_W - WIN_W)
    col_ok = jnp.logical_and(kc >= col_start, kc < col_start + WIN_W)
    left_half = lane < GRID_W
    shift = pair_w - (WIN_W - 1)
    left, right = [], []
    for dr in range(N_REL_ROWS):
        base = jnp.broadcast_to(rp_ref[0, dr:dr + 1, :], (GRID_W, pair_w))
        left.append(pltpu.roll(base, shift, 1, stride=1, stride_axis=0))
        right.append(pltpu.roll(base, (shift + GRID_W) % pair_w, 1, stride=1, stride_axis=0))
    masked = jnp.full((GRID_W, pair_w), NEG_INF, _f32)
    left.append(masked)
    right.append(masked)
    for pt in range(dr_tab.shape[0]):
        for i in range(ROWS_PER_TILE):
            for pr in range(KEY_TILES * 2):
                blk = jnp.where(left_half, left[dr_tab[pt, i, 2 * pr]], right[dr_tab[pt, i, 2 * pr + 1]])
                o_ref[pt, 0, i, pr] = jnp.where(col_ok, blk, NEG_INF).astype(o_ref.dtype)


def _attn_bias_table(rpb):
    rel_rows_pad = -(-N_REL_ROWS // 8) * 8
    rp = jnp.pad(rpb.astype(_f32) * LOG2E,
                 ((0, 0), (0, rel_rows_pad - N_REL_ROWS), (0, 2 * GRID_W - (2 * WIN_W - 1))))
    shape = (3, N_HEADS, ROWS_PER_TILE, KEY_TILES * 2, GRID_W, 2 * GRID_W)
    return pl.pallas_call(
        _bias_kernel,
        grid=(N_HEADS,),
        in_specs=[pl.BlockSpec((1, rel_rows_pad, 2 * GRID_W), lambda h: (h, 0, 0))],
        out_specs=pl.BlockSpec((3, 1) + shape[2:], lambda h: (0, h, 0, 0, 0, 0)),
        out_shape=jax.ShapeDtypeStruct(shape, _bf16),
        compiler_params=_cparams(("parallel",)),
        name="attn_bias",
    )(rp)


def _attn_heads(interior, q_ref, k_refs, v_refs, bias_ref, o_ref):
    lane = lax.broadcasted_iota(jnp.int32, (1, LANES), 1)
    low = lane < HEAD_DIM
    head_mask = [low.astype(_bf16), (~low).astype(_bf16)]
    head_of_lane = jnp.right_shift(lax.broadcasted_iota(jnp.int32, (1, MXU_EDGE), 1), HEAD_DIM.bit_length() - 1)
    n_pairs = 2 * KEY_TILES
    zero_blk = jnp.zeros((GRID_W, 2 * GRID_W), _bf16)
    heads_per_slice = MXU_EDGE // HEAD_DIM
    for hq in range(N_HEADS // heads_per_slice):
        vcols = slice(hq * MXU_EDGE, (hq + 1) * MXU_EDGE)
        acc = None
        for h4 in range(heads_per_slice):
            head = heads_per_slice * hq + h4
            cols = slice((head // 2) * LANES, (head // 2 + 1) * LANES)
            qm = q_ref[0, :, cols] * head_mask[head % 2]
            s = [lax.dot_general(qm, k_refs[t][0, :, cols], (((1,), (1,)), ((), ())),
                                 preferred_element_type=_f32) for t in range(KEY_TILES)]
            p_rows, inv_l = [], []
            for i in range(ROWS_PER_TILE):
                pairs = range(i // 2, (i + WIN_H - 1) // 2 + 1) if interior else range(n_pairs)
                rows = slice(i * GRID_W, (i + 1) * GRID_W)
                blk = {pr: s[pr // 2][rows, (pr % 2) * LANES:(pr % 2 + 1) * LANES]
                       + bias_ref[0, head, i, pr].astype(_f32) for pr in pairs}
                m = None
                for pr in pairs:
                    m = blk[pr] if m is None else jnp.maximum(m, blk[pr])
                m = jnp.max(m, axis=-1, keepdims=True)
                e = {pr: jnp.exp2(blk[pr] - m) for pr in pairs}
                tot = None
                for pr in pairs:
                    tot = e[pr] if tot is None else tot + e[pr]
                inv_l.append(1.0 / jnp.sum(tot, axis=-1, keepdims=True))
                p_rows.append([e[pr].astype(_bf16) if pr in e else zero_blk for pr in range(n_pairs)])
            o = None
            for t in range(KEY_TILES):
                p_t = jnp.concatenate(
                    [jnp.concatenate([p_rows[i][2 * t], p_rows[i][2 * t + 1]], axis=1)
                     for i in range(ROWS_PER_TILE)], axis=0)
                part = jnp.dot(p_t, v_refs[t][0, :, vcols], preferred_element_type=_f32)
                o = part if o is None else o + part
            o = o * jnp.concatenate(inv_l, axis=0)
            acc = o if acc is None else jnp.where(head_of_lane == h4, o, acc)
        o_ref[0, :, vcols] = acc.astype(o_ref.dtype)


def _attn_kernel(q_ref, k0_ref, k1_ref, k2_ref, v0_ref, v1_ref, v2_ref, bias_ref, o_ref):
    j = pl.program_id(1)
    interior = jnp.logical_and(j > 0, j < pl.num_programs(1) - 1)
    args = (q_ref, (k0_ref, k1_ref, k2_ref), (v0_ref, v1_ref, v2_ref), bias_ref, o_ref)
    pl.when(interior)(functools.partial(_attn_heads, True, *args))
    pl.when(jnp.logical_not(interior))(functools.partial(_attn_heads, False, *args))


def _attention(qkv, bias):
    b, l, _ = qkv.shape
    nt = l // TILE_TOK
    assert nt >= KEY_TILES

    def base(j):
        return jnp.clip(j - 1, 0, nt - KEY_TILES)

    def pattern(j):
        return jnp.where(j == 0, 0, jnp.where(j == nt - 1, 2, 1))

    def kv_spec(part, t):
        return pl.BlockSpec((1, TILE_TOK, D_MODEL), lambda i, j: (i, base(j) + t, part))

    return pl.pallas_call(
        _attn_kernel,
        grid=(b, nt),
        in_specs=[pl.BlockSpec((1, TILE_TOK, D_MODEL), lambda i, j: (i, j, 0))]
        + [kv_spec(1, t) for t in range(KEY_TILES)]
        + [kv_spec(2, t) for t in range(KEY_TILES)]
        + [pl.BlockSpec((1, N_HEADS, ROWS_PER_TILE, KEY_TILES * 2, GRID_W, 2 * GRID_W),
                        lambda i, j: (pattern(j), 0, 0, 0, 0, 0))],
        out_specs=pl.BlockSpec((1, TILE_TOK, D_MODEL), lambda i, j: (i, j, 0)),
        out_shape=jax.ShapeDtypeStruct((b, l, D_MODEL), _bf16),
        compiler_params=_cparams(("parallel", "arbitrary")),
        name="nbr_attention",
    )(qkv, qkv, qkv, qkv, qkv, qkv, qkv, bias)


def _s5_tables(a_re, a_im, log_dt, b_re, b_im, c_re, c_im):
    f = lambda v: v.astype(_f32)
    pairs = (2, S5_PAIRS)
    a_row = [f(a).reshape(*pairs, 1, STATE_W) for a in (a_re, a_im)]
    a_col = [f(a).reshape(*pairs, STATE_W, 1) for a in (a_re, a_im)]
    ldt = jnp.repeat(f(log_dt), S5_STATE, axis=-1)
    ldt_row = ldt.reshape(*pairs, 1, STATE_W)
    ldt_col = ldt.reshape(*pairs, STATE_W, 1)
    b_t = [jnp.transpose(f(b).reshape(*pairs, 2, S5_STATE, S5_GROUP), (0, 1, 4, 2, 3))
           .reshape(*pairs, S5_GROUP, STATE_W) for b in (b_re, b_im)]
    c_t = [jnp.transpose(f(c), (0, 1, 3, 2)).reshape(*pairs, STATE_W, S5_GROUP) for c in (c_re, c_im)]

    def spec(shape):
        nd = len(shape)
        return pl.BlockSpec((2, 1) + tuple(shape[2:]), lambda p: (0, p) + (0,) * (nd - 2))

    ins = a_row + a_col + [ldt_row, ldt_col] + b_t + c_t
    out_shapes = (
        jax.ShapeDtypeStruct((S5_GROUPS, PAIR_W // 2, PAIR_W // 2), _bf16),
        jax.ShapeDtypeStruct((2, S5_PAIRS, PAIR_W, 2 * STATE_W), _bf16),
        jax.ShapeDtypeStruct((2, S5_PAIRS, 2 * STATE_W, PAIR_W), _bf16),
        jax.ShapeDtypeStruct((2, S5_PAIRS, S5_SEGS, 2 * STATE_W), _f32),
    )
    return pl.pallas_call(
        _s5_tables_kernel,
        grid=(S5_PAIRS,),
        in_specs=[spec(v.shape) for v in ins],
        out_specs=(pl.BlockSpec((2, PAIR_W // 2, PAIR_W // 2), lambda p: (p, 0, 0)),
                   spec(out_shapes[1].shape), spec(out_shapes[2].shape), spec(out_shapes[3].shape)),
        out_shape=out_shapes,
        compiler_params=_cparams(("parallel",)),
        name="s5_tables",
    )(*ins)


def _cexp(k, a_re, a_im, dt):
    mag = jnp.exp(k * (a_re * dt))
    ang = k * (a_im * dt)
    return mag * jnp.cos(ang), mag * jnp.sin(ang)


def _s5_tables_kernel(ar_re_ref, ar_im_ref, ac_re_ref, ac_im_ref, ldr_ref, ldc_ref,
                      bt_re_ref, bt_im_ref, ct_re_ref, ct_im_ref, m_ref, w_ref, v_ref, lam_ref):
    hi = lax.Precision.HIGHEST
    t = S5_CHUNK
    cw = t * S5_GROUP
    lane_c = lax.broadcasted_iota(jnp.int32, (1, cw), 1)
    log_group = S5_GROUP.bit_length() - 1
    lag = jnp.right_shift(lane_c, log_group).astype(_f32)
    tok_row = jnp.right_shift(lax.broadcasted_iota(jnp.int32, (cw, 1), 0), log_group).astype(_f32)
    low = lax.broadcasted_iota(jnp.int32, (1, STATE_W), 1) < S5_STATE
    tile_i = (lax.broadcasted_iota(jnp.int32, (S5_GROUP, cw), 0)
              == jnp.bitwise_and(lax.broadcasted_iota(jnp.int32, (S5_GROUP, cw), 1), S5_GROUP - 1)).astype(_f32)
    k_sub = jnp.minimum(lax.broadcasted_iota(jnp.int32, (NPOW, 1), 0), t).astype(_f32)
    k_lane = jnp.minimum(lax.broadcasted_iota(jnp.int32, (1, STATE_W), 1), t).astype(_f32)
    pow_of_row = lax.broadcasted_iota(jnp.int32, (cw, NPOW), 1).astype(_f32)
    pow_of_lane = lax.broadcasted_iota(jnp.int32, (STATE_W, cw), 0).astype(_f32)

    def spread(a, b, data_left=True):
        data, sel = (a, b) if data_left else (b, a)
        top = data.astype(_bf16)
        rest = data - top.astype(_f32)
        mid = rest.astype(_bf16)
        low = (rest - mid.astype(_f32)).astype(_bf16)
        sel = sel.astype(_bf16)
        parts = [jnp.dot(p, sel, preferred_element_type=_f32) if data_left
                 else jnp.dot(sel, p, preferred_element_type=_f32) for p in (top, mid, low)]
        return (parts[0] + parts[1]) + parts[2]

    g_lag = [[None, None], [None, None]]
    for d in range(2):
        a_re, a_im = ar_re_ref[d, 0], ar_im_ref[d, 0]
        pw_re, pw_im = _cexp(k_sub, a_re, a_im, jnp.exp(ldr_ref[d, 0]))
        lam_re, lam_im = pw_re[1:2], pw_im[1:2]
        den = a_re * a_re + a_im * a_im
        nr, ni = lam_re - 1.0, lam_im
        zr = (nr * a_re + ni * a_im) / den
        zi = (ni * a_re - nr * a_im) / den
        bb_re = zr * bt_re_ref[d, 0] - zi * bt_im_ref[d, 0]
        bb_im = zr * bt_im_ref[d, 0] + zi * bt_re_ref[d, 0]
        sel_tok = (pow_of_row == ((t - 1.0) - tok_row if d == 0 else tok_row)).astype(_f32)
        p_re = spread(sel_tok, pw_re, data_left=False)
        p_im = spread(sel_tok, pw_im, data_left=False)
        bt_re = jnp.concatenate([bb_re] * t, axis=0)
        bt_im = jnp.concatenate([bb_im] * t, axis=0)
        w_re, w_im = _cmul(p_re, p_im, bt_re, bt_im)
        top = jnp.concatenate([jnp.where(low, w_re, 0.0), jnp.where(low, w_im, 0.0)], axis=1)
        bot = jnp.concatenate([jnp.where(low, 0.0, w_re), jnp.where(low, 0.0, w_im)], axis=1)
        w_ref[d, 0] = jnp.concatenate([top, bot], axis=0).astype(w_ref.dtype)
        lam_ref[d, 0] = jnp.broadcast_to(jnp.concatenate([pw_re[t:t + 1], pw_im[t:t + 1]], axis=1),
                                         (S5_SEGS, 2 * STATE_W))
        pc_re, pc_im = _cexp(k_lane, ac_re_ref[d, 0], ac_im_ref[d, 0], jnp.exp(ldc_ref[d, 0]))
        ct_re = spread(ct_re_ref[d, 0], tile_i)
        ct_im = spread(ct_im_ref[d, 0], tile_i)

        def c_pow(k_of_block):
            sel = (pow_of_lane == k_of_block).astype(_f32)
            return _cmul(ct_re, ct_im, spread(pc_re, sel), spread(pc_im, sel))

        vb_re, vb_im = c_pow(lag + 1.0 if d == 0 else float(t) - lag)
        zero = jnp.zeros((S5_STATE, cw), _f32)
        v_ref[d, 0] = jnp.concatenate([
            jnp.concatenate([vb_re[:S5_STATE], zero], axis=1), jnp.concatenate([zero, vb_re[S5_STATE:]], axis=1),
            jnp.concatenate([-vb_im[:S5_STATE], zero], axis=1), jnp.concatenate([zero, -vb_im[S5_STATE:]], axis=1),
        ], axis=0).astype(v_ref.dtype)
        ce_re, ce_im = c_pow(lag if d == 0 else (t - 1.0) - lag)
        for g2 in range(2):
            sel = low if g2 == 0 else ~low
            g_lag[d][g2] = (
                jnp.dot(jnp.where(sel, bb_re, 0.0), ce_re, preferred_element_type=_f32, precision=hi)
                - jnp.dot(jnp.where(sel, bb_im, 0.0), ce_im, preferred_element_type=_f32, precision=hi))
    for g2 in range(2):
        rows = []
        for s in range(t):
            fwd = g_lag[0][g2] if s == 0 else pltpu.roll(g_lag[0][g2], S5_GROUP * s, 1)
            fwd = jnp.where(lane_c >= S5_GROUP * s, fwd, 0.0)
            shift = (cw - S5_GROUP * (t - 1 - s)) % cw
            bwd = g_lag[1][g2] if shift == 0 else pltpu.roll(g_lag[1][g2], shift, 1)
            bwd = jnp.where(lane_c < S5_GROUP * (s + 1), bwd, 0.0)
            rows.append(fwd + bwd)
        m_ref[g2] = jnp.concatenate(rows, axis=0).astype(m_ref.dtype)


def _cmul(ar, ai, br, bi):
    return ar * br - ai * bi, ar * bi + ai * br


S5_PAIRS_PER_STEP = 2


def _s5_kernel(chained, u_ref, m_ref, w_ref, v_ref, lam_ref, y_ref, x_scr, h_scr):
    rows = u_ref.shape[1]
    nc = rows // S5_SEGS
    rb = min(512, rows)
    sw = STATE_W
    chains = [(q, d) for q in range(S5_PAIRS_PER_STEP) for d in range(2)]

    for q in range(S5_PAIRS_PER_STEP):
        for r in range(rows // rb):
            rs = slice(r * rb, (r + 1) * rb)
            u = u_ref[q, rs, :]
            for d in range(2):
                x_scr[q, d, rs, :] = jnp.dot(u, w_ref[d, q], preferred_element_type=_f32)

    lam = {(q, d): (lam_ref[d, q, :, :sw], lam_ref[d, q, :, sw:]) for q, d in chains}

    def row_of(d, c):
        cc = c if d == 0 else nc - 1 - c
        return pl.ds(pl.multiple_of(cc * S5_SEGS, S5_SEGS), S5_SEGS)

    def scan(init, store):
        def body(c, carry):
            new = []
            for (q, d), (hr, hi) in zip(chains, carry):
                rs = row_of(d, c)
                if store:
                    h_scr[q, d, rs, :sw] = hr
                    h_scr[q, d, rs, sw:] = hi
                pr, pi = _cmul(lam[q, d][0], lam[q, d][1], hr, hi)
                new.append((pr + x_scr[q, d, rs, :sw], pi + x_scr[q, d, rs, sw:]))
            return tuple(new)

        return lax.fori_loop(0, nc, body, init, unroll=4)

    zero = jnp.zeros((S5_SEGS, sw), _f32)
    start = tuple((zero, zero) for _ in chains)
    if not chained:
        scan(start, store=True)
    else:
        ends = scan(start, store=False)
        one = jnp.ones((S5_SEGS, sw), _f32)
        h_in = []
        for (q, d), (er, ei) in zip(chains, ends):
            dr, di = one[0:1], zero[0:1]
            br, bi = lam[q, d][0][0:1], lam[q, d][1][0:1]
            n = nc
            while n:
                if n & 1:
                    dr, di = _cmul(dr, di, br, bi)
                n >>= 1
                if n:
                    br, bi = _cmul(br, bi, br, bi)
            order = range(S5_SEGS) if d == 0 else range(S5_SEGS - 1, -1, -1)
            cr = jnp.zeros((1, sw), _f32)
            ci = jnp.zeros((1, sw), _f32)
            rows_r = [None] * S5_SEGS
            rows_i = [None] * S5_SEGS
            for sgm in order:
                rows_r[sgm], rows_i[sgm] = cr, ci
                pr, pi = _cmul(dr, di, cr, ci)
                cr, ci = pr + er[sgm:sgm + 1], pi + ei[sgm:sgm + 1]
            h_in.append((jnp.concatenate(rows_r, axis=0), jnp.concatenate(rows_i, axis=0)))
        scan(tuple(h_in), store=True)

    half = PAIR_W // 2

    for q in range(S5_PAIRS_PER_STEP):
        for r in range(rows // rb):
            rs = slice(r * rb, (r + 1) * rb)
            u = u_ref[q, rs, :]
            y = (jnp.dot(h_scr[q, 0, rs, :].astype(_bf16), v_ref[0, q], preferred_element_type=_f32)
                 + jnp.dot(h_scr[q, 1, rs, :].astype(_bf16), v_ref[1, q], preferred_element_type=_f32))
            y0 = y[:, :half] + jnp.dot(u[:, :half], m_ref[2 * q], preferred_element_type=_f32)
            y1 = y[:, half:] + jnp.dot(u[:, half:], m_ref[2 * q + 1], preferred_element_type=_f32)
            y_ref[q, rs, :half] = y0.astype(y_ref.dtype)
            y_ref[q, rs, half:] = y1.astype(y_ref.dtype)


def _s5_core(u_t, tables, chained):
    m_sum, w_in, v_out, lam_t = tables
    _, rows, _ = u_t.shape
    nq = S5_PAIRS_PER_STEP
    return pl.pallas_call(
        functools.partial(_s5_kernel, chained),
        grid=(S5_PAIRS // nq,),
        in_specs=[
            pl.BlockSpec((nq, rows, PAIR_W), lambda p: (p, 0, 0)),
            pl.BlockSpec((2 * nq, PAIR_W // 2, PAIR_W // 2), lambda p: (p, 0, 0)),
            pl.BlockSpec((2, nq, PAIR_W, 2 * STATE_W), lambda p: (0, p, 0, 0)),
            pl.BlockSpec((2, nq, 2 * STATE_W, PAIR_W), lambda p: (0, p, 0, 0)),
            pl.BlockSpec((2, nq, S5_SEGS, 2 * STATE_W), lambda p: (0, p, 0, 0)),
        ],
        out_specs=pl.BlockSpec((nq, rows, PAIR_W), lambda p: (p, 0, 0)),
        out_shape=jax.ShapeDtypeStruct((S5_PAIRS, rows, PAIR_W), _bf16),
        scratch_shapes=[pltpu.VMEM((nq, 2, rows, 2 * STATE_W), _f32),
                        pltpu.VMEM((nq, 2, rows, 2 * STATE_W), _f32)],
        compiler_params=_cparams(("parallel",)),
        name="s5_core",
    )(u_t, m_sum, w_in, v_out, lam_t)


def _gelu_tanh(x):
    return 0.5 * x * (1.0 + jnp.tanh(math.sqrt(2.0 / math.pi) * (x + 0.044715 * (x * x * x))))


def _post_kernel(mixer, *refs):
    if mixer == "attn":
        (x_ref, mix_ref, mod_ref, gain_ref, modn_ref, gainn_ref, perm_ref, wp_ref, w1_ref, w2_ref, o_ref, u_ref,
         x1_scr, h_scr, acc_ref, ubuf) = refs
    else:
        (x_ref, mix_ref, mod_ref, gain_ref, skip_ref, pt_ref, wp_ref, w1_ref, w2_ref, o_ref,
         x1_scr, h_scr, acc_ref, nat_scr) = refs
    tmc = x_ref.shape[1]

    def rows(seg):
        return slice(seg * tmc, (seg + 1) * tmc)

    if mixer == "attn":
        for seg in range(S5_SEGS):
            h_scr[rows(seg), :] = mix_ref[seg]
        y = jnp.dot(h_scr[...], wp_ref[...], preferred_element_type=_f32)
    else:
        nck = tmc // S5_CHUNK
        blk = nck * S5_SEGS
        a = jnp.concatenate(
            [jnp.concatenate([mix_ref[(8 * lt + gl) // 2, :, _chunk_col(8 * lt + gl, h):_chunk_col(8 * lt + gl, h) + LANES]
                              for gl in range(8)], axis=1)
             for lt in range(LANE_TILES) for h in range(2)], axis=0)
        for sp in range(4):
            out = jnp.dot(a, pt_ref[:, sp * MXU_EDGE:(sp + 1) * MXU_EDGE], preferred_element_type=_f32)
            for lt in range(LANE_TILES):
                for h in range(2):
                    for s in (2 * sp, 2 * sp + 1):
                        for c in range(nck):
                            tok = c * S5_CHUNK + 8 * h + s
                            row = (lt * 2 + h) * blk + c * S5_SEGS
                            nat_scr[lt, tok * S5_SEGS:(tok + 1) * S5_SEGS, :] = (
                                out[row:row + S5_SEGS, (s % 2) * LANES:(s % 2 + 1) * LANES])
        for seg in range(S5_SEGS):
            mod = mod_ref[seg]
            u = _norm_mod(x_ref[seg], gain_ref[0:1, :], mod[1:2, :], mod[0:1, :])
            y_ssm = jnp.concatenate([nat_scr[lt, pl.ds(seg, tmc, stride=S5_SEGS), :]
                                     for lt in range(LANE_TILES)], axis=1)
            h_scr[rows(seg), :] = _gelu_tanh(skip_ref[...] * u + y_ssm).astype(_bf16)
        g = h_scr[...]
        val = jnp.dot(g, wp_ref[:, :D_MODEL], preferred_element_type=_f32)
        gate = jnp.dot(g, wp_ref[:, D_MODEL:], preferred_element_type=_f32)
        y = val * jax.nn.sigmoid(gate)
    for seg in range(S5_SEGS):
        mod = mod_ref[seg]
        x1 = x_ref[seg] + mod[2:3, :] * _rms(y[rows(seg), :], gain_ref[1:2, :])
        x1_scr[rows(seg), :] = x1
        h_scr[rows(seg), :] = _norm_mod(x1, gain_ref[2:3, :], mod[4:5, :], mod[3:4, :]).astype(_bf16)
    h2 = h_scr[...]
    tf = 512
    for kf in range(D_FF // tf):
        cols = slice(kf * tf, (kf + 1) * tf)
        a = jnp.dot(h2, w1_ref[:, cols], preferred_element_type=_f32)
        a = jnp.square(jnp.maximum(a, 0.0)).astype(_bf16)
        part = jnp.dot(a, w2_ref[cols, :], preferred_element_type=_f32)
        if kf == 0:
            acc_ref[...] = part
        else:
            acc_ref[...] += part
    for seg in range(S5_SEGS):
        mod = mod_ref[seg]
        out = x1_scr[rows(seg), :] + mod[5:6, :] * _rms(acc_ref[rows(seg), :], gain_ref[3:4, :])
        o_ref[seg] = out
        if mixer == "attn":
            modn = modn_ref[seg]
            _chunk_stage(_norm_mod(out, gainn_ref[0:1, :], modn[1:2, :], modn[0:1, :]), seg, ubuf)
    if mixer == "attn":
        _chunk_emit(ubuf, perm_ref, u_ref)


def _post_mixer(mixer, x8, mix, mod8, gains, w_proj, w1, w2, skip=None, perm=None, next_mod8=None, next_gains=None,
                tmc=64):
    _, ls, _ = x8.shape
    tok = lambda j: (0, j, 0)
    const2 = lambda j: (0, 0)
    m = S5_SEGS * tmc
    chunk_rows = tmc // S5_CHUNK * S5_SEGS
    mod_spec = _resident((S5_SEGS, 6, D_MODEL), lambda j: (0, 0, 0))
    in_specs = [pl.BlockSpec((S5_SEGS, tmc, D_MODEL), tok)]
    scratch = [pltpu.VMEM((m, D_MODEL), _f32), pltpu.VMEM((m, D_MODEL), _bf16), pltpu.VMEM((m, D_MODEL), _f32)]
    out_specs = pl.BlockSpec((S5_SEGS, tmc, D_MODEL), tok)
    out_shape = jax.ShapeDtypeStruct(x8.shape, _f32)
    if mixer == "attn":
        in_specs.append(pl.BlockSpec((S5_SEGS, tmc, D_MODEL), tok))
        scratch.append(pltpu.VMEM((LANE_TILES, m, LANES), _f32))
        out_specs = (out_specs, pl.BlockSpec((S5_PAIRS, chunk_rows, PAIR_W), tok))
        out_shape = (out_shape, jax.ShapeDtypeStruct((S5_PAIRS, ls // S5_CHUNK * S5_SEGS, PAIR_W), _bf16))
    else:
        in_specs.append(pl.BlockSpec((S5_PAIRS, chunk_rows, PAIR_W), tok))
        scratch.append(pltpu.VMEM((LANE_TILES, m, LANES), _f32))
    in_specs += [mod_spec, _resident((4, D_MODEL), const2)]
    args = [x8, mix, mod8, gains]
    if mixer == "attn":
        in_specs += [mod_spec, _resident((4, D_MODEL), const2), _resident((D_MODEL, D_MODEL), const2)]
        args += [next_mod8, next_gains, perm]
    else:
        in_specs += [_resident((1, D_MODEL), const2), _resident((D_MODEL, D_MODEL), const2)]
        args += [skip, perm]
    in_specs += [_resident(w_proj.shape, const2), _resident(w1.shape, const2), _resident(w2.shape, const2)]
    args += [w_proj, w1, w2]
    return pl.pallas_call(
        functools.partial(_post_kernel, mixer),
        grid=(ls // tmc,),
        in_specs=in_specs,
        out_specs=out_specs,
        out_shape=out_shape,
        scratch_shapes=scratch,
        compiler_params=_cparams(("parallel",)),
        name="post_" + mixer,
    )(*args)


def _trunk(x, mod, params, chained):
    b, l, _ = x.shape
    assert (b == 1) if chained else (b == S5_SEGS)
    ls = b * l // S5_SEGS
    gains = params["norm_gain"]
    mod8 = jnp.broadcast_to(mod, (DEPTH, S5_SEGS, 6, D_MODEL))
    qkv = _qkv_proj(x, mod[0], gains[0], params["w_qkv"])
    att = _attention(qkv, params["attn_bias"])
    x8, u = _post_mixer("attn", x.reshape(S5_SEGS, ls, D_MODEL), att.reshape(S5_SEGS, ls, D_MODEL), mod8[0],
                        gains[0], params["w_o"], params["ffn_w1"][0], params["ffn_w2"][0],
                        perm=_lane_perm(), next_mod8=mod8[1], next_gains=gains[1])
    y = _s5_core(u, params["s5_tables"], chained)
    x8 = _post_mixer("s5", x8, y, mod8[1], gains[1], params["w_glu"], params["ffn_w1"][1], params["ffn_w2"][1],
                     skip=params["s5_d"], perm=_lane_perm(inverse=True))
    return x8.reshape(b, l, D_MODEL)


def kernel(x_prompt, x_sample, c_prompt, c_sample, ada_w, ada_b, norm_gain, attn_w_qkv, attn_w_o, attn_rpb,
           s5_a_re, s5_a_im, s5_log_dt, s5_b_re, s5_b_im, s5_c_re, s5_c_im, s5_d, s5_w_glu, ffn_w1, ffn_w2):
    nbp, nbs = c_prompt.shape[0], c_sample.shape[0]
    nb = -(-(nbp + nbs) // 8) * 8
    c_all = jnp.concatenate([c_prompt, c_sample, jnp.zeros((nb - nbp - nbs, D_MODEL), _f32)], axis=0)
    mod = _modulation(c_all, ada_w, ada_b)
    mod_p = mod[:, :nbp].reshape(DEPTH, nbp, 6, D_MODEL)
    mod_s = mod[:, nbp:nbp + nbs].reshape(DEPTH, nbs, 6, D_MODEL)

    params = {
        "norm_gain": norm_gain,
        "w_qkv": attn_w_qkv[0].astype(_bf16),
        "w_o": attn_w_o[0].astype(_bf16),
        "attn_bias": _attn_bias_table(attn_rpb[0]),
        "s5_tables": _s5_tables(s5_a_re[0], s5_a_im[0], s5_log_dt[0], s5_b_re[0], s5_b_im[0],
                                s5_c_re[0], s5_c_im[0]),
        "s5_d": s5_d[0].reshape(1, D_MODEL),
        "w_glu": s5_w_glu[0].astype(_bf16),
        "ffn_w1": ffn_w1.astype(_bf16),
        "ffn_w2": ffn_w2.astype(_bf16),
    }
    y_prompt = _trunk(x_prompt, mod_p, params, chained=False)
    y_sample = _trunk(x_sample, mod_s, params, chained=True)
    return (y_prompt, y_sample)
```

```python
import functools
import math

import numpy as np
import jax
import jax.numpy as jnp
from jax import lax
from jax.experimental import pallas as pl
from jax.experimental.pallas import tpu as pltpu

D_MODEL = 1024
D_FF = 4 * D_MODEL
DEPTH = 2
EPS = 1e-6
NEG_INF = -1e30
LOG2E = math.log2(math.e)

GRID_W = 64
N_HEADS = 16
HEAD_DIM = D_MODEL // N_HEADS
WIN_H = 8
WIN_W = 16
ROWS_PER_TILE = 4
TILE_TOK = ROWS_PER_TILE * GRID_W
KEY_TILES = 3
S5_GROUP = 16
S5_GROUPS = D_MODEL // S5_GROUP
S5_STATE = 64
S5_CHUNK = 16
S5_SEGS = 8
S5_PAIRS = S5_GROUPS // 2
PAIR_W = 2 * S5_CHUNK * S5_GROUP
STATE_W = 2 * S5_STATE
NPOW = 32

LANES = 128
MXU_EDGE = 256

VMEM_LIMIT = 62 * 1024 * 1024

_f32 = jnp.float32
_bf16 = jnp.bfloat16


def _cparams(sem):
    return pltpu.CompilerParams(dimension_semantics=sem, vmem_limit_bytes=VMEM_LIMIT)


def _resident(shape, index_map):
    return pl.BlockSpec(shape, index_map, pipeline_mode=pl.Buffered(1))


def _rms(x, gain):
    ms = jnp.mean(x * x, axis=-1, keepdims=True)
    return x * lax.rsqrt(ms + EPS) * gain


def _norm_mod(x, gain, scale, shift):
    return _rms(x, gain * (1.0 + scale)) + shift


def _mod_kernel(c_ref, w_ref, b_ref, o_ref):
    c = c_ref[...]
    act = c * jax.nn.sigmoid(c)
    o_ref[0] = jnp.dot(act, w_ref[0], preferred_element_type=_f32,
                       precision=lax.Precision.HIGHEST) + b_ref[0]


def _modulation(c_all, ada_w, ada_b):
    nb = c_all.shape[0]
    tn = 1536
    return pl.pallas_call(
        _mod_kernel,
        grid=(DEPTH, 6 * D_MODEL // tn),
        in_specs=[
            pl.BlockSpec((nb, D_MODEL), lambda i, n: (0, 0)),
            pl.BlockSpec((1, D_MODEL, tn), lambda i, n: (i, 0, n)),
            pl.BlockSpec((1, 1, tn), lambda i, n: (i, 0, n)),
        ],
        out_specs=pl.BlockSpec((1, nb, tn), lambda i, n: (i, 0, n)),
        out_shape=jax.ShapeDtypeStruct((DEPTH, nb, 6 * D_MODEL), _f32),
        compiler_params=_cparams(("arbitrary", "arbitrary")),
        name="adaln_mod",
    )(c_all, ada_w, ada_b.reshape(DEPTH, 1, 6 * D_MODEL))


def _qkv_kernel(x_ref, mod_ref, gain_ref, w_ref, o_ref):
    mod = mod_ref[0]
    h = _norm_mod(x_ref[0], gain_ref[0:1, :], mod[1:2, :], mod[0:1, :]).astype(_bf16)
    q_scale = HEAD_DIM ** -0.5 * LOG2E
    for part in range(3):
        cols = slice(part * D_MODEL, (part + 1) * D_MODEL)
        y = jnp.dot(h, w_ref[:, cols], preferred_element_type=_f32)
        if part == 0:
            y = y * q_scale
        o_ref[0, :, cols] = y.astype(_bf16)


def _qkv_proj(x, mod, gains, w_qkv, tm=1024):
    b, l, _ = x.shape
    return pl.pallas_call(
        _qkv_kernel,
        grid=(b, l // tm),
        in_specs=[
            pl.BlockSpec((1, tm, D_MODEL), lambda i, j: (i, j, 0)),
            pl.BlockSpec((1, 6, D_MODEL), lambda i, j: (i, 0, 0)),
            _resident((4, D_MODEL), lambda i, j: (0, 0)),
            _resident((D_MODEL, 3 * D_MODEL), lambda i, j: (0, 0)),
        ],
        out_specs=pl.BlockSpec((1, tm, 3 * D_MODEL), lambda i, j: (i, j, 0)),
        out_shape=jax.ShapeDtypeStruct((b, l, 3 * D_MODEL), _bf16),
        compiler_params=_cparams(("parallel", "parallel")),
        name="norm_qkv",
    )(x, mod, gains, w_qkv)


LANE_TILES = D_MODEL // LANES


def _lane_perm(inverse=False):
    eye = np.eye
    p = np.einsum('sS,gG,iI->sgiGSI', eye(8), eye(8), eye(S5_GROUP)).reshape(D_MODEL, D_MODEL)
    return jnp.asarray(p.T if inverse else p, _bf16)


def _chunk_col(g, h):
    return (g % 2) * (PAIR_W // 2) + h * LANES


def _chunk_stage(u, seg, ubuf):
    tmc = u.shape[0]
    for lt in range(LANE_TILES):
        ubuf[lt, seg * tmc:(seg + 1) * tmc, :] = u[:, lt * LANES:(lt + 1) * LANES]


def _chunk_emit(ubuf, p_ref, o_ref):
    tmc = ubuf.shape[1] // S5_SEGS
    nck = tmc // S5_CHUNK
    blk = nck * S5_SEGS
    a = jnp.concatenate(
        [jnp.concatenate([ubuf[lt, pl.ds(c * S5_CHUNK + 8 * h + s, S5_SEGS, stride=tmc), :]
                          for lt in range(LANE_TILES) for h in range(2) for c in range(nck)], axis=0)
         for s in range(8)], axis=1).astype(_bf16)
    for gp in range(4):
        out = jnp.dot(a, p_ref[:, gp * MXU_EDGE:(gp + 1) * MXU_EDGE], preferred_element_type=_f32)
        for lt in range(LANE_TILES):
            for h in range(2):
                for gl in (2 * gp, 2 * gp + 1):
                    g = 8 * lt + gl
                    col = _chunk_col(g, h)
                    piece = out[(lt * 2 + h) * blk:(lt * 2 + h + 1) * blk, (gl % 2) * LANES:(gl % 2 + 1) * LANES]
                    o_ref[g // 2, :, col:col + LANES] = piece.astype(o_ref.dtype)


N_REL_ROWS = 2 * WIN_H - 1


def _bias_rows():
    i = np.arange(ROWS_PER_TILE)[:, None]
    kk = np.arange(KEY_TILES * ROWS_PER_TILE)[None, :]
    dr = np.stack([kk - i + 7, kk - i + 3, kk - i - 1])
    row_ok = np.stack([(kk < WIN_H) & (i >= 0), (kk >= i) & (kk < i + WIN_H), (kk >= 4) & (i >= 0)])
    return np.where(row_ok, dr, N_REL_ROWS)


def _bias_kernel(rp_ref, o_ref):
    dr_tab = _bias_rows()
    pair_w = 2 * GRID_W
    qc = lax.broadcasted_iota(jnp.int32, (GRID_W, pair_w), 0)
    lane = lax.broadcasted_iota(jnp.int32, (GRID_W, pair_w), 1)
    kc = jnp.bitwise_and(lane, GRID_W - 1)
    col_start = jnp.clip(qc - WIN_W // 2, 0, GRID_W - WIN_W)
    col_ok = jnp.logical_and(kc >= col_start, kc < col_start + WIN_W)
    left_half = lane < GRID_W
    shift = pair_w - (WIN_W - 1)
    left, right = [], []
    for dr in range(N_REL_ROWS):
        base = jnp.broadcast_to(rp_ref[0, dr:dr + 1, :], (GRID_W, pair_w))
        left.append(pltpu.roll(base, shift, 1, stride=1, stride_axis=0))
        right.append(pltpu.roll(base, (shift + GRID_W) % pair_w, 1, stride=1, stride_axis=0))
    masked = jnp.full((GRID_W, pair_w), NEG_INF, _f32)
    left.append(masked)
    right.append(masked)
    for pt in range(dr_tab.shape[0]):
        for i in range(ROWS_PER_TILE):
            for pr in range(KEY_TILES * 2):
                blk = jnp.where(left_half, left[dr_tab[pt, i, 2 * pr]], right[dr_tab[pt, i, 2 * pr + 1]])
                o_ref[pt, 0, i, pr] = jnp.where(col_ok, blk, NEG_INF).astype(o_ref.dtype)


def _attn_bias_table(rpb):
    rel_rows_pad = -(-N_REL_ROWS // 8) * 8
    rp = jnp.pad(rpb.astype(_f32) * LOG2E,
                 ((0, 0), (0, rel_rows_pad - N_REL_ROWS), (0, 2 * GRID_W - (2 * WIN_W - 1))))
    shape = (3, N_HEADS, ROWS_PER_TILE, KEY_TILES * 2, GRID_W, 2 * GRID_W)
    return pl.pallas_call(
        _bias_kernel,
        grid=(N_HEADS,),
        in_specs=[pl.BlockSpec((1, rel_rows_pad, 2 * GRID_W), lambda h: (h, 0, 0))],
        out_specs=pl.BlockSpec((3, 1) + shape[2:], lambda h: (0, h, 0, 0, 0, 0)),
        out_shape=jax.ShapeDtypeStruct(shape, _bf16),
        compiler_params=_cparams(("parallel",)),
        name="attn_bias",
    )(rp)


def _attn_heads(interior, q_ref, k_refs, v_refs, bias_ref, o_ref):
    lane = lax.broadcasted_iota(jnp.int32, (1, LANES), 1)
    low = lane < HEAD_DIM
    head_mask = [low.astype(_bf16), (~low).astype(_bf16)]
    head_of_lane = jnp.right_shift(lax.broadcasted_iota(jnp.int32, (1, MXU_EDGE), 1), HEAD_DIM.bit_length() - 1)
    n_pairs = 2 * KEY_TILES
    zero_blk = jnp.zeros((GRID_W, 2 * GRID_W), _bf16)
    heads_per_slice = MXU_EDGE // HEAD_DIM
    for hq in range(N_HEADS // heads_per_slice):
        vcols = slice(hq * MXU_EDGE, (hq + 1) * MXU_EDGE)
        acc = None
        for h4 in range(heads_per_slice):
            head = heads_per_slice * hq + h4
            cols = slice((head // 2) * LANES, (head // 2 + 1) * LANES)
            qm = q_ref[0, :, cols] * head_mask[head % 2]
            s = [lax.dot_general(qm, k_refs[t][0, :, cols], (((1,), (1,)), ((), ())),
                                 preferred_element_type=_f32) for t in range(KEY_TILES)]
            p_rows, inv_l = [], []
            for i in range(ROWS_PER_TILE):
                pairs = range(i // 2, (i + WIN_H - 1) // 2 + 1) if interior else range(n_pairs)
                rows = slice(i * GRID_W, (i + 1) * GRID_W)
                blk = {pr: s[pr // 2][rows, (pr % 2) * LANES:(pr % 2 + 1) * LANES]
                       + bias_ref[0, head, i, pr].astype(_f32) for pr in pairs}
                m = None
                for pr in pairs:
                    m = blk[pr] if m is None else jnp.maximum(m, blk[pr])
                m = jnp.max(m, axis=-1, keepdims=True)
                e = {pr: jnp.exp2(blk[pr] - m) for pr in pairs}
                tot = None
                for pr in pairs:
                    tot = e[pr] if tot is None else tot + e[pr]
                inv_l.append(1.0 / jnp.sum(tot, axis=-1, keepdims=True))
                p_rows.append([e[pr].astype(_bf16) if pr in e else zero_blk for pr in range(n_pairs)])
            o = None
            for t in range(KEY_TILES):
                p_t = jnp.concatenate(
                    [jnp.concatenate([p_rows[i][2 * t], p_rows[i][2 * t + 1]], axis=1)
                     for i in range(ROWS_PER_TILE)], axis=0)
                part = jnp.dot(p_t, v_refs[t][0, :, vcols], preferred_element_type=_f32)
                o = part if o is None else o + part
            o = o * jnp.concatenate(inv_l, axis=0)
            acc = o if acc is None else jnp.where(head_of_lane == h4, o, acc)
        o_ref[0, :, vcols] = acc.astype(o_ref.dtype)


def _attn_kernel(q_ref, k0_ref, k1_ref, k2_ref, v0_ref, v1_ref, v2_ref, bias_ref, o_ref):
    j = pl.program_id(1)
    interior = jnp.logical_and(j > 0, j < pl.num_programs(1) - 1)
    args = (q_ref, (k0_ref, k1_ref, k2_ref), (v0_ref, v1_ref, v2_ref), bias_ref, o_ref)
    pl.when(interior)(functools.partial(_attn_heads, True, *args))
    pl.when(jnp.logical_not(interior))(functools.partial(_attn_heads, False, *args))


def _attention(qkv, bias):
    b, l, _ = qkv.shape
    nt = l // TILE_TOK
    assert nt >= KEY_TILES

    def base(j):
        return jnp.clip(j - 1, 0, nt - KEY_TILES)

    def pattern(j):
        return jnp.where(j == 0, 0, jnp.where(j == nt - 1, 2, 1))

    def kv_spec(part, t):
        return pl.BlockSpec((1, TILE_TOK, D_MODEL), lambda i, j: (i, base(j) + t, part))

    return pl.pallas_call(
        _attn_kernel,
        grid=(b, nt),
        in_specs=[pl.BlockSpec((1, TILE_TOK, D_MODEL), lambda i, j: (i, j, 0))]
        + [kv_spec(1, t) for t in range(KEY_TILES)]
        + [kv_spec(2, t) for t in range(KEY_TILES)]
        + [pl.BlockSpec((1, N_HEADS, ROWS_PER_TILE, KEY_TILES * 2, GRID_W, 2 * GRID_W),
                        lambda i, j: (pattern(j), 0, 0, 0, 0, 0))],
        out_specs=pl.BlockSpec((1, TILE_TOK, D_MODEL), lambda i, j: (i, j, 0)),
        out_shape=jax.ShapeDtypeStruct((b, l, D_MODEL), _bf16),
        compiler_params=_cparams(("parallel", "arbitrary")),
        name="nbr_attention",
    )(qkv, qkv, qkv, qkv, qkv, qkv, qkv, bias)


def _s5_tables(a_re, a_im, log_dt, b_re, b_im, c_re, c_im):
    f = lambda v: v.astype(_f32)
    pairs = (2, S5_PAIRS)
    a_row = [f(a).reshape(*pairs, 1, STATE_W) for a in (a_re, a_im)]
    a_col = [f(a).reshape(*pairs, STATE_W, 1) for a in (a_re, a_im)]
    ldt = jnp.repeat(f(log_dt), S5_STATE, axis=-1)
    ldt_row = ldt.reshape(*pairs, 1, STATE_W)
    ldt_col = ldt.reshape(*pairs, STATE_W, 1)
    b_t = [jnp.transpose(f(b).reshape(*pairs, 2, S5_STATE, S5_GROUP), (0, 1, 4, 2, 3))
           .reshape(*pairs, S5_GROUP, STATE_W) for b in (b_re, b_im)]
    c_t = [jnp.transpose(f(c), (0, 1, 3, 2)).reshape(*pairs, STATE_W, S5_GROUP) for c in (c_re, c_im)]

    def spec(shape):
        nd = len(shape)
        return pl.BlockSpec((2, 1) + tuple(shape[2:]), lambda p: (0, p) + (0,) * (nd - 2))

    ins = a_row + a_col + [ldt_row, ldt_col] + b_t + c_t
    out_shapes = (
        jax.ShapeDtypeStruct((S5_GROUPS, PAIR_W // 2, PAIR_W // 2), _bf16),
        jax.ShapeDtypeStruct((2, S5_PAIRS, PAIR_W, 2 * STATE_W), _bf16),
        jax.ShapeDtypeStruct((2, S5_PAIRS, 2 * STATE_W, PAIR_W), _bf16),
        jax.ShapeDtypeStruct((2, S5_PAIRS, S5_SEGS, 2 * STATE_W), _f32),
    )
    return pl.pallas_call(
        _s5_tables_kernel,
        grid=(S5_PAIRS,),
        in_specs=[spec(v.shape) for v in ins],
        out_specs=(pl.BlockSpec((2, PAIR_W // 2, PAIR_W // 2), lambda p: (p, 0, 0)),
                   spec(out_shapes[1].shape), spec(out_shapes[2].shape), spec(out_shapes[3].shape)),
        out_shape=out_shapes,
        compiler_params=_cparams(("parallel",)),
        name="s5_tables",
    )(*ins)


def _cexp(k, a_re, a_im, dt):
    mag = jnp.exp(k * (a_re * dt))
    ang = k * (a_im * dt)
    return mag * jnp.cos(ang), mag * jnp.sin(ang)


def _s5_tables_kernel(ar_re_ref, ar_im_ref, ac_re_ref, ac_im_ref, ldr_ref, ldc_ref,
                      bt_re_ref, bt_im_ref, ct_re_ref, ct_im_ref, m_ref, w_ref, v_ref, lam_ref):
    hi = lax.Precision.HIGHEST
    t = S5_CHUNK
    cw = t * S5_GROUP
    lane_c = lax.broadcasted_iota(jnp.int32, (1, cw), 1)
    log_group = S5_GROUP.bit_length() - 1
    lag = jnp.right_shift(lane_c, log_group).astype(_f32)
    tok_row = jnp.right_shift(lax.broadcasted_iota(jnp.int32, (cw, 1), 0), log_group).astype(_f32)
    low = lax.broadcasted_iota(jnp.int32, (1, STATE_W), 1) < S5_STATE
    tile_i = (lax.broadcasted_iota(jnp.int32, (S5_GROUP, cw), 0)
              == jnp.bitwise_and(lax.broadcasted_iota(jnp.int32, (S5_GROUP, cw), 1), S5_GROUP - 1)).astype(_f32)
    k_sub = jnp.minimum(lax.broadcasted_iota(jnp.int32, (NPOW, 1), 0), t).astype(_f32)
    k_lane = jnp.minimum(lax.broadcasted_iota(jnp.int32, (1, STATE_W), 1), t).astype(_f32)
    pow_of_row = lax.broadcasted_iota(jnp.int32, (cw, NPOW), 1).astype(_f32)
    pow_of_lane = lax.broadcasted_iota(jnp.int32, (STATE_W, cw), 0).astype(_f32)

    def spread(a, b, data_left=True):
        data, sel = (a, b) if data_left else (b, a)
        top = data.astype(_bf16)
        rest = data - top.astype(_f32)
        mid = rest.astype(_bf16)
        low = (rest - mid.astype(_f32)).astype(_bf16)
        sel = sel.astype(_bf16)
        parts = [jnp.dot(p, sel, preferred_element_type=_f32) if data_left
                 else jnp.dot(sel, p, preferred_element_type=_f32) for p in (top, mid, low)]
        return (parts[0] + parts[1]) + parts[2]

    g_lag = [[None, None], [None, None]]
    for d in range(2):
        a_re, a_im = ar_re_ref[d, 0], ar_im_ref[d, 0]
        pw_re, pw_im = _cexp(k_sub, a_re, a_im, jnp.exp(ldr_ref[d, 0]))
        lam_re, lam_im = pw_re[1:2], pw_im[1:2]
        den = a_re * a_re + a_im * a_im
        nr, ni = lam_re - 1.0, lam_im
        zr = (nr * a_re + ni * a_im) / den
        zi = (ni * a_re - nr * a_im) / den
        bb_re = zr * bt_re_ref[d, 0] - zi * bt_im_ref[d, 0]
        bb_im = zr * bt_im_ref[d, 0] + zi * bt_re_ref[d, 0]
        sel_tok = (pow_of_row == ((t - 1.0) - tok_row if d == 0 else tok_row)).astype(_f32)
        p_re = spread(sel_tok, pw_re, data_left=False)
        p_im = spread(sel_tok, pw_im, data_left=False)
        bt_re = jnp.concatenate([bb_re] * t, axis=0)
        bt_im = jnp.concatenate([bb_im] * t, axis=0)
        w_re, w_im = _cmul(p_re, p_im, bt_re, bt_im)
        top = jnp.concatenate([jnp.where(low, w_re, 0.0), jnp.where(low, w_im, 0.0)], axis=1)
        bot = jnp.concatenate([jnp.where(low, 0.0, w_re), jnp.where(low, 0.0, w_im)], axis=1)
        w_ref[d, 0] = jnp.concatenate([top, bot], axis=0).astype(w_ref.dtype)
        lam_ref[d, 0] = jnp.broadcast_to(jnp.concatenate([pw_re[t:t + 1], pw_im[t:t + 1]], axis=1),
                                         (S5_SEGS, 2 * STATE_W))
        pc_re, pc_im = _cexp(k_lane, ac_re_ref[d, 0], ac_im_ref[d, 0], jnp.exp(ldc_ref[d, 0]))
        ct_re = spread(ct_re_ref[d, 0], tile_i)
        ct_im = spread(ct_im_ref[d, 0], tile_i)

        def c_pow(k_of_block):
            sel = (pow_of_lane == k_of_block).astype(_f32)
            return _cmul(ct_re, ct_im, spread(pc_re, sel), spread(pc_im, sel))

        vb_re, vb_im = c_pow(lag + 1.0 if d == 0 else float(t) - lag)
        zero = jnp.zeros((S5_STATE, cw), _f32)
        v_ref[d, 0] = jnp.concatenate([
            jnp.concatenate([vb_re[:S5_STATE], zero], axis=1), jnp.concatenate([zero, vb_re[S5_STATE:]], axis=1),
            jnp.concatenate([-vb_im[:S5_STATE], zero], axis=1), jnp.concatenate([zero, -vb_im[S5_STATE:]], axis=1),
        ], axis=0).astype(v_ref.dtype)
        ce_re, ce_im = c_pow(lag if d == 0 else (t - 1.0) - lag)
        for g2 in range(2):
            sel = low if g2 == 0 else ~low
            g_lag[d][g2] = (
                jnp.dot(jnp.where(sel, bb_re, 0.0), ce_re, preferred_element_type=_f32, precision=hi)
                - jnp.dot(jnp.where(sel, bb_im, 0.0), ce_im, preferred_element_type=_f32, precision=hi))
    for g2 in range(2):
        rows = []
        for s in range(t):
            fwd = g_lag[0][g2] if s == 0 else pltpu.roll(g_lag[0][g2], S5_GROUP * s, 1)
            fwd = jnp.where(lane_c >= S5_GROUP * s, fwd, 0.0)
            shift = (cw - S5_GROUP * (t - 1 - s)) % cw
            bwd = g_lag[1][g2] if shift == 0 else pltpu.roll(g_lag[1][g2], shift, 1)
            bwd = jnp.where(lane_c < S5_GROUP * (s + 1), bwd, 0.0)
            rows.append(fwd + bwd)
        m_ref[g2] = jnp.concatenate(rows, axis=0).astype(m_ref.dtype)


def _cmul(ar, ai, br, bi):
    return ar * br - ai * bi, ar * bi + ai * br


S5_PAIRS_PER_STEP = 2


def _s5_kernel(chained, u_ref, m_ref, w_ref, v_ref, lam_ref, y_ref, x_scr, h_scr):
    rows = u_ref.shape[1]
    nc = rows // S5_SEGS
    rb = min(512, rows)
    sw = STATE_W
    chains = [(q, d) for q in range(S5_PAIRS_PER_STEP) for d in range(2)]

    for q in range(S5_PAIRS_PER_STEP):
        for r in range(rows // rb):
            rs = slice(r * rb, (r + 1) * rb)
            u = u_ref[q, rs, :]
            for d in range(2):
                x_scr[q, d, rs, :] = jnp.dot(u, w_ref[d, q], preferred_element_type=_f32)

    lam = {(q, d): (lam_ref[d, q, :, :sw], lam_ref[d, q, :, sw:]) for q, d in chains}

    def row_of(d, c):
        cc = c if d == 0 else nc - 1 - c
        return pl.ds(pl.multiple_of(cc * S5_SEGS, S5_SEGS), S5_SEGS)

    def scan(init, store):
        def body(c, carry):
            new = []
            for (q, d), (hr, hi) in zip(chains, carry):
                rs = row_of(d, c)
                if store:
                    h_scr[q, d, rs, :sw] = hr
                    h_scr[q, d, rs, sw:] = hi
                pr, pi = _cmul(lam[q, d][0], lam[q, d][1], hr, hi)
                new.append((pr + x_scr[q, d, rs, :sw], pi + x_scr[q, d, rs, sw:]))
            return tuple(new)

        return lax.fori_loop(0, nc, body, init, unroll=4)

    zero = jnp.zeros((S5_SEGS, sw), _f32)
    start = tuple((zero, zero) for _ in chains)
    if not chained:
        scan(start, store=True)
    else:
        ends = scan(start, store=False)
        one = jnp.ones((S5_SEGS, sw), _f32)
        h_in = []
        for (q, d), (er, ei) in zip(chains, ends):
            dr, di = one[0:1], zero[0:1]
            br, bi = lam[q, d][0][0:1], lam[q, d][1][0:1]
            n = nc
            while n:
                if n & 1:
                    dr, di = _cmul(dr, di, br, bi)
                n >>= 1
                if n:
                    br, bi = _cmul(br, bi, br, bi)
            order = range(S5_SEGS) if d == 0 else range(S5_SEGS - 1, -1, -1)
            cr = jnp.zeros((1, sw), _f32)
            ci = jnp.zeros((1, sw), _f32)
            rows_r = [None] * S5_SEGS
            rows_i = [None] * S5_SEGS
            for sgm in order:
                rows_r[sgm], rows_i[sgm] = cr, ci
                pr, pi = _cmul(dr, di, cr, ci)
                cr, ci = pr + er[sgm:sgm + 1], pi + ei[sgm:sgm + 1]
            h_in.append((jnp.concatenate(rows_r, axis=0), jnp.concatenate(rows_i, axis=0)))
        scan(tuple(h_in), store=True)

    half = PAIR_W // 2

    for q in range(S5_PAIRS_PER_STEP):
        for r in range(rows // rb):
            rs = slice(r * rb, (r + 1) * rb)
            u = u_ref[q, rs, :]
            y = (jnp.dot(h_scr[q, 0, rs, :].astype(_bf16), v_ref[0, q], preferred_element_type=_f32)
                 + jnp.dot(h_scr[q, 1, rs, :].astype(_bf16), v_ref[1, q], preferred_element_type=_f32))
            y0 = y[:, :half] + jnp.dot(u[:, :half], m_ref[2 * q], preferred_element_type=_f32)
            y1 = y[:, half:] + jnp.dot(u[:, half:], m_ref[2 * q + 1], preferred_element_type=_f32)
            y_ref[q, rs, :half] = y0.astype(y_ref.dtype)
            y_ref[q, rs, half:] = y1.astype(y_ref.dtype)


def _s5_core(u_t, tables, chained):
    m_sum, w_in, v_out, lam_t = tables
    _, rows, _ = u_t.shape
    nq = S5_PAIRS_PER_STEP
    return pl.pallas_call(
        functools.partial(_s5_kernel, chained),
        grid=(S5_PAIRS // nq,),
        in_specs=[
            pl.BlockSpec((nq, rows, PAIR_W), lambda p: (p, 0, 0)),
            pl.BlockSpec((2 * nq, PAIR_W // 2, PAIR_W // 2), lambda p: (p, 0, 0)),
            pl.BlockSpec((2, nq, PAIR_W, 2 * STATE_W), lambda p: (0, p, 0, 0)),
            pl.BlockSpec((2, nq, 2 * STATE_W, PAIR_W), lambda p: (0, p, 0, 0)),
            pl.BlockSpec((2, nq, S5_SEGS, 2 * STATE_W), lambda p: (0, p, 0, 0)),
        ],
        out_specs=pl.BlockSpec((nq, rows, PAIR_W), lambda p: (p, 0, 0)),
        out_shape=jax.ShapeDtypeStruct((S5_PAIRS, rows, PAIR_W), _bf16),
        scratch_shapes=[pltpu.VMEM((nq, 2, rows, 2 * STATE_W), _f32),
                        pltpu.VMEM((nq, 2, rows, 2 * STATE_W), _f32)],
        compiler_params=_cparams(("parallel",)),
        name="s5_core",
    )(u_t, m_sum, w_in, v_out, lam_t)


def _gelu_tanh(x):
    return 0.5 * x * (1.0 + jnp.tanh(math.sqrt(2.0 / math.pi) * (x + 0.044715 * (x * x * x))))


def _post_kernel(mixer, *refs):
    if mixer == "attn":
        (x_ref, mix_ref, mod_ref, gain_ref, modn_ref, gainn_ref, perm_ref, wp_ref, w1_ref, w2_ref, o_ref, u_ref,
         x1_scr, h_scr, acc_ref, ubuf) = refs
    else:
        (x_ref, mix_ref, mod_ref, gain_ref, skip_ref, pt_ref, wp_ref, w1_ref, w2_ref, o_ref,
         x1_scr, h_scr, acc_ref, nat_scr) = refs
    tmc = x_ref.shape[1]

    def rows(seg):
        return slice(seg * tmc, (seg + 1) * tmc)

    if mixer == "attn":
        for seg in range(S5_SEGS):
            h_scr[rows(seg), :] = mix_ref[seg]
        y = jnp.dot(h_scr[...], wp_ref[...], preferred_element_type=_f32)
    else:
        nck = tmc // S5_CHUNK
        blk = nck * S5_SEGS
        a = jnp.concatenate(
            [jnp.concatenate([mix_ref[(8 * lt + gl) // 2, :, _chunk_col(8 * lt + gl, h):_chunk_col(8 * lt + gl, h) + LANES]
                              for gl in range(8)], axis=1)
             for lt in range(LANE_TILES) for h in range(2)], axis=0)
        for sp in range(4):
            out = jnp.dot(a, pt_ref[:, sp * MXU_EDGE:(sp + 1) * MXU_EDGE], preferred_element_type=_f32)
            for lt in range(LANE_TILES):
                for h in range(2):
                    for s in (2 * sp, 2 * sp + 1):
                        for c in range(nck):
                            tok = c * S5_CHUNK + 8 * h + s
                            row = (lt * 2 + h) * blk + c * S5_SEGS
                            nat_scr[lt, tok * S5_SEGS:(tok + 1) * S5_SEGS, :] = (
                                out[row:row + S5_SEGS, (s % 2) * LANES:(s % 2 + 1) * LANES])
        for seg in range(S5_SEGS):
            mod = mod_ref[seg]
            u = _norm_mod(x_ref[seg], gain_ref[0:1, :], mod[1:2, :], mod[0:1, :])
            y_ssm = jnp.concatenate([nat_scr[lt, pl.ds(seg, tmc, stride=S5_SEGS), :]
                                     for lt in range(LANE_TILES)], axis=1)
            h_scr[rows(seg), :] = _gelu_tanh(skip_ref[...] * u + y_ssm).astype(_bf16)
        g = h_scr[...]
        val = jnp.dot(g, wp_ref[:, :D_MODEL], preferred_element_type=_f32)
        gate = jnp.dot(g, wp_ref[:, D_MODEL:], preferred_element_type=_f32)
        y = val * jax.nn.sigmoid(gate)
    for seg in range(S5_SEGS):
        mod = mod_ref[seg]
        x1 = x_ref[seg] + mod[2:3, :] * _rms(y[rows(seg), :], gain_ref[1:2, :])
        x1_scr[rows(seg), :] = x1
        h_scr[rows(seg), :] = _norm_mod(x1, gain_ref[2:3, :], mod[4:5, :], mod[3:4, :]).astype(_bf16)
    h2 = h_scr[...]
    tf = 512
    for kf in range(D_FF // tf):
        cols = slice(kf * tf, (kf + 1) * tf)
        a = jnp.dot(h2, w1_ref[:, cols], preferred_element_type=_f32)
        a = jnp.square(jnp.maximum(a, 0.0)).astype(_bf16)
        part = jnp.dot(a, w2_ref[cols, :], preferred_element_type=_f32)
        if kf == 0:
            acc_ref[...] = part
        else:
            acc_ref[...] += part
    for seg in range(S5_SEGS):
        mod = mod_ref[seg]
        out = x1_scr[rows(seg), :] + mod[5:6, :] * _rms(acc_ref[rows(seg), :], gain_ref[3:4, :])
        o_ref[seg] = out
        if mixer == "attn":
            modn = modn_ref[seg]
            _chunk_stage(_norm_mod(out, gainn_ref[0:1, :], modn[1:2, :], modn[0:1, :]), seg, ubuf)
    if mixer == "attn":
        _chunk_emit(ubuf, perm_ref, u_ref)


def _post_mixer(mixer, x8, mix, mod8, gains, w_proj, w1, w2, skip=None, perm=None, next_mod8=None, next_gains=None,
                tmc=128):
    _, ls, _ = x8.shape
    tok = lambda j: (0, j, 0)
    const2 = lambda j: (0, 0)
    m = S5_SEGS * tmc
    chunk_rows = tmc // S5_CHUNK * S5_SEGS
    mod_spec = _resident((S5_SEGS, 6, D_MODEL), lambda j: (0, 0, 0))
    in_specs = [pl.BlockSpec((S5_SEGS, tmc, D_MODEL), tok)]
    scratch = [pltpu.VMEM((m, D_MODEL), _f32), pltpu.VMEM((m, D_MODEL), _bf16), pltpu.VMEM((m, D_MODEL), _f32)]
    out_specs = pl.BlockSpec((S5_SEGS, tmc, D_MODEL), tok)
    out_shape = jax.ShapeDtypeStruct(x8.shape, _f32)
    if mixer == "attn":
        in_specs.append(pl.BlockSpec((S5_SEGS, tmc, D_MODEL), tok))
        scratch.append(pltpu.VMEM((LANE_TILES, m, LANES), _f32))
        out_specs = (out_specs, pl.BlockSpec((S5_PAIRS, chunk_rows, PAIR_W), tok))
        out_shape = (out_shape, jax.ShapeDtypeStruct((S5_PAIRS, ls // S5_CHUNK * S5_SEGS, PAIR_W), _bf16))
    else:
        in_specs.append(pl.BlockSpec((S5_PAIRS, chunk_rows, PAIR_W), tok))
        scratch.append(pltpu.VMEM((LANE_TILES, m, LANES), _f32))
    in_specs += [mod_spec, _resident((4, D_MODEL), const2)]
    args = [x8, mix, mod8, gains]
    if mixer == "attn":
        in_specs += [mod_spec, _resident((4, D_MODEL), const2), _resident((D_MODEL, D_MODEL), const2)]
        args += [next_mod8, next_gains, perm]
    else:
        in_specs += [_resident((1, D_MODEL), const2), _resident((D_MODEL, D_MODEL), const2)]
        args += [skip, perm]
    in_specs += [_resident(w_proj.shape, const2), _resident(w1.shape, const2), _resident(w2.shape, const2)]
    args += [w_proj, w1, w2]
    return pl.pallas_call(
        functools.partial(_post_kernel, mixer),
        grid=(ls // tmc,),
        in_specs=in_specs,
        out_specs=out_specs,
        out_shape=out_shape,
        scratch_shapes=scratch,
        compiler_params=_cparams(("parallel",)),
        name="post_" + mixer,
    )(*args)


def _trunk(x, mod, params, chained):
    b, l, _ = x.shape
    assert (b == 1) if chained else (b == S5_SEGS)
    ls = b * l // S5_SEGS
    gains = params["norm_gain"]
    mod8 = jnp.broadcast_to(mod, (DEPTH, S5_SEGS, 6, D_MODEL))
    qkv = _qkv_proj(x, mod[0], gains[0], params["w_qkv"])
    att = _attention(qkv, params["attn_bias"])
    x8, u = _post_mixer("attn", x.reshape(S5_SEGS, ls, D_MODEL), att.reshape(S5_SEGS, ls, D_MODEL), mod8[0],
                        gains[0], params["w_o"], params["ffn_w1"][0], params["ffn_w2"][0],
                        perm=_lane_perm(), next_mod8=mod8[1], next_gains=gains[1])
    y = _s5_core(u, params["s5_tables"], chained)
    x8 = _post_mixer("s5", x8, y, mod8[1], gains[1], params["w_glu"], params["ffn_w1"][1], params["ffn_w2"][1],
                     skip=params["s5_d"], perm=_lane_perm(inverse=True))
    return x8.reshape(b, l, D_MODEL)


def kernel(x_prompt, x_sample, c_prompt, c_sample, ada_w, ada_b, norm_gain, attn_w_qkv, attn_w_o, attn_rpb,
           s5_a_re, s5_a_im, s5_log_dt, s5_b_re, s5_b_im, s5_c_re, s5_c_im, s5_d, s5_w_glu, ffn_w1, ffn_w2):
    nbp, nbs = c_prompt.shape[0], c_sample.shape[0]
    nb = -(-(nbp + nbs) // 8) * 8
    c_all = jnp.concatenate([c_prompt, c_sample, jnp.zeros((nb - nbp - nbs, D_MODEL), _f32)], axis=0)
    mod = _modulation(c_all, ada_w, ada_b)
    mod_p = mod[:, :nbp].reshape(DEPTH, nbp, 6, D_MODEL)
    mod_s = mod[:, nbp:nbp + nbs].reshape(DEPTH, nbs, 6, D_MODEL)

    params = {
        "norm_gain": norm_gain,
        "w_qkv": attn_w_qkv[0].astype(_bf16),
        "w_o": attn_w_o[0].astype(_bf16),
        "attn_bias": _attn_bias_table(attn_rpb[0]),
        "s5_tables": _s5_tables(s5_a_re[0], s5_a_im[0], s5_log_dt[0], s5_b_re[0], s5_b_im[0],
                                s5_c_re[0], s5_c_im[0]),
        "s5_d": s5_d[0].reshape(1, D_MODEL),
        "w_glu": s5_w_glu[0].astype(_bf16),
        "ffn_w1": ffn_w1.astype(_bf16),
        "ffn_w2": ffn_w2.astype(_bf16),
    }
    y_prompt = _trunk(x_prompt, mod_p, params, chained=False)
    y_sample = _trunk(x_sample, mod_s, params, chained=True)
    return (y_prompt, y_sample)
```
